```python
import math
import jax, jax.numpy as jnp
from jax import lax
import numpy as np

D_MODEL = 1024
BATCH = 4
SEQ = 4096
DEPTH = 2
DEC_BATCH = 128
DEC_SEQ = 1
PAST_LEN = 2048
PAGE_SIZE = 128

F32 = jnp.float32
N_BRANCH = 3
BRANCH_W = D_MODEL // 2
LRU_BLOCKS = 8
LRU_BW = BRANCH_W // LRU_BLOCKS
LRU_CONV = 4
LRU_C = 8.0
RWKV_HEAD = 64
RWKV_HEADS = BRANCH_W // RWKV_HEAD
DECAY_LORA = 64
AAA_LORA = 64
GATE_LORA = 128
RWKV_COLS = 3 * BRANCH_W + DECAY_LORA + AAA_LORA + GATE_LORA
RWKV_GN_EPS = 64e-5
ATT_HEAD = 64
ATT_HEADS = BRANCH_W // ATT_HEAD
MOBA_BLOCK = 256
MOBA_TOPK = 3
Q_CHUNK = 64
N_BUCKETS = 32
MAX_DISTANCE = 128
D_FF = 3 * D_MODEL
FFN_CONV = 3
PLE_DIM = 256
RMS_EPS = 1e-6
N_IN = 2 * BRANCH_W + RWKV_COLS + 3 * BRANCH_W + N_BRANCH * D_MODEL

kernel_name = 'hybrid_lru_rwkv7_moba_step'


def rms_norm(x, g):
    xf = x.astype(F32)
    y = xf * lax.rsqrt(jnp.mean(xf * xf, axis=-1, keepdims=True) + RMS_EPS)
    return (y * g.astype(F32)).astype(x.dtype)


def causal_dwconv(x, buf, w, b):
    width = w.shape[0]
    T = x.shape[1]
    xx = jnp.concatenate([buf.astype(x.dtype), x], axis=1)
    y = xx[:, 0:T] * w[0]
    for j in range(1, width):
        y = y + xx[:, j:j + T] * w[j]
    return y + b, xx[:, xx.shape[1] - (width - 1):]


def _lin_combine(left, right):
    a_l, b_l = left
    a_r, b_r = right
    return a_l * a_r, a_r * b_l + b_r


def rglru_branch(cols, conv_buf, h0, lp):
    B, T, _ = cols.shape
    xa, ga = jnp.split(cols, 2, axis=-1)
    xc, conv_new = causal_dwconv(xa, conv_buf, lp['lru_conv_w'], lp['lru_conv_b'])
    xh = xc.reshape(B, T, LRU_BLOCKS, LRU_BW)
    r = jax.nn.sigmoid(jnp.einsum('btgi,gij->btgj', xh, lp['lru_w_r']).reshape(B, T, BRANCH_W) + lp['lru_b_r'])
    ig = jax.nn.sigmoid(jnp.einsum('btgi,gij->btgj', xh, lp['lru_w_i']).reshape(B, T, BRANCH_W) + lp['lru_b_i'])
    log_a = (-LRU_C * r.astype(F32)) * jax.nn.softplus(-lp['lru_lambda'].astype(F32))
    a = jnp.exp(log_a)
    bx = jnp.sqrt(-jnp.expm1(2.0 * log_a)) * (ig * xc).astype(F32)
    a_cum, b_cum = lax.associative_scan(_lin_combine, (a, bx), axis=1)
    h = a_cum * h0.astype(F32)[:, None, :] + b_cum
    y = h * jax.nn.gelu(ga.astype(F32))
    return y, conv_new, h[:, -1]


def _rwkv_step(S, inp):
    r, w, k, v, a, b = inp
    sa = jnp.einsum('bhij,bhj->bhi', S, a)
    S = S * w[:, :, None, :] + sa[..., None] * b[:, :, None, :] + v[..., None] * k[:, :, None, :]
    return S, jnp.einsum('bhij,bhj->bhi', S, r)


def rwkv7_branch(cols, shift_prev, s0, lp):
    B, T, _ = cols.shape
    H, N = RWKV_HEADS, RWKV_HEAD
    cols = cols.astype(F32)
    prev = jnp.concatenate([shift_prev.astype(F32)[:, None, :], cols[:, :-1]], axis=1)
    mixed = cols + (prev - cols) * lp['rwkv_mu']
    r, k, v, wl, al, gl = jnp.split(mixed, [BRANCH_W, 2 * BRANCH_W, 3 * BRANCH_W,
                                            3 * BRANCH_W + DECAY_LORA,
                                            3 * BRANCH_W + DECAY_LORA + AAA_LORA], axis=-1)
    w = -jax.nn.softplus(-(lp['rwkv_w0'] + jnp.tanh(wl) @ lp['rwkv_w2'])) - 0.5
    decay = jnp.exp(-jnp.exp(w))
    a = jax.nn.sigmoid(lp['rwkv_a0'] + al @ lp['rwkv_a2'])
    g = jax.nn.sigmoid(gl) @ lp['rwkv_g2']
    heads = lambda t: t.reshape(B, T, H, N)
    kk = heads(k * lp['rwkv_k_k'])
    kk = kk / jnp.maximum(jnp.sqrt(jnp.sum(kk * kk, axis=-1, keepdims=True)), 1e-12)
    k = heads(k * (1.0 + (a - 1.0) * lp['rwkv_k_a']))
    r, v, decay, a = heads(r), heads(v), heads(decay), heads(a)
    tm = lambda t: t.transpose(1, 0, 2, 3)
    s_final, ys = lax.scan(_rwkv_step, s0.astype(F32),
                           (tm(r), tm(decay), tm(k), tm(v), tm(-kk), tm(kk * a)))
    y = tm(ys)
    mu = jnp.mean(y, axis=-1, keepdims=True)
    var = jnp.mean(jnp.square(y - mu), axis=-1, keepdims=True)
    y = ((y - mu) * lax.rsqrt(var + RWKV_GN_EPS)).reshape(B, T, BRANCH_W) * lp['rwkv_ln_w'] + lp['rwkv_ln_b']
    bonus = jnp.sum(r * k * lp['rwkv_r_k'], axis=-1, keepdims=True) * v
    y = (y + bonus.reshape(B, T, BRANCH_W)) * g
    return y, cols[:, -1], s_final


def t5_bucket(dist):
    n = jnp.maximum(dist, 0)
    max_exact = N_BUCKETS // 2
    nf = jnp.maximum(n, 1).astype(F32)
    large = max_exact + (jnp.log(nf / max_exact) / math.log(MAX_DISTANCE / max_exact)
                         * (N_BUCKETS - max_exact)).astype(jnp.int32)
    large = jnp.minimum(large, N_BUCKETS - 1)
    return jnp.where(n < max_exact, n, large)


def moba_attention(q, k_new, v_new, k_past, v_past, rel_bias):
    B, T, H, hd = q.shape
    P = k_past.shape[1]
    k = jnp.concatenate([k_past.astype(k_new.dtype), k_new], axis=1)
    v = jnp.concatenate([v_past.astype(v_new.dtype), v_new], axis=1)
    L = P + T
    n_full = L // MOBA_BLOCK
    kmean = k[:, :n_full * MOBA_BLOCK].astype(F32).reshape(B, n_full, MOBA_BLOCK, H, hd).mean(axis=2)
    n_cand = max(n_full, MOBA_TOPK)
    kmean = jnp.pad(kmean, ((0, 0), (0, n_cand - n_full), (0, 0), (0, 0)))
    qc = Q_CHUNK if T % Q_CHUNK == 0 else T
    n_chunks = T // qc
    qs = q.astype(F32).reshape(B, n_chunks, qc, H, hd).transpose(1, 0, 3, 2, 4)
    b_idx = jnp.arange(B)[:, None, None, None, None]
    h_idx = jnp.arange(H)[None, :, None, None, None]
    offs = jnp.arange(MOBA_BLOCK)
    scale = hd ** -0.5
    bias_table = rel_bias.astype(F32).T

    def attend_chunk(args):
        qch, c = args
        qpos = P + c * qc + jnp.arange(qc)
        own = qpos // MOBA_BLOCK
        blk_score = jnp.einsum('bhqd,bnhd->bhqn', qch, kmean)
        past_ok = jnp.arange(n_cand)[None, :] < own[:, None]
        blk_score = jnp.where(past_ok, blk_score, -jnp.inf)
        top_score, top_idx = lax.top_k(blk_score, MOBA_TOPK)
        own_b = jnp.broadcast_to(own[None, None, :, None], (B, H, qc, 1))
        sel = jnp.concatenate([top_idx, own_b], axis=-1)
        sel_ok = jnp.concatenate([top_score > -jnp.inf, jnp.ones((B, H, qc, 1), bool)], axis=-1)
        kpos = sel[..., None] * MOBA_BLOCK + offs
        kidx = jnp.minimum(kpos, L - 1)
        kg = k[b_idx, kidx, h_idx].astype(F32)
        vg = v[b_idx, kidx, h_idx].astype(F32)
        dist = qpos[:, None, None] - kpos
        logits = (jnp.einsum('bhqd,bhqskd->bhqsk', qch, kg) * scale
                  + bias_table[h_idx, t5_bucket(dist)])
        mask = sel_ok[..., None] & (dist >= 0)
        logits = jnp.where(mask, logits, -jnp.inf).reshape(B, H, qc, -1)
        probs = jax.nn.softmax(logits, axis=-1)
        return jnp.einsum('bhqn,bhqnd->bhqd', probs, vg.reshape(B, H, qc, -1, hd))

    out = lax.map(attend_chunk, (qs, jnp.arange(n_chunks)))
    return out.transpose(1, 0, 3, 2, 4).reshape(B, T, H * hd).astype(q.dtype)


def trunk_layer(x, p, st, lp, rel_bias):
    B, T, _ = x.shape
    xn = rms_norm(x, lp['ln1'])
    proj = xn @ lp['w_in']
    o1 = 2 * BRANCH_W
    o2 = o1 + RWKV_COLS
    o3 = o2 + 3 * BRANCH_W
    y_a, lru_conv_new, lru_h_new = rglru_branch(proj[..., :o1], st['lru_conv'], st['lru_h'], lp)
    y_b, shift_new, rwkv_new = rwkv7_branch(proj[..., o1:o2], st['rwkv_shift'], st['rwkv'], lp)
    q, k, v = jnp.split(proj[..., o2:o3], 3, axis=-1)
    q = q.reshape(B, T, ATT_HEADS, ATT_HEAD)
    k = k.reshape(B, T, ATT_HEADS, ATT_HEAD)
    v = v.reshape(B, T, ATT_HEADS, ATT_HEAD)
    y_c = moba_attention(q, k, v, st['k_past'], st['v_past'], rel_bias)
    branches = jnp.stack([y_a.astype(F32), y_b.astype(F32), y_c.astype(F32)], axis=2)
    gates = jax.nn.sigmoid(proj[..., o3:].reshape(B, T, N_BRANCH, D_MODEL))
    merged = jnp.sum(jnp.einsum('btgc,gcd->btgd', branches, lp['w_branch']) * gates, axis=2)
    x = x + merged @ lp['w_out']
    h = rms_norm(x, lp['ln2'])
    u = h @ lp['ffn_up']
    uc, ffn_conv_new = causal_dwconv(u, st['ffn_conv'], lp['ffn_conv_w'], lp['ffn_conv_b'])
    gt, up = jnp.split(uc, 2, axis=-1)
    x = x + (jax.nn.gelu(gt) * up) @ lp['ffn_down']
    x = x + jax.nn.sigmoid(rms_norm(x, lp['ln3']) @ lp['ple_gate']) * (p @ lp['ple_proj'])
    new = {'k': k, 'v': v, 'lru_h': lru_h_new, 'lru_conv': lru_conv_new,
           'rwkv': rwkv_new, 'rwkv_shift': shift_new, 'ffn_conv': ffn_conv_new}
    return x, new


def _stack(outs, name):
    return jnp.stack([o[name] for o in outs])


def setup_inputs(seed: int = 0) -> dict:
    key = jax.random.key(seed)
    ks = jax.random.split(key, 48)
    n_pages = PAST_LEN // PAGE_SIZE
    n_pool = (5 * DEC_BATCH * n_pages) // 4
    nrm = jax.random.normal

    def w(k, shape, fan_in):
        return nrm(k, shape, F32) * (fan_in ** -0.5)

    def gain(k, shape):
        return 1.0 + 0.02 * nrm(k, shape, F32)

    u = jax.random.uniform(ks[20], (DEPTH, BRANCH_W), F32, 0.9, 0.999)
    a_base = u ** (1.0 / LRU_C)
    lru_lambda = jnp.log(a_base) - jnp.log1p(-a_base)
    page_table = jax.random.permutation(ks[9], n_pool)[:DEC_BATCH * n_pages].reshape(DEC_BATCH, n_pages).astype(jnp.int32)
    return {
        'x_prompt': nrm(ks[0], (BATCH, SEQ, D_MODEL), F32),
        'x_sample': nrm(ks[1], (DEC_BATCH, DEC_SEQ, D_MODEL), F32),
        'cache_k': nrm(ks[2], (DEPTH, n_pool, PAGE_SIZE, ATT_HEADS, ATT_HEAD), F32),
        'cache_v': nrm(ks[3], (DEPTH, n_pool, PAGE_SIZE, ATT_HEADS, ATT_HEAD), F32),
        'state_lru_h': 0.5 * nrm(ks[4], (DEPTH, DEC_BATCH, BRANCH_W), F32),
        'state_lru_conv': nrm(ks[5], (DEPTH, DEC_BATCH, LRU_CONV - 1, BRANCH_W), F32),
        'state_rwkv': 0.3 * nrm(ks[6], (DEPTH, DEC_BATCH, RWKV_HEADS, RWKV_HEAD, RWKV_HEAD), F32),
        'state_rwkv_shift': nrm(ks[7], (DEPTH, DEC_BATCH, RWKV_COLS), F32),
        'state_ffn_conv': nrm(ks[8], (DEPTH, DEC_BATCH, FFN_CONV - 1, 2 * D_FF), F32),
        'page_table': page_table,
        'p_prompt': nrm(ks[10], (DEPTH, BATCH, SEQ, PLE_DIM), F32),
        'p_sample': nrm(ks[11], (DEPTH, DEC_BATCH, DEC_SEQ, PLE_DIM), F32),
        'ln1': gain(ks[12], (DEPTH, D_MODEL)),
        'w_in': w(ks[13], (DEPTH, D_MODEL, N_IN), D_MODEL),
        'lru_conv_w': w(ks[14], (DEPTH, LRU_CONV, BRANCH_W), LRU_CONV),
        'lru_conv_b': 0.02 * nrm(ks[15], (DEPTH, BRANCH_W), F32),
        'lru_w_r': w(ks[16], (DEPTH, LRU_BLOCKS, LRU_BW, LRU_BW), LRU_BW),
        'lru_b_r': 0.02 * nrm(ks[17], (DEPTH, BRANCH_W), F32),
        'lru_w_i': w(ks[18], (DEPTH, LRU_BLOCKS, LRU_BW, LRU_BW), LRU_BW),
        'lru_b_i': 0.02 * nrm(ks[19], (DEPTH, BRANCH_W), F32),
        'lru_lambda': lru_lambda,
        'rwkv_mu': jax.random.uniform(ks[21], (DEPTH, RWKV_COLS), F32),
        'rwkv_w0': jax.random.uniform(ks[22], (DEPTH, BRANCH_W), F32, -6.0, 1.0),
        'rwkv_w2': 0.1 * w(ks[23], (DEPTH, DECAY_LORA, BRANCH_W), DECAY_LORA),
        'rwkv_a0': 0.1 * nrm(ks[24], (DEPTH, BRANCH_W), F32),
        'rwkv_a2': 0.5 * w(ks[25], (DEPTH, AAA_LORA, BRANCH_W), AAA_LORA),
        'rwkv_g2': w(ks[26], (DEPTH, GATE_LORA, BRANCH_W), GATE_LORA),
        'rwkv_k_k': 0.85 + 0.02 * nrm(ks[27], (DEPTH, BRANCH_W), F32),
        'rwkv_k_a': gain(ks[28], (DEPTH, BRANCH_W)),
        'rwkv_r_k': 0.1 * nrm(ks[29], (DEPTH, RWKV_HEADS, RWKV_HEAD), F32),
        'rwkv_ln_w': gain(ks[30], (DEPTH, BRANCH_W)),
        'rwkv_ln_b': 0.02 * nrm(ks[31], (DEPTH, BRANCH_W), F32),
        'rel_bias': 0.5 * nrm(ks[32], (N_BUCKETS, ATT_HEADS), F32),
        'w_branch': w(ks[33], (DEPTH, N_BRANCH, BRANCH_W, D_MODEL), BRANCH_W),
        'w_out': w(ks[34], (DEPTH, D_MODEL, D_MODEL), D_MODEL),
        'ln2': gain(ks[35], (DEPTH, D_MODEL)),
        'ffn_up': w(ks[36], (DEPTH, D_MODEL, 2 * D_FF), D_MODEL),
        'ffn_conv_w': w(ks[37], (DEPTH, FFN_CONV, 2 * D_FF), FFN_CONV),
        'ffn_conv_b': 0.02 * nrm(ks[38], (DEPTH, 2 * D_FF), F32),
        'ffn_down': w(ks[39], (DEPTH, D_FF, D_MODEL), D_FF),
        'ln3': gain(ks[40], (DEPTH, D_MODEL)),
        'ple_gate': w(ks[41], (DEPTH, D_MODEL, D_MODEL), D_MODEL),
        'ple_proj': w(ks[42], (DEPTH, PLE_DIM, D_MODEL), PLE_DIM),
        'ln_f': gain(ks[43], (D_MODEL,)),
    }


def reference(x_prompt, x_sample, cache_k, cache_v, state_lru_h, state_lru_conv, state_rwkv,
              state_rwkv_shift, state_ffn_conv, page_table, p_prompt, p_sample,
              ln1, w_in, lru_conv_w, lru_conv_b, lru_w_r, lru_b_r, lru_w_i, lru_b_i, lru_lambda,
              rwkv_mu, rwkv_w0, rwkv_w2, rwkv_a0, rwkv_a2, rwkv_g2, rwkv_k_k, rwkv_k_a, rwkv_r_k,
              rwkv_ln_w, rwkv_ln_b, rel_bias, w_branch, w_out, ln2, ffn_up, ffn_conv_w, ffn_conv_b,
              ffn_down, ln3, ple_gate, ple_proj, ln_f):
    dt = x_prompt.dtype
    B = x_prompt.shape[0]
    DB = x_sample.shape[0]
    past = page_table.shape[1] * cache_k.shape[2]
    xp, xs = x_prompt, x_sample
    outs_p, outs_s = [], []
    for i in range(DEPTH):
        lp = {'ln1': ln1[i], 'w_in': w_in[i], 'lru_conv_w': lru_conv_w[i], 'lru_conv_b': lru_conv_b[i],
              'lru_w_r': lru_w_r[i], 'lru_b_r': lru_b_r[i], 'lru_w_i': lru_w_i[i], 'lru_b_i': lru_b_i[i],
              'lru_lambda': lru_lambda[i], 'rwkv_mu': rwkv_mu[i], 'rwkv_w0': rwkv_w0[i],
              'rwkv_w2': rwkv_w2[i], 'rwkv_a0': rwkv_a0[i], 'rwkv_a2': rwkv_a2[i], 'rwkv_g2': rwkv_g2[i],
              'rwkv_k_k': rwkv_k_k[i], 'rwkv_k_a': rwkv_k_a[i], 'rwkv_r_k': rwkv_r_k[i],
              'rwkv_ln_w': rwkv_ln_w[i], 'rwkv_ln_b': rwkv_ln_b[i], 'w_branch': w_branch[i],
              'w_out': w_out[i], 'ln2': ln2[i], 'ffn_up': ffn_up[i], 'ffn_conv_w': ffn_conv_w[i],
              'ffn_conv_b': ffn_conv_b[i], 'ffn_down': ffn_down[i], 'ln3': ln3[i],
              'ple_gate': ple_gate[i], 'ple_proj': ple_proj[i]}
        st_p = {'k_past': jnp.zeros((B, 0, ATT_HEADS, ATT_HEAD), dt),
                'v_past': jnp.zeros((B, 0, ATT_HEADS, ATT_HEAD), dt),
                'lru_h': jnp.zeros((B, BRANCH_W), dt),
                'lru_conv': jnp.zeros((B, LRU_CONV - 1, BRANCH_W), dt),
                'rwkv': jnp.zeros((B, RWKV_HEADS, RWKV_HEAD, RWKV_HEAD), dt),
                'rwkv_shift': jnp.zeros((B, RWKV_COLS), dt),
                'ffn_conv': jnp.zeros((B, FFN_CONV - 1, 2 * D_FF), dt)}
        st_s = {'k_past': cache_k[i][page_table].reshape(DB, past, ATT_HEADS, ATT_HEAD),
                'v_past': cache_v[i][page_table].reshape(DB, past, ATT_HEADS, ATT_HEAD),
                'lru_h': state_lru_h[i], 'lru_conv': state_lru_conv[i], 'rwkv': state_rwkv[i],
                'rwkv_shift': state_rwkv_shift[i], 'ffn_conv': state_ffn_conv[i]}
        xp, new_p = trunk_layer(xp, p_prompt[i], st_p, lp, rel_bias)
        xs, new_s = trunk_layer(xs, p_sample[i], st_s, lp, rel_bias)
        outs_p.append(new_p)
        outs_s.append(new_s)
    y_prompt = rms_norm(xp, ln_f).astype(dt)
    y_sample = rms_norm(xs, ln_f).astype(dt)
    new_k_prompt = _stack(outs_p, 'k')
    new_k_sample = _stack(outs_s, 'k')
    new_v_prompt = _stack(outs_p, 'v')
    new_v_sample = _stack(outs_s, 'v')
    new_lru_h_prompt = _stack(outs_p, 'lru_h')
    new_lru_h_sample = _stack(outs_s, 'lru_h')
    new_lru_conv_prompt = _stack(outs_p, 'lru_conv')
    new_lru_conv_sample = _stack(outs_s, 'lru_conv')
    new_rwkv_prompt = _stack(outs_p, 'rwkv')
    new_rwkv_sample = _stack(outs_s, 'rwkv')
    new_rwkv_shift_prompt = _stack(outs_p, 'rwkv_shift')
    new_rwkv_shift_sample = _stack(outs_s, 'rwkv_shift')
    new_ffn_conv_prompt = _stack(outs_p, 'ffn_conv')
    new_ffn_conv_sample = _stack(outs_s, 'ffn_conv')
    return (y_prompt, y_sample, new_k_prompt, new_k_sample, new_v_prompt, new_v_sample,
            new_lru_h_prompt, new_lru_h_sample, new_lru_conv_prompt, new_lru_conv_sample,
            new_rwkv_prompt, new_rwkv_sample, new_rwkv_shift_prompt, new_rwkv_shift_sample,
            new_ffn_conv_prompt, new_ffn_conv_sample)
```

```python
import functools
import math

import numpy as np
import jax
import jax.numpy as jnp
from jax import lax
from jax.experimental import pallas as pl
from jax.experimental.pallas import tpu as pltpu

F32 = jnp.float32
BF16 = jnp.bfloat16
I32 = jnp.int32

D_MODEL = 1024
DEPTH = 2
PAGE_SIZE = 128
N_BRANCH = 3
BRANCH_W = D_MODEL // 2
LRU_BLOCKS = 8
LRU_CONV = 4
LRU_C = 8.0
RWKV_HEAD = 64
RWKV_HEADS = BRANCH_W // RWKV_HEAD
DECAY_LORA = 64
AAA_LORA = 64
GATE_LORA = 128
RWKV_COLS = 3 * BRANCH_W + DECAY_LORA + AAA_LORA + GATE_LORA
RWKV_GN_EPS = 64e-5
ATT_HEAD = 64
ATT_HEADS = BRANCH_W // ATT_HEAD
MOBA_BLOCK = 256
MOBA_TOPK = 3
N_BUCKETS = 32
MAX_DISTANCE = 128
D_FF = 3 * D_MODEL
FFN_CONV = 3
PLE_DIM = 256
RMS_EPS = 1e-6
N_IN = 2 * BRANCH_W + RWKV_COLS + 3 * BRANCH_W + N_BRANCH * D_MODEL

_SEG_EDGES = (0, 2 * BRANCH_W, 2 * BRANCH_W + RWKV_COLS, 2 * BRANCH_W + RWKV_COLS + BRANCH_W,
              2 * BRANCH_W + RWKV_COLS + 2 * BRANCH_W, 2 * BRANCH_W + RWKV_COLS + 3 * BRANCH_W, N_IN)
_IN_SEGS = tuple(zip(_SEG_EDGES[:-1], _SEG_EDGES[1:]))

LANES = 128
SUBLANES = 8
VMEM_LIMIT = 56 * 1024 * 1024
NEG = -1e30
LORA_W = DECAY_LORA + AAA_LORA


def _params(n_axes):
    return pltpu.CompilerParams(dimension_semantics=("arbitrary",) * n_axes, vmem_limit_bytes=VMEM_LIMIT)


def _const(shape):
    return pl.BlockSpec(shape, lambda *_: (0,) * len(shape), pipeline_mode=pl.Buffered(1))


def _softplus(x):
    return jnp.maximum(x, 0.0) + jnp.log1p(jnp.exp(-jnp.abs(x)))


def _gelu_tanh(x):
    return 0.5 * x * (1.0 + jnp.tanh(math.sqrt(2.0 / math.pi) * (x + 0.044715 * (x * x * x))))


def _rms(x, g):
    return x * lax.rsqrt(jnp.mean(x * x, axis=-1, keepdims=True) + RMS_EPS) * g


def _dot(a, b):
    return jnp.dot(a, b, preferred_element_type=F32)


def _dot_nt(a, b, precision=None):
    return lax.dot_general(a, b, (((1,), (1,)), ((), ())), precision=precision, preferred_element_type=F32)


def _head_sum(x, ones_bd):
    hi = x.astype(BF16)
    lo = (x - hi.astype(F32)).astype(BF16)
    return _dot(hi, ones_bd) + _dot(lo, ones_bd)


def _in_proj_kernel(x_ref, g_ref, w_ref, *out_refs):
    xn = _rms(x_ref[...], g_ref[...]).astype(BF16)
    for ref, (lo, hi) in zip(out_refs, _IN_SEGS):
        ref[...] = _dot(xn, w_ref[:, lo:hi])


def _in_proj(x, g, w_bf16, tm):
    m = x.shape[0]
    widths = [hi - lo for lo, hi in _IN_SEGS]
    return pl.pallas_call(
        _in_proj_kernel,
        grid=(m // tm,),
        in_specs=[pl.BlockSpec((tm, D_MODEL), lambda i: (i, 0)), _const((1, D_MODEL)), _const((D_MODEL, N_IN))],
        out_specs=[pl.BlockSpec((tm, w), lambda i: (i, 0)) for w in widths],
        out_shape=[jax.ShapeDtypeStruct((m, w), F32) for w in widths],
        compiler_params=_params(1),
        name="in_proj",
    )(x, g, w_bf16)


def _lru_gates(xc, ga_unused, wr, br, wi, bi, lam):
    xcb = xc.astype(BF16)
    r = jax.nn.sigmoid(_dot(xcb, wr) + br)
    ig = jax.nn.sigmoid(_dot(xcb, wi) + bi)
    log_a = (-LRU_C * r) * _softplus(-lam)
    a = jnp.exp(log_a)
    bx = jnp.sqrt(1.0 - jnp.exp(2.0 * log_a)) * (ig * xc)
    return a, bx


def _lru_seq_kernel(cols_ref, cw_ref, cb_ref, wr_ref, br_ref, wi_ref, bi_ref, lam_ref,
                    y_ref, conv_ref, h_ref, ext_ref, hc_ref, *, tc):
    t = pl.program_id(1)

    @pl.when(t == 0)
    def _():
        ext_ref[0:SUBLANES, :] = jnp.zeros((SUBLANES, BRANCH_W), F32)
        hc_ref[...] = jnp.zeros_like(hc_ref)

    @pl.when(t > 0)
    def _():
        ext_ref[0:SUBLANES, :] = ext_ref[tc:tc + SUBLANES, :]

    xa = cols_ref[0, :, 0:BRANCH_W]
    ga = cols_ref[0, :, BRANCH_W:]
    ext_ref[SUBLANES:, :] = xa
    xc = ext_ref[pl.ds(SUBLANES - 3, tc), :] * cw_ref[0:1, :]
    xc = xc + ext_ref[pl.ds(SUBLANES - 2, tc), :] * cw_ref[1:2, :]
    xc = xc + ext_ref[pl.ds(SUBLANES - 1, tc), :] * cw_ref[2:3, :]
    xc = xc + xa * cw_ref[3:4, :] + cb_ref[...]
    a, bx = _lru_gates(xc, ga, wr_ref[...], br_ref[...], wi_ref[...], bi_ref[...], lam_ref[...])
    row = lax.broadcasted_iota(I32, (tc, BRANCH_W), 0)
    d = 1
    while d < tc:
        keep = row >= d
        a_s = jnp.where(keep, pltpu.roll(a, d, 0), 1.0)
        b_s = jnp.where(keep, pltpu.roll(bx, d, 0), 0.0)
        bx = a * b_s + bx
        a = a * a_s
        d *= 2
    h = a * hc_ref[...] + bx
    hc_ref[...] = h[tc - 1:tc, :]
    y_ref[0] = h * _gelu_tanh(ga)
    conv_ref[0] = ext_ref[tc + SUBLANES - 3:tc + SUBLANES, :]
    h_ref[0] = h[tc - 1:tc, :]


def _lru_seq(cols, lw, tc):
    b, t, _ = cols.shape
    vec = _const((1, BRANCH_W))
    mat = _const((BRANCH_W, BRANCH_W))
    return pl.pallas_call(
        functools.partial(_lru_seq_kernel, tc=tc),
        grid=(b, t // tc),
        in_specs=[pl.BlockSpec((1, tc, 2 * BRANCH_W), lambda i, j: (i, j, 0)),
                  _const((LRU_CONV, BRANCH_W)), vec, mat, vec, mat, vec, vec],
        out_specs=[pl.BlockSpec((1, tc, BRANCH_W), lambda i, j: (i, j, 0)),
                   pl.BlockSpec((1, LRU_CONV - 1, BRANCH_W), lambda i, j: (i, 0, 0)),
                   pl.BlockSpec((1, 1, BRANCH_W), lambda i, j: (i, 0, 0))],
        out_shape=[jax.ShapeDtypeStruct((b, t, BRANCH_W), F32),
                   jax.ShapeDtypeStruct((b, LRU_CONV - 1, BRANCH_W), F32),
                   jax.ShapeDtypeStruct((b, 1, BRANCH_W), F32)],
        scratch_shapes=[pltpu.VMEM((tc + SUBLANES, BRANCH_W), F32), pltpu.VMEM((1, BRANCH_W), F32)],
        compiler_params=_params(2),
        name="lru_seq",
    )(cols, lw["conv_w"], lw["conv_b"], lw["w_r"], lw["b_r"], lw["w_i"], lw["b_i"], lw["lam"])


def _lru_step_kernel(cols_ref, buf_ref, h0_ref, cw_ref, cb_ref, wr_ref, br_ref, wi_ref, bi_ref, lam_ref,
                     y_ref, conv_ref, h_ref):
    xa = cols_ref[:, 0:BRANCH_W]
    ga = cols_ref[:, BRANCH_W:]
    xc = buf_ref[:, 0:BRANCH_W] * cw_ref[0:1, :]
    xc = xc + buf_ref[:, BRANCH_W:2 * BRANCH_W] * cw_ref[1:2, :]
    xc = xc + buf_ref[:, 2 * BRANCH_W:] * cw_ref[2:3, :]
    xc = xc + xa * cw_ref[3:4, :] + cb_ref[...]
    a, bx = _lru_gates(xc, ga, wr_ref[...], br_ref[...], wi_ref[...], bi_ref[...], lam_ref[...])
    h = a * h0_ref[...] + bx
    y_ref[...] = h * _gelu_tanh(ga)
    conv_ref[:, 0:2 * BRANCH_W] = buf_ref[:, BRANCH_W:]
    conv_ref[:, 2 * BRANCH_W:] = xa
    h_ref[...] = h


def _lru_step(cols, buf, h0, lw):
    n = cols.shape[0]
    full = lambda w: pl.BlockSpec((n, w), lambda i: (0, 0))
    vec = _const((1, BRANCH_W))
    mat = _const((BRANCH_W, BRANCH_W))
    return pl.pallas_call(
        _lru_step_kernel,
        grid=(1,),
        in_specs=[full(2 * BRANCH_W), full(3 * BRANCH_W), full(BRANCH_W),
                  _const((LRU_CONV, BRANCH_W)), vec, mat, vec, mat, vec, vec],
        out_specs=[full(BRANCH_W), full(3 * BRANCH_W), full(BRANCH_W)],
        out_shape=[jax.ShapeDtypeStruct((n, BRANCH_W), F32), jax.ShapeDtypeStruct((n, 3 * BRANCH_W), F32),
                   jax.ShapeDtypeStruct((n, BRANCH_W), F32)],
        compiler_params=_params(1),
        name="lru_step",
    )(cols, buf, h0, lw["conv_w"], lw["conv_b"], lw["w_r"], lw["b_r"], lw["w_i"], lw["b_i"], lw["lam"])


def _rwkv_token_math(cols, prev, mu, w0, w2p, a0, a2p, g2, k_k, k_a, r_k, ones_bd):
    mixed = cols + (prev - cols) * mu
    r = mixed[:, 0:BRANCH_W]
    k = mixed[:, BRANCH_W:2 * BRANCH_W]
    v = mixed[:, 2 * BRANCH_W:3 * BRANCH_W]
    la = mixed[:, 3 * BRANCH_W:3 * BRANCH_W + LORA_W]
    gl = mixed[:, 3 * BRANCH_W + LORA_W:]
    w = -_softplus(-(w0 + _dot(jnp.tanh(la).astype(BF16), w2p))) - 0.5
    decay = jnp.exp(-jnp.exp(w))
    a = jax.nn.sigmoid(a0 + _dot(la.astype(BF16), a2p))
    g = _dot(jax.nn.sigmoid(gl).astype(BF16), g2)
    kk = k * k_k
    kk = kk / jnp.maximum(jnp.sqrt(_head_sum(kk * kk, ones_bd)), 1e-12)
    k = k * (1.0 + (a - 1.0) * k_a)
    bonus = _head_sum(r * k * r_k, ones_bd) * v
    return r, decay, k, v, -kk, kk * a, g, bonus


_RW_PARAM_ORDER = ("mu", "w0", "w2p", "a0", "a2p", "g2", "k_k", "k_a", "r_k", "ones_bd")


def _rw_param_specs():
    vec = _const((1, BRANCH_W))
    return [_const((1, RWKV_COLS)), vec, _const((LORA_W, BRANCH_W)), vec, _const((LORA_W, BRANCH_W)),
            _const((GATE_LORA, BRANCH_W)), vec, vec, vec, _const((BRANCH_W, BRANCH_W))]


def _rwkv_pre_seq_kernel(cols_ref, *refs, tc):
    prm = [r[...] for r in refs[:10]]
    outs = refs[10:18]
    ext_ref = refs[18]
    t = pl.program_id(1)

    @pl.when(t == 0)
    def _():
        ext_ref[0:SUBLANES, :] = jnp.zeros((SUBLANES, RWKV_COLS), F32)

    @pl.when(t > 0)
    def _():
        ext_ref[0:SUBLANES, :] = ext_ref[tc:tc + SUBLANES, :]

    cols = cols_ref[0]
    ext_ref[SUBLANES:, :] = cols
    prev = ext_ref[pl.ds(SUBLANES - 1, tc), :]
    for ref, val in zip(outs, _rwkv_token_math(cols, prev, *prm)):
        ref[0] = val


def _rwkv_pre_seq(cols, rw, tc):
    b, t, _ = cols.shape
    blk = pl.BlockSpec((1, tc, BRANCH_W), lambda i, j: (i, j, 0))
    return pl.pallas_call(
        functools.partial(_rwkv_pre_seq_kernel, tc=tc),
        grid=(b, t // tc),
        in_specs=[pl.BlockSpec((1, tc, RWKV_COLS), lambda i, j: (i, j, 0))] + _rw_param_specs(),
        out_specs=[blk] * 8,
        out_shape=[jax.ShapeDtypeStruct((b, t, BRANCH_W), F32)] * 8,
        scratch_shapes=[pltpu.VMEM((tc + SUBLANES, RWKV_COLS), F32)],
        compiler_params=_params(2),
        name="rwkv_pre_seq",
    )(cols, *[rw[k] for k in _RW_PARAM_ORDER])


def _rwkv_pre_step_kernel(cols_ref, prev_ref, *refs):
    prm = [r[...] for r in refs[:10]]
    for ref, val in zip(refs[10:18], _rwkv_token_math(cols_ref[...], prev_ref[...], *prm)):
        ref[...] = val


def _rwkv_pre_step(cols, prev, rw):
    n = cols.shape[0]
    full = lambda w: pl.BlockSpec((n, w), lambda i: (0, 0))
    return pl.pallas_call(
        _rwkv_pre_step_kernel,
        grid=(1,),
        in_specs=[full(RWKV_COLS), full(RWKV_COLS)] + _rw_param_specs(),
        out_specs=[full(BRANCH_W)] * 8,
        out_shape=[jax.ShapeDtypeStruct((n, BRANCH_W), F32)] * 8,
        compiler_params=_params(1),
        name="rwkv_pre_step",
    )(cols, prev, *[rw[k] for k in _RW_PARAM_ORDER])


def _rwkv_scan_kernel(r_ref, w_ref, k_ref, a_ref, b_ref, v_ref, s0_ref, y_ref, so_ref, s_ref, *, tc, ni):
    t = pl.program_id(1)

    @pl.when(t == 0)
    def _():
        s_ref[...] = s0_ref[0]

    def step(tt, carry):
        a = a_ref[0, tt]
        b = b_ref[0, tt]
        w = w_ref[0, tt]
        k = k_ref[0, tt]
        r = r_ref[0, tt]
        for i in range(ni):
            s = s_ref[i]
            sa = jnp.sum(s * a, axis=0, keepdims=True)
            vi = v_ref[0, tt, pl.ds(i, 1), :]
            s = s * w + sa * b + vi * k
            s_ref[i] = s
            y_ref[0, tt, pl.ds(i, 1), :] = jnp.sum(s * r, axis=0, keepdims=True)
        return carry

    lax.fori_loop(0, tc, step, 0)

    @pl.when(t == pl.num_programs(1) - 1)
    def _():
        so_ref[0] = s_ref[...]


def _rwkv_scan(r, w, k, a, b, v, s0, tc):
    g, t, nj, _ = r.shape
    ni = v.shape[2]
    vec = pl.BlockSpec((1, tc, nj, LANES), lambda i, j: (i, j, 0, 0))
    row = pl.BlockSpec((1, tc, ni, LANES), lambda i, j: (i, j, 0, 0))
    st = pl.BlockSpec((1, ni, nj, LANES), lambda i, j: (i, 0, 0, 0))
    return pl.pallas_call(
        functools.partial(_rwkv_scan_kernel, tc=tc, ni=ni),
        grid=(g, t // tc),
        in_specs=[vec] * 5 + [row, st],
        out_specs=[row, st],
        out_shape=[jax.ShapeDtypeStruct((g, t, ni, LANES), F32), jax.ShapeDtypeStruct((g, ni, nj, LANES), F32)],
        scratch_shapes=[pltpu.VMEM((ni, nj, LANES), F32)],
        compiler_params=_params(2),
        name="rwkv_scan",
    )(r, w, k, a, b, v, s0)


def _t5_bucket_np(dist):
    n = np.maximum(dist, 0)
    max_exact = N_BUCKETS // 2
    nf = np.maximum(n, 1).astype(np.float32)
    large = max_exact + (np.log(nf / np.float32(max_exact)) / np.float32(math.log(MAX_DISTANCE / max_exact))
                         * np.float32(N_BUCKETS - max_exact)).astype(np.int32)
    large = np.minimum(large, N_BUCKETS - 1)
    return np.where(n < max_exact, n, large).astype(np.int32)


def _moba_seq_kernel(tab_ref, bko_ref, bkp_ref, q_ref, k_ref, v_ref, o_ref, km_ref, bias_ref, *, nblk):
    bi = pl.program_id(0)
    i = pl.program_id(1)
    blk = MOBA_BLOCK
    far_bucket = N_BUCKETS - 1

    @pl.when((bi == 0) & (i == 0))
    def _():
        bko = bko_ref[...]
        bkp = bkp_ref[...]
        for h in range(ATT_HEADS):
            c = tab_ref[far_bucket, h]
            own = jnp.zeros((blk, blk), F32)
            prev = jnp.zeros((blk, blk), F32)
            for j in range(N_BUCKETS):
                val = tab_ref[j, h] - c
                own = jnp.where(bko == j, val, own)
                prev = jnp.where(bkp == j, val, prev)
            bias_ref[h, 0] = jnp.where(bko < 0, NEG, own)
            bias_ref[h, 1] = prev

    @pl.when(i == 0)
    def _():
        lane = lax.broadcasted_iota(I32, (1, LANES), 1)
        for h in range(ATT_HEADS):
            km_ref[h] = jnp.zeros((LANES, LANES), F32)
            for n in range(nblk):
                ks = jnp.sum(k_ref[0, h, n * blk:(n + 1) * blk, :].astype(F32), axis=0, keepdims=True)
                km_ref[h, ATT_HEAD + n:ATT_HEAD + n + 1, :] = jnp.where(lane < ATT_HEAD, ks * (1.0 / blk), 0.0)

    lane = lax.broadcasted_iota(I32, (blk, LANES), 1)
    lane_f = lane.astype(F32)
    n_idx = lane - ATT_HEAD
    scale = ATT_HEAD ** -0.5
    n_far = jnp.maximum(i - 1, 0)
    i_prev = jnp.maximum(i - 1, 0)
    pen_prev = jnp.where(i >= 1, 0.0, NEG)

    def head(h, carry):
        q = q_ref[0, h]
        bs = _dot_nt(q, km_ref[h], precision=lax.Precision.HIGHEST)
        work = jnp.where((n_idx >= 0) & (n_idx < i), bs, -jnp.inf)
        sel = n_idx == i
        for _ in range(MOBA_TOPK):
            m = jnp.max(work, axis=1, keepdims=True)
            is_m = (work == m) & (work > -jnp.inf)
            idx = jnp.min(jnp.where(is_m, lane_f, 4.0 * LANES), axis=1, keepdims=True)
            pick = lane_f == idx
            sel = sel | pick
            work = jnp.where(pick, -jnp.inf, work)
        selb = jnp.where((n_idx >= 0) & (n_idx < nblk) & jnp.logical_not(sel), NEG, 0.0)
        qa = (q * scale + selb).astype(BF16)

        def logits(n):
            kb = k_ref[0, h, pl.ds(pl.multiple_of(n * blk, blk), blk), :]
            return _dot_nt(qa, kb)

        def pv(p, n):
            vb = v_ref[0, h, pl.ds(pl.multiple_of(n * blk, blk), blk), :]
            return _dot(p.astype(BF16), vb)

        s = logits(i) + bias_ref[h, 0]
        m = jnp.max(s, axis=1, keepdims=True)
        p = jnp.exp(s - m)
        l = jnp.sum(p, axis=1, keepdims=True)
        acc = pv(p, i)

        def update(s, n, m, l, acc):
            m_new = jnp.maximum(m, jnp.max(s, axis=1, keepdims=True))
            alpha = jnp.exp(m - m_new)
            p = jnp.exp(s - m_new)
            return m_new, alpha * l + jnp.sum(p, axis=1, keepdims=True), alpha * acc + pv(p, n)

        m, l, acc = update(logits(i_prev) + bias_ref[h, 1] + pen_prev, i_prev, m, l, acc)

        def far(n, c):
            return update(logits(n), n, *c)

        m, l, acc = lax.fori_loop(0, n_far, far, (m, l, acc))
        o_ref[0, h] = acc / l
        return carry

    lax.fori_loop(0, ATT_HEADS, head, 0)


def _moba_seq(q_pad, k_aug, v_h, rel_bias):
    b, nh, t, _ = q_pad.shape
    nblk = t // MOBA_BLOCK
    d = np.arange(MOBA_BLOCK)[:, None] - np.arange(MOBA_BLOCK)[None, :]
    bko = jnp.asarray(np.where(d >= 0, _t5_bucket_np(d), -1).astype(np.int32))
    bkp = jnp.asarray(_t5_bucket_np(d + MOBA_BLOCK))
    return pl.pallas_call(
        functools.partial(_moba_seq_kernel, nblk=nblk),
        grid=(b, nblk),
        in_specs=[pl.BlockSpec(memory_space=pltpu.SMEM),
                  _const((MOBA_BLOCK, MOBA_BLOCK)), _const((MOBA_BLOCK, MOBA_BLOCK)),
                  pl.BlockSpec((1, nh, MOBA_BLOCK, LANES), lambda i, j: (i, 0, j, 0)),
                  pl.BlockSpec((1, nh, t, LANES), lambda i, j: (i, 0, 0, 0), pipeline_mode=pl.Buffered(1)),
                  pl.BlockSpec((1, nh, t, ATT_HEAD), lambda i, j: (i, 0, 0, 0), pipeline_mode=pl.Buffered(1))],
        out_specs=pl.BlockSpec((1, nh, MOBA_BLOCK, ATT_HEAD), lambda i, j: (i, 0, j, 0)),
        out_shape=jax.ShapeDtypeStruct((b, nh, t, ATT_HEAD), F32),
        scratch_shapes=[pltpu.VMEM((nh, LANES, LANES), F32), pltpu.VMEM((nh, 2, MOBA_BLOCK, MOBA_BLOCK), F32)],
        compiler_params=_params(2),
        name="moba_seq",
    )(rel_bias, bko, bkp, q_pad, k_aug, v_h)


def _moba_step_kernel(pt_ref, q_ref, kn_ref, vn_ref, relt_ref, bkt_ref, hs_ref, *refs, n_pages):
    kp = refs[:n_pages]
    vp = refs[n_pages:2 * n_pages]
    o_ref = refs[2 * n_pages]
    pages_per_block = MOBA_BLOCK // PAGE_SIZE
    n_blocks = n_pages // pages_per_block
    scale = ATT_HEAD ** -0.5
    q = q_ref[0]
    hs = hs_ref[...]
    hs_b = hs.astype(BF16)
    relt = relt_ref[...]
    c_far = relt[:, N_BUCKETS - 1:N_BUCKETS]
    bkt = bkt_ref[...]
    bias_last = jnp.zeros((ATT_HEADS, PAGE_SIZE), F32)
    for j in range(N_BUCKETS):
        bias_last = jnp.where(bkt == j, relt[:, j:j + 1] - c_far, bias_last)

    ms, ls, accs, ksums = [], [], [], []
    for p in range(n_pages):
        kpg = kp[p][0]
        s = _dot_nt(hs_b, (kpg * q).astype(BF16)) * scale
        if p == n_pages - 1:
            s = s + bias_last
        m = jnp.max(s, axis=1, keepdims=True)
        e = jnp.exp(s - m)
        ms.append(m)
        ls.append(jnp.sum(e, axis=1, keepdims=True))
        accs.append(_dot(e.astype(BF16), vp[p][0].astype(BF16)))
        ksums.append(jnp.sum(kpg, axis=0, keepdims=True))

    sc = []
    for n in range(n_blocks):
        ks = ksums[n * pages_per_block]
        for j in range(1, pages_per_block):
            ks = ks + ksums[n * pages_per_block + j]
        sc.append(jnp.sum(hs * (ks * (1.0 / MOBA_BLOCK) * q), axis=1, keepdims=True))
    sel = []
    for n in range(n_blocks):
        rank = jnp.zeros((ATT_HEADS, 1), I32)
        for j in range(n_blocks):
            if j != n:
                ahead = (sc[j] > sc[n]) | ((sc[j] == sc[n]) & (j < n))
                rank = rank + ahead.astype(I32)
        sel.append(rank < MOBA_TOPK)

    s_self = jnp.sum(hs * (kn_ref[0] * q), axis=1, keepdims=True) * scale + (relt[:, 0:1] - c_far)
    m_all = s_self
    for p in range(n_pages):
        m_all = jnp.maximum(m_all, jnp.where(sel[p // pages_per_block], ms[p], -jnp.inf))
    w_self = jnp.exp(s_self - m_all)
    l_all = w_self
    out = w_self * vn_ref[0]
    for p in range(n_pages):
        wgt = jnp.where(sel[p // pages_per_block], jnp.exp(ms[p] - m_all), 0.0)
        l_all = l_all + wgt * ls[p]
        out = out + wgt * accs[p]
    o_ref[0] = jnp.sum((out / l_all) * hs, axis=0, keepdims=True)


def _moba_step(q, k_new, v_new, cache_k, cache_v, page_table, rel_bias):
    n, n_pages = page_table.shape
    row = pl.BlockSpec((1, 1, BRANCH_W), lambda i, pt: (i, 0, 0))
    relt = jnp.pad(rel_bias.T, ((0, 0), (0, LANES - N_BUCKETS)))
    bkt = jnp.asarray(_t5_bucket_np(PAGE_SIZE - np.arange(PAGE_SIZE))[None, :])
    hs = jnp.asarray((np.arange(BRANCH_W)[None, :] // ATT_HEAD == np.arange(ATT_HEADS)[:, None]).astype(np.float32))

    def page_spec(p):
        return pl.BlockSpec((1, PAGE_SIZE, BRANCH_W), lambda i, pt: (pt[i * n_pages + p], 0, 0))

    grid_spec = pltpu.PrefetchScalarGridSpec(
        num_scalar_prefetch=1,
        grid=(n,),
        in_specs=[row, row, row,
                  pl.BlockSpec((ATT_HEADS, LANES), lambda i, pt: (0, 0)),
                  pl.BlockSpec((1, PAGE_SIZE), lambda i, pt: (0, 0)),
                  pl.BlockSpec((ATT_HEADS, BRANCH_W), lambda i, pt: (0, 0))]
                 + [page_spec(p) for p in range(n_pages)] * 2,
        out_specs=row,
    )
    ck = cache_k.reshape(cache_k.shape[0], PAGE_SIZE, BRANCH_W)
    cv = cache_v.reshape(cache_v.shape[0], PAGE_SIZE, BRANCH_W)
    return pl.pallas_call(
        functools.partial(_moba_step_kernel, n_pages=n_pages),
        grid_spec=grid_spec,
        out_shape=jax.ShapeDtypeStruct((n, 1, BRANCH_W), F32),
        compiler_params=_params(1),
        name="moba_step",
    )(page_table.reshape(-1), q[:, None, :], k_new[:, None, :], v_new[:, None, :], relt, bkt, hs,
      *([ck] * n_pages), *([cv] * n_pages))[:, 0, :]


def _merge_kernel(x_ref, ya_ref, yr_ref, bon_ref, g_ref, yc_ref, gate_ref,
                  lnw_ref, lnb_ref, ones_ref, wb_ref, wo_ref, o_ref):
    ones_bd = ones_ref[...]
    y = yr_ref[...]
    mu = _head_sum(y, ones_bd) * (1.0 / RWKV_HEAD)
    d = y - mu
    var = _head_sum(d * d, ones_bd) * (1.0 / RWKV_HEAD)
    yb = (d * lax.rsqrt(var + RWKV_GN_EPS) * lnw_ref[...] + lnb_ref[...] + bon_ref[...]) * g_ref[...]
    merged = None
    for j, yj in enumerate((ya_ref[...], yb, yc_ref[...])):
        gate = jax.nn.sigmoid(gate_ref[:, j * D_MODEL:(j + 1) * D_MODEL])
        term = _dot(yj.astype(BF16), wb_ref[j]) * gate
        merged = term if merged is None else merged + term
    o_ref[...] = x_ref[...] + _dot(merged.astype(BF16), wo_ref[...])


def _merge(x, ya, yr, bonus, g, yc, gates, mw, tm):
    m = x.shape[0]
    tok = lambda w: pl.BlockSpec((tm, w), lambda i: (i, 0))
    vec = _const((1, BRANCH_W))
    return pl.pallas_call(
        _merge_kernel,
        grid=(m // tm,),
        in_specs=[tok(D_MODEL)] + [tok(BRANCH_W)] * 5 + [tok(N_BRANCH * D_MODEL), vec, vec,
                  _const((BRANCH_W, BRANCH_W)), _const((N_BRANCH, BRANCH_W, D_MODEL)), _const((D_MODEL, D_MODEL))],
        out_specs=tok(D_MODEL),
        out_shape=jax.ShapeDtypeStruct((m, D_MODEL), F32),
        compiler_params=_params(1),
        name="merge",
    )(x, ya, yr, bonus, g, yc, gates, mw["ln_w"], mw["ln_b"], mw["ones_bd"], mw["w_branch"], mw["w_out"])


FFN_CHUNK = 512


def _ffn_kernel(*refs, seq_mode, final_norm, tm, tiles_per_seq):
    (x_ref, p_ref, ln2_ref, wup_ref, fcw_ref, fcb_ref, wdn_ref, ln3_ref, pg_ref, pp_ref, lnf_ref) = refs[:11]
    if seq_mode:
        o_ref, fc_ref, ext_ref = refs[11:14]
        i = pl.program_id(0)

        @pl.when(i % tiles_per_seq == 0)
        def _():
            ext_ref[0:SUBLANES, :] = jnp.zeros((SUBLANES, 2 * D_FF), F32)

        @pl.when(i % tiles_per_seq != 0)
        def _():
            ext_ref[0:SUBLANES, :] = ext_ref[tm:tm + SUBLANES, :]
    else:
        prev_ref, o_ref, u_ref = refs[11:14]

    x = x_ref[...]
    hb = _rms(x, ln2_ref[...]).astype(BF16)

    def conv_cols(lo, hi):
        u = _dot(hb, wup_ref[:, lo:hi])
        if seq_mode:
            ext_ref[SUBLANES:, lo:hi] = u
            u2 = ext_ref[pl.ds(SUBLANES - 2, tm), lo:hi]
            u1 = ext_ref[pl.ds(SUBLANES - 1, tm), lo:hi]
        else:
            u_ref[:, lo:hi] = u
            u2 = prev_ref[:, lo:hi]
            u1 = prev_ref[:, 2 * D_FF + lo:2 * D_FF + hi]
        return u2 * fcw_ref[0:1, lo:hi] + u1 * fcw_ref[1:2, lo:hi] + u * fcw_ref[2:3, lo:hi] + fcb_ref[:, lo:hi]

    acc = None
    for c in range(D_FF // FFN_CHUNK):
        lo, hi = c * FFN_CHUNK, (c + 1) * FFN_CHUNK
        act = _gelu_tanh(conv_cols(lo, hi)) * conv_cols(D_FF + lo, D_FF + hi)
        term = _dot(act.astype(BF16), wdn_ref[lo:hi, :])
        acc = term if acc is None else acc + term
    x = x + acc
    x = x + jax.nn.sigmoid(_dot(_rms(x, ln3_ref[...]).astype(BF16), pg_ref[...])) * _dot(p_ref[...].astype(BF16), pp_ref[...])
    o_ref[...] = _rms(x, lnf_ref[...]) if final_norm else x
    if seq_mode:
        fc_ref[0] = ext_ref[tm + SUBLANES - 2:tm + SUBLANES, :]


def _ffn(x, p, fw, tm, seq_len, prev=None, final_norm=False):
    m = x.shape[0]
    seq_mode = seq_len is not None
    tok = lambda w: pl.BlockSpec((tm, w), lambda i: (i, 0))
    vecd = _const((1, D_MODEL))
    in_specs = [tok(D_MODEL), tok(PLE_DIM), vecd, _const((D_MODEL, 2 * D_FF)), _const((FFN_CONV, 2 * D_FF)),
                _const((1, 2 * D_FF)), _const((D_FF, D_MODEL)), vecd, _const((D_MODEL, D_MODEL)),
                _const((PLE_DIM, D_MODEL)), vecd]
    args = [x, p, fw["ln2"], fw["ffn_up"], fw["conv_w"], fw["conv_b"], fw["ffn_down"], fw["ln3"],
            fw["ple_gate"], fw["ple_proj"], fw["ln_f"]]
    if seq_mode:
        tiles_per_seq = seq_len // tm
        out_specs = [tok(D_MODEL), pl.BlockSpec((1, FFN_CONV - 1, 2 * D_FF), lambda i: (i // tiles_per_seq, 0, 0))]
        out_shape = [jax.ShapeDtypeStruct((m, D_MODEL), F32),
                     jax.ShapeDtypeStruct((m // seq_len, FFN_CONV - 1, 2 * D_FF), F32)]
        scratch = [pltpu.VMEM((tm + SUBLANES, 2 * D_FF), F32)]
    else:
        tiles_per_seq = 1
        in_specs.append(tok((FFN_CONV - 1) * 2 * D_FF))
        args.append(prev)
        out_specs = [tok(D_MODEL), tok(2 * D_FF)]
        out_shape = [jax.ShapeDtypeStruct((m, D_MODEL), F32), jax.ShapeDtypeStruct((m, 2 * D_FF), F32)]
        scratch = []
    return pl.pallas_call(
        functools.partial(_ffn_kernel, seq_mode=seq_mode, final_norm=final_norm, tm=tm, tiles_per_seq=tiles_per_seq),
        grid=(m // tm,),
        in_specs=in_specs,
        out_specs=out_specs,
        out_shape=out_shape,
        scratch_shapes=scratch,
        compiler_params=_params(1),
        name="ffn",
    )(*args)


def _block_diag(w):
    g, n, _ = w.shape
    eye = jnp.eye(g, dtype=w.dtype)
    return (eye[:, None, :, None] * w[:, :, None, :]).reshape(g * n, g * n)


def _layer_weights(i, wt):
    row = lambda v: v.reshape(1, -1)
    ones_bd = _block_diag(jnp.ones((RWKV_HEADS, RWKV_HEAD, RWKV_HEAD), BF16))
    zeros_lora = jnp.zeros((DECAY_LORA, BRANCH_W), BF16)
    lw = {"conv_w": wt["lru_conv_w"][i], "conv_b": row(wt["lru_conv_b"][i]),
          "w_r": _block_diag(wt["lru_w_r"][i]).astype(BF16), "b_r": row(wt["lru_b_r"][i]),
          "w_i": _block_diag(wt["lru_w_i"][i]).astype(BF16), "b_i": row(wt["lru_b_i"][i]),
          "lam": row(wt["lru_lambda"][i])}
    rw = {"mu": row(wt["rwkv_mu"][i]), "w0": row(wt["rwkv_w0"][i]),
          "w2p": jnp.concatenate([wt["rwkv_w2"][i].astype(BF16), zeros_lora], axis=0),
          "a0": row(wt["rwkv_a0"][i]),
          "a2p": jnp.concatenate([zeros_lora, wt["rwkv_a2"][i].astype(BF16)], axis=0),
          "g2": wt["rwkv_g2"][i].astype(BF16), "k_k": row(wt["rwkv_k_k"][i]), "k_a": row(wt["rwkv_k_a"][i]),
          "r_k": row(wt["rwkv_r_k"][i]), "ones_bd": ones_bd}
    mw = {"ln_w": row(wt["rwkv_ln_w"][i]), "ln_b": row(wt["rwkv_ln_b"][i]), "ones_bd": ones_bd,
          "w_branch": wt["w_branch"][i].astype(BF16), "w_out": wt["w_out"][i].astype(BF16)}
    fw = {"ln2": row(wt["ln2"][i]), "ffn_up": wt["ffn_up"][i].astype(BF16), "conv_w": wt["ffn_conv_w"][i],
          "conv_b": row(wt["ffn_conv_b"][i]), "ffn_down": wt["ffn_down"][i].astype(BF16),
          "ln3": row(wt["ln3"][i]), "ple_gate": wt["ple_gate"][i].astype(BF16),
          "ple_proj": wt["ple_proj"][i].astype(BF16), "ln_f": row(wt["ln_f"])}
    return {"ln1": row(wt["ln1"][i]), "w_in": wt["w_in"][i].astype(BF16), "lru": lw, "rwkv": rw, "merge": mw, "ffn": fw}


SCAN_ROW_SPLIT = 4


def _prompt_layer(x, p, lp, rel_bias, final_norm):
    b, t, _ = x.shape
    m = b * t
    nh, hd = RWKV_HEADS, RWKV_HEAD
    lru_c, rw_c, q, k, v, gates = _in_proj(x.reshape(m, D_MODEL), lp["ln1"], lp["w_in"], tm=256)

    ya, conv_new, h_new = _lru_seq(lru_c.reshape(b, t, 2 * BRANCH_W), lp["lru"], tc=256)

    pre = _rwkv_pre_seq(rw_c.reshape(b, t, RWKV_COLS), lp["rwkv"], tc=256)
    r_, w_, k_, v_, a_, b_, g_, bonus = pre
    split = SCAN_ROW_SPLIT
    assert b * nh * split == LANES

    def key_major(z):
        z = z.reshape(b, t, nh, hd).transpose(1, 3, 0, 2).reshape(t, hd, b * nh)
        return jnp.tile(z, (1, 1, split))[None]

    def row_major(z):
        z = z.reshape(b, t, nh, hd // split, split).transpose(1, 3, 4, 0, 2)
        return z.reshape(1, t, hd // split, LANES)

    s0 = jnp.zeros((1, hd // split, hd, LANES), F32)
    y_l, s_l = _rwkv_scan(key_major(r_), key_major(w_), key_major(k_), key_major(a_), key_major(b_),
                          row_major(v_), s0, tc=32)
    yr = y_l.reshape(t, hd // split, split, b, nh).transpose(3, 0, 4, 1, 2).reshape(m, BRANCH_W)
    s_new = s_l.reshape(hd // split, hd, split, b, nh).transpose(3, 4, 0, 2, 1).reshape(b, nh, hd, hd)

    nblk = t // MOBA_BLOCK
    heads = lambda z: z.reshape(b, t, ATT_HEADS, ATT_HEAD).transpose(0, 2, 1, 3)
    q_pad = jnp.pad(heads(q), ((0, 0), (0, 0), (0, 0), (0, LANES - ATT_HEAD)))
    onehot = (jnp.arange(t)[:, None] // MOBA_BLOCK == jnp.arange(LANES - ATT_HEAD)[None, :]).astype(BF16)
    k_aug = jnp.concatenate([heads(k).astype(BF16), jnp.broadcast_to(onehot, (b, ATT_HEADS, t, LANES - ATT_HEAD))], axis=-1)
    assert nblk <= LANES - ATT_HEAD
    yc = _moba_seq(q_pad, k_aug, heads(v).astype(BF16), rel_bias)
    yc = yc.transpose(0, 2, 1, 3).reshape(m, BRANCH_W)

    flat = lambda z: z.reshape(m, BRANCH_W)
    x1 = _merge(x.reshape(m, D_MODEL), flat(ya), yr, flat(bonus), flat(g_), yc, gates, lp["merge"], tm=256)
    x2, fc_new = _ffn(x1, p.reshape(m, PLE_DIM), lp["ffn"], tm=256, seq_len=t, final_norm=final_norm)
    new = {"k": k.reshape(b, t, ATT_HEADS, ATT_HEAD), "v": v.reshape(b, t, ATT_HEADS, ATT_HEAD),
           "lru_h": h_new[:, 0, :], "lru_conv": conv_new, "rwkv": s_new,
           "rwkv_shift": rw_c.reshape(b, t, RWKV_COLS)[:, -1, :], "ffn_conv": fc_new}
    return x2.reshape(b, t, D_MODEL), new


def _sample_layer(x, p, st, lp, rel_bias, final_norm):
    n = x.shape[0]
    nh, hd = RWKV_HEADS, RWKV_HEAD
    lru_c, rw_c, q, k, v, gates = _in_proj(x.reshape(n, D_MODEL), lp["ln1"], lp["w_in"], tm=n)

    ya, conv_new, h_new = _lru_step(lru_c, st["lru_conv"].reshape(n, (LRU_CONV - 1) * BRANCH_W), st["lru_h"], lp["lru"])

    r_, w_, k_, v_, a_, b_, g_, bonus = _rwkv_pre_step(rw_c, st["rwkv_shift"], lp["rwkv"])
    assert n == LANES
    lanes_last = lambda z: z.reshape(n, nh, hd).transpose(1, 2, 0)[:, None]
    s0 = st["rwkv"].transpose(1, 2, 3, 0)
    y_l, s_l = _rwkv_scan(lanes_last(r_), lanes_last(w_), lanes_last(k_), lanes_last(a_), lanes_last(b_),
                          lanes_last(v_), s0, tc=1)
    yr = y_l[:, 0].transpose(2, 0, 1).reshape(n, BRANCH_W)
    s_new = s_l.transpose(3, 0, 1, 2)

    yc = _moba_step(q, k, v, st["cache_k"], st["cache_v"], st["page_table"], rel_bias)

    x1 = _merge(x.reshape(n, D_MODEL), ya, yr, bonus, g_, yc, gates, lp["merge"], tm=n)
    fc_prev = st["ffn_conv"].reshape(n, (FFN_CONV - 1) * 2 * D_FF)
    x2, u = _ffn(x1, p.reshape(n, PLE_DIM), lp["ffn"], tm=n, seq_len=None, prev=fc_prev, final_norm=final_norm)
    new = {"k": k.reshape(n, 1, ATT_HEADS, ATT_HEAD), "v": v.reshape(n, 1, ATT_HEADS, ATT_HEAD),
           "lru_h": h_new, "lru_conv": conv_new.reshape(n, LRU_CONV - 1, BRANCH_W), "rwkv": s_new,
           "rwkv_shift": rw_c, "ffn_conv": jnp.stack([st["ffn_conv"][:, 1, :], u], axis=1)}
    return x2.reshape(n, 1, D_MODEL), new


def kernel(x_prompt, x_sample, cache_k, cache_v, state_lru_h, state_lru_conv, state_rwkv, state_rwkv_shift, state_ffn_conv, page_table, p_prompt, p_sample, ln1, w_in, lru_conv_w, lru_conv_b, lru_w_r, lru_b_r, lru_w_i, lru_b_i, lru_lambda, rwkv_mu, rwkv_w0, rwkv_w2, rwkv_a0, rwkv_a2, rwkv_g2, rwkv_k_k, rwkv_k_a, rwkv_r_k, rwkv_ln_w, rwkv_ln_b, rel_bias, w_branch, w_out, ln2, ffn_up, ffn_conv_w, ffn_conv_b, ffn_down, ln3, ple_gate, ple_proj, ln_f):
    wt = dict(ln1=ln1, w_in=w_in, lru_conv_w=lru_conv_w, lru_conv_b=lru_conv_b, lru_w_r=lru_w_r, lru_b_r=lru_b_r,
              lru_w_i=lru_w_i, lru_b_i=lru_b_i, lru_lambda=lru_lambda, rwkv_mu=rwkv_mu, rwkv_w0=rwkv_w0,
              rwkv_w2=rwkv_w2, rwkv_a0=rwkv_a0, rwkv_a2=rwkv_a2, rwkv_g2=rwkv_g2, rwkv_k_k=rwkv_k_k,
              rwkv_k_a=rwkv_k_a, rwkv_r_k=rwkv_r_k, rwkv_ln_w=rwkv_ln_w, rwkv_ln_b=rwkv_ln_b, w_branch=w_branch,
              w_out=w_out, ln2=ln2, ffn_up=ffn_up, ffn_conv_w=ffn_conv_w, ffn_conv_b=ffn_conv_b, ffn_down=ffn_down,
              ln3=ln3, ple_gate=ple_gate, ple_proj=ple_proj, ln_f=ln_f)
    depth = w_in.shape[0]
    xp, xs = x_prompt, x_sample[:, 0, :]
    outs_p, outs_s = [], []
    for i in range(depth):
        lp = _layer_weights(i, wt)
        last = i == depth - 1
        st = {"cache_k": cache_k[i], "cache_v": cache_v[i], "page_table": page_table,
              "lru_h": state_lru_h[i], "lru_conv": state_lru_conv[i], "rwkv": state_rwkv[i],
              "rwkv_shift": state_rwkv_shift[i], "ffn_conv": state_ffn_conv[i]}
        xp, new_p = _prompt_layer(xp, p_prompt[i], lp, rel_bias, last)
        xs3, new_s = _sample_layer(xs, p_sample[i], st, lp, rel_bias, last)
        xs = xs3[:, 0, :]
        outs_p.append(new_p)
        outs_s.append(new_s)
    stack = lambda outs, name: jnp.stack([o[name] for o in outs])
    res = [xp, xs[:, None, :]]
    for name in ("k", "v", "lru_h", "lru_conv", "rwkv", "rwkv_shift", "ffn_conv"):
        res.append(stack(outs_p, name))
        res.append(stack(outs_s, name))
    return tuple(res)
```

```python
import functools
import math

import numpy as np
import jax
import jax.numpy as jnp
from jax import lax
from jax.experimental import pallas as pl
from jax.experimental.pallas import tpu as pltpu

F32 = jnp.float32
BF16 = jnp.bfloat16
I32 = jnp.int32

D_MODEL = 1024
DEPTH = 2
PAGE_SIZE = 128
N_BRANCH = 3
BRANCH_W = D_MODEL // 2
LRU_BLOCKS = 8
LRU_CONV = 4
LRU_C = 8.0
RWKV_HEAD = 64
RWKV_HEADS = BRANCH_W // RWKV_HEAD
DECAY_LORA = 64
AAA_LORA = 64
GATE_LORA = 128
RWKV_COLS = 3 * BRANCH_W + DECAY_LORA + AAA_LORA + GATE_LORA
RWKV_GN_EPS = 64e-5
ATT_HEAD = 64
ATT_HEADS = BRANCH_W // ATT_HEAD
MOBA_BLOCK = 256
MOBA_TOPK = 3
N_BUCKETS = 32
MAX_DISTANCE = 128
D_FF = 3 * D_MODEL
FFN_CONV = 3
PLE_DIM = 256
RMS_EPS = 1e-6
N_IN = 2 * BRANCH_W + RWKV_COLS + 3 * BRANCH_W + N_BRANCH * D_MODEL

_SEG_EDGES = (0, 2 * BRANCH_W, 2 * BRANCH_W + RWKV_COLS, 2 * BRANCH_W + RWKV_COLS + BRANCH_W,
              2 * BRANCH_W + RWKV_COLS + 2 * BRANCH_W, 2 * BRANCH_W + RWKV_COLS + 3 * BRANCH_W, N_IN)
_IN_SEGS = tuple(zip(_SEG_EDGES[:-1], _SEG_EDGES[1:]))

LANES = 128
SUBLANES = 8
VMEM_LIMIT = 56 * 1024 * 1024
NEG = -1e30
LORA_W = DECAY_LORA + AAA_LORA


def _params(n_axes):
    return pltpu.CompilerParams(dimension_semantics=("arbitrary",) * n_axes, vmem_limit_bytes=VMEM_LIMIT)


def _const(shape):
    return pl.BlockSpec(shape, lambda *_: (0,) * len(shape), pipeline_mode=pl.Buffered(1))


def _softplus(x):
    return jnp.maximum(x, 0.0) + jnp.log1p(jnp.exp(-jnp.abs(x)))


def _gelu_tanh(x):
    return 0.5 * x * (1.0 + jnp.tanh(math.sqrt(2.0 / math.pi) * (x + 0.044715 * (x * x * x))))


def _rms(x, g):
    return x * lax.rsqrt(jnp.mean(x * x, axis=-1, keepdims=True) + RMS_EPS) * g


def _dot(a, b):
    return jnp.dot(a, b, preferred_element_type=F32)


def _dot_nt(a, b, precision=None):
    return lax.dot_general(a, b, (((1,), (1,)), ((), ())), precision=precision, preferred_element_type=F32)


def _head_sum(x, ones_bd):
    hi = x.astype(BF16)
    lo = (x - hi.astype(F32)).astype(BF16)
    return _dot(hi, ones_bd) + _dot(lo, ones_bd)


def _in_proj_kernel(x_ref, g_ref, w_ref, *out_refs):
    xn = _rms(x_ref[...], g_ref[...]).astype(BF16)
    for ref, (lo, hi) in zip(out_refs, _IN_SEGS):
        ref[...] = _dot(xn, w_ref[:, lo:hi])


def _in_proj(x, g, w_bf16, tm):
    m = x.shape[0]
    widths = [hi - lo for lo, hi in _IN_SEGS]
    return pl.pallas_call(
        _in_proj_kernel,
        grid=(m // tm,),
        in_specs=[pl.BlockSpec((tm, D_MODEL), lambda i: (i, 0)), _const((1, D_MODEL)), _const((D_MODEL, N_IN))],
        out_specs=[pl.BlockSpec((tm, w), lambda i: (i, 0)) for w in widths],
        out_shape=[jax.ShapeDtypeStruct((m, w), F32) for w in widths],
        compiler_params=_params(1),
        name="in_proj",
    )(x, g, w_bf16)


_SEQ_SEGS = (0, 1, 3, 4, 5)


def _in_proj_seq_kernel(x_ref, g_ref, w_ref, wqt_ref, wvt_ref, wka_ref, *out_refs, nblk):
    xn = _rms(x_ref[...], g_ref[...]).astype(BF16)
    for ref, s in zip(out_refs[:5], _SEQ_SEGS):
        lo, hi = _IN_SEGS[s]
        ref[...] = _dot(xn, w_ref[:, lo:hi])
    qt_ref, vt_ref, ka_ref = out_refs[5:8]
    qt_ref[0, 0] = _dot_nt(wqt_ref[...], xn)
    vt_ref[0, 0] = _dot_nt(wvt_ref[...], xn).astype(BF16)
    n = pl.program_id(0) % nblk
    onehot = (lax.broadcasted_iota(I32, (1, LANES), 1) == ATT_HEAD + n).astype(F32)
    for h in range(ATT_HEADS):
        ka_ref[0, h] = (_dot(xn, wka_ref[h]) + onehot).astype(BF16)


def _in_proj_seq(x, g, pw, b, t):
    m = x.shape[0]
    tm = MOBA_BLOCK
    nblk = t // tm
    widths = [_IN_SEGS[s][1] - _IN_SEGS[s][0] for s in _SEQ_SEGS]
    blk_t = pl.BlockSpec((1, 1, BRANCH_W, tm), lambda i: (i // nblk, i % nblk, 0, 0))
    return pl.pallas_call(
        functools.partial(_in_proj_seq_kernel, nblk=nblk),
        grid=(m // tm,),
        in_specs=[pl.BlockSpec((tm, D_MODEL), lambda i: (i, 0)), _const((1, D_MODEL)), _const((D_MODEL, N_IN)),
                  _const((BRANCH_W, D_MODEL)), _const((BRANCH_W, D_MODEL)), _const((ATT_HEADS, D_MODEL, LANES))],
        out_specs=[pl.BlockSpec((tm, w), lambda i: (i, 0)) for w in widths]
                  + [blk_t, blk_t, pl.BlockSpec((1, ATT_HEADS, tm, LANES), lambda i: (i // nblk, 0, i % nblk, 0))],
        out_shape=[jax.ShapeDtypeStruct((m, w), F32) for w in widths]
                  + [jax.ShapeDtypeStruct((b, nblk, BRANCH_W, tm), F32),
                     jax.ShapeDtypeStruct((b, nblk, BRANCH_W, tm), BF16),
                     jax.ShapeDtypeStruct((b, ATT_HEADS, t, LANES), BF16)],
        compiler_params=_params(1),
        name="in_proj_seq",
    )(x, g, pw["w_in"], pw["wq_t"], pw["wv_t"], pw["wk_aug"])


def _lru_gates(xc, ga_unused, wr, br, wi, bi, lam):
    xcb = xc.astype(BF16)
    r = jax.nn.sigmoid(_dot(xcb, wr) + br)
    ig = jax.nn.sigmoid(_dot(xcb, wi) + bi)
    log_a = (-LRU_C * r) * _softplus(-lam)
    a = jnp.exp(log_a)
    bx = jnp.sqrt(1.0 - jnp.exp(2.0 * log_a)) * (ig * xc)
    return a, bx


def _lru_seq_kernel(cols_ref, cw_ref, cb_ref, wr_ref, br_ref, wi_ref, bi_ref, lam_ref,
                    y_ref, conv_ref, h_ref, ext_ref, hc_ref, *, tc):
    t = pl.program_id(1)

    @pl.when(t == 0)
    def _():
        ext_ref[0:SUBLANES, :] = jnp.zeros((SUBLANES, BRANCH_W), F32)
        hc_ref[...] = jnp.zeros_like(hc_ref)

    @pl.when(t > 0)
    def _():
        ext_ref[0:SUBLANES, :] = ext_ref[tc:tc + SUBLANES, :]

    xa = cols_ref[0, :, 0:BRANCH_W]
    ga = cols_ref[0, :, BRANCH_W:]
    ext_ref[SUBLANES:, :] = xa
    xc = ext_ref[pl.ds(SUBLANES - 3, tc), :] * cw_ref[0:1, :]
    xc = xc + ext_ref[pl.ds(SUBLANES - 2, tc), :] * cw_ref[1:2, :]
    xc = xc + ext_ref[pl.ds(SUBLANES - 1, tc), :] * cw_ref[2:3, :]
    xc = xc + xa * cw_ref[3:4, :] + cb_ref[...]
    a, bx = _lru_gates(xc, ga, wr_ref[...], br_ref[...], wi_ref[...], bi_ref[...], lam_ref[...])
    row = lax.broadcasted_iota(I32, (tc, BRANCH_W), 0)
    d = 1
    while d < tc:
        keep = row >= d
        a_s = jnp.where(keep, pltpu.roll(a, d, 0), 1.0)
        b_s = jnp.where(keep, pltpu.roll(bx, d, 0), 0.0)
        bx = a * b_s + bx
        a = a * a_s
        d *= 2
    h = a * hc_ref[...] + bx
    hc_ref[...] = h[tc - 1:tc, :]
    y_ref[0] = h * _gelu_tanh(ga)
    conv_ref[0] = ext_ref[tc + SUBLANES - 3:tc + SUBLANES, :]
    h_ref[0] = h[tc - 1:tc, :]


def _lru_seq(cols, lw, tc):
    b, t, _ = cols.shape
    vec = _const((1, BRANCH_W))
    mat = _const((BRANCH_W, BRANCH_W))
    return pl.pallas_call(
        functools.partial(_lru_seq_kernel, tc=tc),
        grid=(b, t // tc),
        in_specs=[pl.BlockSpec((1, tc, 2 * BRANCH_W), lambda i, j: (i, j, 0)),
                  _const((LRU_CONV, BRANCH_W)), vec, mat, vec, mat, vec, vec],
        out_specs=[pl.BlockSpec((1, tc, BRANCH_W), lambda i, j: (i, j, 0)),
                   pl.BlockSpec((1, LRU_CONV - 1, BRANCH_W), lambda i, j: (i, 0, 0)),
                   pl.BlockSpec((1, 1, BRANCH_W), lambda i, j: (i, 0, 0))],
        out_shape=[jax.ShapeDtypeStruct((b, t, BRANCH_W), F32),
                   jax.ShapeDtypeStruct((b, LRU_CONV - 1, BRANCH_W), F32),
                   jax.ShapeDtypeStruct((b, 1, BRANCH_W), F32)],
        scratch_shapes=[pltpu.VMEM((tc + SUBLANES, BRANCH_W), F32), pltpu.VMEM((1, BRANCH_W), F32)],
        compiler_params=_params(2),
        name="lru_seq",
    )(cols, lw["conv_w"], lw["conv_b"], lw["w_r"], lw["b_r"], lw["w_i"], lw["b_i"], lw["lam"])


def _lru_step_kernel(cols_ref, buf_ref, h0_ref, cw_ref, cb_ref, wr_ref, br_ref, wi_ref, bi_ref, lam_ref,
                     y_ref, conv_ref, h_ref):
    xa = cols_ref[:, 0:BRANCH_W]
    ga = cols_ref[:, BRANCH_W:]
    xc = buf_ref[:, 0:BRANCH_W] * cw_ref[0:1, :]
    xc = xc + buf_ref[:, BRANCH_W:2 * BRANCH_W] * cw_ref[1:2, :]
    xc = xc + buf_ref[:, 2 * BRANCH_W:] * cw_ref[2:3, :]
    xc = xc + xa * cw_ref[3:4, :] + cb_ref[...]
    a, bx = _lru_gates(xc, ga, wr_ref[...], br_ref[...], wi_ref[...], bi_ref[...], lam_ref[...])
    h = a * h0_ref[...] + bx
    y_ref[...] = h * _gelu_tanh(ga)
    conv_ref[:, 0:2 * BRANCH_W] = buf_ref[:, BRANCH_W:]
    conv_ref[:, 2 * BRANCH_W:] = xa
    h_ref[...] = h


def _lru_step(cols, buf, h0, lw):
    n = cols.shape[0]
    full = lambda w: pl.BlockSpec((n, w), lambda i: (0, 0))
    vec = _const((1, BRANCH_W))
    mat = _const((BRANCH_W, BRANCH_W))
    return pl.pallas_call(
        _lru_step_kernel,
        grid=(1,),
        in_specs=[full(2 * BRANCH_W), full(3 * BRANCH_W), full(BRANCH_W),
                  _const((LRU_CONV, BRANCH_W)), vec, mat, vec, mat, vec, vec],
        out_specs=[full(BRANCH_W), full(3 * BRANCH_W), full(BRANCH_W)],
        out_shape=[jax.ShapeDtypeStruct((n, BRANCH_W), F32), jax.ShapeDtypeStruct((n, 3 * BRANCH_W), F32),
                   jax.ShapeDtypeStruct((n, BRANCH_W), F32)],
        compiler_params=_params(1),
        name="lru_step",
    )(cols, buf, h0, lw["conv_w"], lw["conv_b"], lw["w_r"], lw["b_r"], lw["w_i"], lw["b_i"], lw["lam"])


def _rwkv_token_math(cols, prev, mu, w0, w2p, a0, a2p, g2, k_k, k_a, r_k, ones_bd):
    mixed = cols + (prev - cols) * mu
    r = mixed[:, 0:BRANCH_W]
    k = mixed[:, BRANCH_W:2 * BRANCH_W]
    v = mixed[:, 2 * BRANCH_W:3 * BRANCH_W]
    la = mixed[:, 3 * BRANCH_W:3 * BRANCH_W + LORA_W]
    gl = mixed[:, 3 * BRANCH_W + LORA_W:]
    w = -_softplus(-(w0 + _dot(jnp.tanh(la).astype(BF16), w2p))) - 0.5
    decay = jnp.exp(-jnp.exp(w))
    a = jax.nn.sigmoid(a0 + _dot(la.astype(BF16), a2p))
    g = _dot(jax.nn.sigmoid(gl).astype(BF16), g2)
    kk = k * k_k
    kk = kk / jnp.maximum(jnp.sqrt(_head_sum(kk * kk, ones_bd)), 1e-12)
    k = k * (1.0 + (a - 1.0) * k_a)
    bonus = _head_sum(r * k * r_k, ones_bd) * v
    return r, decay, k, v, -kk, kk * a, g, bonus


_RW_PARAM_ORDER = ("mu", "w0", "w2p", "a0", "a2p", "g2", "k_k", "k_a", "r_k", "ones_bd")


def _rw_param_specs():
    vec = _const((1, BRANCH_W))
    return [_const((1, RWKV_COLS)), vec, _const((LORA_W, BRANCH_W)), vec, _const((LORA_W, BRANCH_W)),
            _const((GATE_LORA, BRANCH_W)), vec, vec, vec, _const((BRANCH_W, BRANCH_W))]


def _rwkv_pre_seq_kernel(cols_ref, *refs, tc):
    prm = [r[...] for r in refs[:10]]
    outs = refs[10:18]
    ext_ref = refs[18]
    t = pl.program_id(1)

    @pl.when(t == 0)
    def _():
        ext_ref[0:SUBLANES, :] = jnp.zeros((SUBLANES, RWKV_COLS), F32)

    @pl.when(t > 0)
    def _():
        ext_ref[0:SUBLANES, :] = ext_ref[tc:tc + SUBLANES, :]

    cols = cols_ref[0]
    ext_ref[SUBLANES:, :] = cols
    prev = ext_ref[pl.ds(SUBLANES - 1, tc), :]
    for ref, val in zip(outs, _rwkv_token_math(cols, prev, *prm)):
        ref[0] = val


def _rwkv_pre_seq(cols, rw, tc):
    b, t, _ = cols.shape
    blk = pl.BlockSpec((1, tc, BRANCH_W), lambda i, j: (i, j, 0))
    return pl.pallas_call(
        functools.partial(_rwkv_pre_seq_kernel, tc=tc),
        grid=(b, t // tc),
        in_specs=[pl.BlockSpec((1, tc, RWKV_COLS), lambda i, j: (i, j, 0))] + _rw_param_specs(),
        out_specs=[blk] * 8,
        out_shape=[jax.ShapeDtypeStruct((b, t, BRANCH_W), F32)] * 8,
        scratch_shapes=[pltpu.VMEM((tc + SUBLANES, RWKV_COLS), F32)],
        compiler_params=_params(2),
        name="rwkv_pre_seq",
    )(cols, *[rw[k] for k in _RW_PARAM_ORDER])


def _rwkv_pre_step_kernel(cols_ref, prev_ref, *refs):
    prm = [r[...] for r in refs[:10]]
    for ref, val in zip(refs[10:18], _rwkv_token_math(cols_ref[...], prev_ref[...], *prm)):
        ref[...] = val


def _rwkv_pre_step(cols, prev, rw):
    n = cols.shape[0]
    full = lambda w: pl.BlockSpec((n, w), lambda i: (0, 0))
    return pl.pallas_call(
        _rwkv_pre_step_kernel,
        grid=(1,),
        in_specs=[full(RWKV_COLS), full(RWKV_COLS)] + _rw_param_specs(),
        out_specs=[full(BRANCH_W)] * 8,
        out_shape=[jax.ShapeDtypeStruct((n, BRANCH_W), F32)] * 8,
        compiler_params=_params(1),
        name="rwkv_pre_step",
    )(cols, prev, *[rw[k] for k in _RW_PARAM_ORDER])


SCAN_GROUPS = 4
SCAN_GROUP_LANES = LANES // SCAN_GROUPS
SCAN_KEY_ROWS = RWKV_HEAD // SCAN_GROUPS


def _rwkv_scan_kernel(r_ref, w_ref, k_ref, a_ref, b_ref, v_ref, y_ref, so_ref, s_ref, exp_ref, *, tc, ni):
    t = pl.program_id(0)

    @pl.when(t == 0)
    def _():
        s_ref[...] = jnp.zeros_like(s_ref)

    def unpack(slot, tt):
        for kk, ref in enumerate((a_ref, b_ref, w_ref, k_ref, r_ref)):
            x = ref[tt]
            for g in range(SCAN_GROUPS):
                xg = x if g == 0 else pltpu.roll(x, LANES - SCAN_GROUP_LANES * g, 1)
                exp_ref[kk, slot, g * SCAN_KEY_ROWS:(g + 1) * SCAN_KEY_ROWS, :] = xg

    unpack(0, 0)

    def step(tt, slot):
        unpack(1 - slot, jnp.minimum(tt + 1, tc - 1))
        for i in range(ni):
            s = s_ref[i]
            sa = jnp.sum(s * exp_ref[0, slot], axis=0, keepdims=True)
            vi = v_ref[tt, pl.ds(i, 1), :]
            s = s * exp_ref[2, slot] + sa * exp_ref[1, slot] + vi * exp_ref[3, slot]
            s_ref[i] = s
            y_ref[tt, pl.ds(i, 1), :] = jnp.sum(s * exp_ref[4, slot], axis=0, keepdims=True)

    def two_steps(pair, carry):
        step(2 * pair, 0)
        step(2 * pair + 1, 1)
        return carry

    lax.fori_loop(0, tc // 2, two_steps, 0)

    @pl.when(t == pl.num_programs(0) - 1)
    def _():
        so_ref[...] = s_ref[...]


def _rwkv_scan(r, w, k, a, b, v, tc):
    t = r.shape[0]
    ni = v.shape[1]
    vec = pl.BlockSpec((tc, SCAN_KEY_ROWS, LANES), lambda j: (j, 0, 0))
    row = pl.BlockSpec((tc, ni, LANES), lambda j: (j, 0, 0))
    st = pl.BlockSpec((ni, RWKV_HEAD, LANES), lambda j: (0, 0, 0))
    return pl.pallas_call(
        functools.partial(_rwkv_scan_kernel, tc=tc, ni=ni),
        grid=(t // tc,),
        in_specs=[vec] * 5 + [row],
        out_specs=[row, st],
        out_shape=[jax.ShapeDtypeStruct((t, ni, LANES), F32), jax.ShapeDtypeStruct((ni, RWKV_HEAD, LANES), F32)],
        scratch_shapes=[pltpu.VMEM((ni, RWKV_HEAD, LANES), F32), pltpu.VMEM((5, 2, RWKV_HEAD, LANES), F32)],
        compiler_params=_params(1),
        name="rwkv_scan",
    )(r, w, k, a, b, v)


def _rwkv_step_kernel(s_ref, r_ref, w_ref, k_ref, a_ref, b_ref, v_ref, so_ref, y_ref):
    s = s_ref[...]
    sa = jnp.sum(s * a_ref[...], axis=-1, keepdims=True)
    s = s * w_ref[...] + sa * b_ref[...] + v_ref[...] * k_ref[...]
    so_ref[...] = s
    y_ref[...] = jnp.sum(s * r_ref[...], axis=-1, keepdims=True)


def _rwkv_step(s, r, w, k, a, b, v, bb):
    n, nh, hd, _ = s.shape
    st = pl.BlockSpec((bb, nh, hd, hd), lambda i: (i, 0, 0, 0))
    key = pl.BlockSpec((bb, nh, 1, hd), lambda i: (i, 0, 0, 0))
    val = pl.BlockSpec((bb, nh, hd, 1), lambda i: (i, 0, 0, 0))
    keyed = lambda z: z.reshape(n, nh, 1, hd)
    return pl.pallas_call(
        _rwkv_step_kernel,
        grid=(n // bb,),
        in_specs=[st] + [key] * 5 + [val],
        out_specs=[st, val],
        out_shape=[jax.ShapeDtypeStruct((n, nh, hd, hd), F32), jax.ShapeDtypeStruct((n, nh, hd, 1), F32)],
        compiler_params=_params(1),
        name="rwkv_step",
    )(s, keyed(r), keyed(w), keyed(k), keyed(a), keyed(b), v.reshape(n, nh, hd, 1))


def _t5_bucket_np(dist):
    n = np.maximum(dist, 0)
    max_exact = N_BUCKETS // 2
    nf = np.maximum(n, 1).astype(np.float32)
    large = max_exact + (np.log(nf / np.float32(max_exact)) / np.float32(math.log(MAX_DISTANCE / max_exact))
                         * np.float32(N_BUCKETS - max_exact)).astype(np.int32)
    large = np.minimum(large, N_BUCKETS - 1)
    return np.where(n < max_exact, n, large).astype(np.int32)


def _moba_seq_kernel(tab_ref, bko_ref, bkp_ref, qt_ref, ka_ref, vt_ref, o_ref,
                     km_ref, bias_ref, qa_ref, m_ref, l_ref, acc_ref, s_ref, *, nblk):
    bi = pl.program_id(0)
    i = pl.program_id(1)
    blk = MOBA_BLOCK
    far_bucket = N_BUCKETS - 1
    sel_rows = 2 * SUBLANES
    scale = ATT_HEAD ** -0.5

    @pl.when((bi == 0) & (i == 0))
    def _():
        bko = bko_ref[...]
        bkp = bkp_ref[...]
        for h in range(ATT_HEADS):
            c = tab_ref[far_bucket, h]
            own = jnp.zeros((blk, blk), F32)
            prev = jnp.zeros((blk, blk), F32)
            for j in range(N_BUCKETS):
                val = tab_ref[j, h] - c
                own = jnp.where(bko == j, val, own)
                prev = jnp.where(bkp == j, val, prev)
            bias_ref[h, 0] = jnp.where(bko < 0, NEG, own)
            bias_ref[h, 1] = prev

    @pl.when(i == 0)
    def _():
        for h in range(ATT_HEADS):
            km_ref[h] = jnp.zeros((sel_rows, ATT_HEAD), F32)
            for n in range(nblk):
                ks = jnp.sum(ka_ref[0, h, n * blk:(n + 1) * blk, :].astype(F32), axis=0, keepdims=True)
                km_ref[h, n:n + 1, :] = ks[:, 0:ATT_HEAD] * (1.0 / blk)

    row = lax.broadcasted_iota(I32, (sel_rows, blk), 0)
    row_f = row.astype(F32)
    for h in range(ATT_HEADS):
        qt = qt_ref[0, 0, h * ATT_HEAD:(h + 1) * ATT_HEAD, :]
        bs = jnp.dot(km_ref[h], qt, precision=lax.Precision.HIGHEST, preferred_element_type=F32)
        work = jnp.where(row < i, bs, -jnp.inf)
        sel = row == i
        for _ in range(MOBA_TOPK):
            mx = jnp.max(work, axis=0, keepdims=True)
            is_m = (work == mx) & (work > -jnp.inf)
            idx = jnp.min(jnp.where(is_m, row_f, 4.0 * LANES), axis=0, keepdims=True)
            pick = row_f == idx
            sel = sel | pick
            work = jnp.where(pick, -jnp.inf, work)
        selb = jnp.where(sel, 0.0, NEG)
        pad = jnp.zeros((LANES - ATT_HEAD - sel_rows, blk), F32)
        qa_ref[h] = jnp.concatenate([qt * scale, selb, pad], axis=0).astype(BF16)

    def block_step(n, kind):
        for h in range(ATT_HEADS):
            kb = ka_ref[0, h, pl.ds(pl.multiple_of(n * blk, blk), blk), :]
            s_ref[h] = _dot(kb, qa_ref[h])
        for h in range(ATT_HEADS):
            s = s_ref[h]
            vt = vt_ref[0, n, h * ATT_HEAD:(h + 1) * ATT_HEAD, :]
            if kind == "own":
                s = s + bias_ref[h, 0]
                m_new = jnp.max(s, axis=0, keepdims=True)
                p = jnp.exp(s - m_new)
                l_ref[h] = jnp.sum(p, axis=0, keepdims=True)
                acc_ref[h] = _dot(vt, p.astype(BF16))
            else:
                if kind == "prev":
                    s = s + bias_ref[h, 1] + jnp.where(i >= 1, 0.0, NEG)
                m_old = m_ref[h]
                m_new = jnp.maximum(m_old, jnp.max(s, axis=0, keepdims=True))
                alpha = jnp.exp(m_old - m_new)
                p = jnp.exp(s - m_new)
                l_ref[h] = alpha * l_ref[h] + jnp.sum(p, axis=0, keepdims=True)
                acc_ref[h] = alpha * acc_ref[h] + _dot(vt, p.astype(BF16))
            m_ref[h] = m_new

    block_step(i, "own")
    block_step(jnp.maximum(i - 1, 0), "prev")

    def far(n, carry):
        block_step(n, "far")
        return carry

    lax.fori_loop(0, jnp.maximum(i - 1, 0), far, 0)
    for h in range(ATT_HEADS):
        o_ref[0, 0, h * ATT_HEAD:(h + 1) * ATT_HEAD, :] = acc_ref[h] / l_ref[h]


def _moba_seq(q_t, k_aug, v_t, rel_bias):
    b, nblk, _, blk = q_t.shape
    nh, t = k_aug.shape[1], k_aug.shape[2]
    assert blk == MOBA_BLOCK and nblk <= 2 * SUBLANES
    d = np.arange(blk)[None, :] - np.arange(blk)[:, None]
    bko = jnp.asarray(np.where(d >= 0, _t5_bucket_np(d), -1).astype(np.int32))
    bkp = jnp.asarray(_t5_bucket_np(d + blk))
    q_blk = pl.BlockSpec((1, 1, BRANCH_W, blk), lambda i, j: (i, j, 0, 0))
    return pl.pallas_call(
        functools.partial(_moba_seq_kernel, nblk=nblk),
        grid=(b, nblk),
        in_specs=[pl.BlockSpec(memory_space=pltpu.SMEM), _const((blk, blk)), _const((blk, blk)), q_blk,
                  pl.BlockSpec((1, nh, t, LANES), lambda i, j: (i, 0, 0, 0), pipeline_mode=pl.Buffered(1)),
                  pl.BlockSpec((1, nblk, BRANCH_W, blk), lambda i, j: (i, 0, 0, 0), pipeline_mode=pl.Buffered(1))],
        out_specs=q_blk,
        out_shape=jax.ShapeDtypeStruct((b, nblk, BRANCH_W, blk), F32),
        scratch_shapes=[pltpu.VMEM((nh, 2 * SUBLANES, ATT_HEAD), F32), pltpu.VMEM((nh, 2, blk, blk), F32),
                        pltpu.VMEM((nh, LANES, blk), BF16), pltpu.VMEM((nh, 1, blk), F32),
                        pltpu.VMEM((nh, 1, blk), F32), pltpu.VMEM((nh, ATT_HEAD, blk), F32),
                        pltpu.VMEM((nh, blk, blk), F32)],
        compiler_params=_params(2),
        name="moba_seq",
    )(rel_bias, bko, bkp, q_t, k_aug, v_t)


def _moba_step_kernel(pt_ref, q_ref, kn_ref, vn_ref, rel_ref, bkt_ref, hs_ref, hst_ref, *refs, n_pages):
    kp = refs[:n_pages]
    vp = refs[n_pages:2 * n_pages]
    o_ref, lg_ref, bias_ref = refs[2 * n_pages:2 * n_pages + 3]
    pages_per_block = MOBA_BLOCK // PAGE_SIZE
    n_blocks = n_pages // pages_per_block
    scale = ATT_HEAD ** -0.5
    hi = lax.Precision.HIGHEST

    @pl.when(pl.program_id(0) == 0)
    def _():
        c_far = rel_ref[N_BUCKETS - 1:N_BUCKETS, :]
        bkt = bkt_ref[...]
        bias = jnp.zeros((PAGE_SIZE, LANES), F32)
        for j in range(N_BUCKETS):
            bias = jnp.where(bkt == j, rel_ref[j:j + 1, :] - c_far, bias)
        bias_ref[...] = bias

    q = q_ref[0]
    hs = hs_ref[...]
    hst = hst_ref[...]
    hst_b = hst.astype(BF16)
    rows = []
    for p in range(n_pages):
        kpg = kp[p][0]
        lg = _dot((kpg * q).astype(BF16), hst_b) * scale
        if p == n_pages - 1:
            lg = lg + bias_ref[...]
        lg_ref[p] = lg
        ks = jnp.sum(kpg, axis=0, keepdims=True)
        if p % pages_per_block == 0:
            rows.append(ks)
        else:
            rows[-1] = rows[-1] + ks
    rows = [ks * (1.0 / MOBA_BLOCK) * q for ks in rows] + [kn_ref[0] * q]
    rows.append(jnp.zeros((2 * SUBLANES - len(rows), BRANCH_W), F32))
    sc = jnp.dot(jnp.concatenate(rows, axis=0), hst, precision=hi, preferred_element_type=F32)
    sel = []
    for n in range(n_blocks):
        rank = jnp.zeros((1, LANES), I32)
        for j in range(n_blocks):
            if j != n:
                ahead = sc[j:j + 1, :] > sc[n:n + 1, :]
                if j < n:
                    ahead = ahead | (sc[j:j + 1, :] == sc[n:n + 1, :])
                rank = rank + ahead.astype(I32)
        sel.append(rank < MOBA_TOPK)
    pm = [jnp.where(sel[p // pages_per_block], 0.0, NEG) for p in range(n_pages)]
    s_self = sc[n_blocks:n_blocks + 1, :] * scale + (rel_ref[0:1, :] - rel_ref[N_BUCKETS - 1:N_BUCKETS, :])
    m_all = s_self
    for p in range(n_pages):
        m_all = jnp.maximum(m_all, jnp.max(lg_ref[p] + pm[p], axis=0, keepdims=True))
    w_self = jnp.exp(s_self - m_all)
    l_all = w_self
    out = jnp.zeros((1, BRANCH_W), F32)
    hs_b = hs.astype(BF16)
    for p in range(n_pages):
        e = jnp.exp(lg_ref[p] + pm[p] - m_all)
        l_all = l_all + jnp.sum(e, axis=0, keepdims=True)
        out = out + jnp.sum(_dot(e.astype(BF16), hs_b) * vp[p][0], axis=0, keepdims=True)
    wl = jnp.concatenate([w_self, l_all, jnp.zeros((SUBLANES - 2, LANES), F32)], axis=0)
    wl = jnp.dot(wl, hs, precision=hi, preferred_element_type=F32)
    o_ref[0] = (out + wl[0:1, :] * vn_ref[0]) / wl[1:2, :]


def _moba_step(q, k_new, v_new, cache_k, cache_v, page_table, rel_bias):
    n, n_pages = page_table.shape
    row = pl.BlockSpec((1, 1, BRANCH_W), lambda i, pt: (i, 0, 0))
    rel = jnp.pad(rel_bias, ((0, 0), (0, LANES - ATT_HEADS)))
    bkt = jnp.asarray(_t5_bucket_np(PAGE_SIZE - np.arange(PAGE_SIZE))[:, None])
    member = (np.arange(BRANCH_W)[None, :] // ATT_HEAD == np.arange(LANES)[:, None]).astype(np.float32)
    hs = jnp.asarray(member)
    hst = jnp.asarray(member.T)
    whole = lambda shape: pl.BlockSpec(shape, lambda i, pt: (0,) * len(shape))

    def page_spec(p):
        return pl.BlockSpec((1, PAGE_SIZE, BRANCH_W), lambda i, pt: (pt[i * n_pages + p], 0, 0))

    grid_spec = pltpu.PrefetchScalarGridSpec(
        num_scalar_prefetch=1,
        grid=(n,),
        in_specs=[row, row, row, whole((N_BUCKETS, LANES)), whole((PAGE_SIZE, 1)), whole((LANES, BRANCH_W)),
                  whole((BRANCH_W, LANES))] + [page_spec(p) for p in range(n_pages)] * 2,
        out_specs=row,
        scratch_shapes=[pltpu.VMEM((n_pages, PAGE_SIZE, LANES), F32), pltpu.VMEM((PAGE_SIZE, LANES), F32)],
    )
    ck = cache_k.reshape(cache_k.shape[0], PAGE_SIZE, BRANCH_W)
    cv = cache_v.reshape(cache_v.shape[0], PAGE_SIZE, BRANCH_W)
    return pl.pallas_call(
        functools.partial(_moba_step_kernel, n_pages=n_pages),
        grid_spec=grid_spec,
        out_shape=jax.ShapeDtypeStruct((n, 1, BRANCH_W), F32),
        compiler_params=_params(1),
        name="moba_step",
    )(page_table.reshape(-1), q[:, None, :], k_new[:, None, :], v_new[:, None, :], rel, bkt, hs, hst,
      *([ck] * n_pages), *([cv] * n_pages))[:, 0, :]


def _merge_kernel(x_ref, ya_ref, yr_ref, bon_ref, g_ref, yc_ref, gate_ref,
                  lnw_ref, lnb_ref, ones_ref, wb_ref, wo_ref, o_ref, *, yc_transposed):
    ones_bd = ones_ref[...]
    y = yr_ref[...]
    mu = _head_sum(y, ones_bd) * (1.0 / RWKV_HEAD)
    d = y - mu
    var = _head_sum(d * d, ones_bd) * (1.0 / RWKV_HEAD)
    yb = (d * lax.rsqrt(var + RWKV_GN_EPS) * lnw_ref[...] + lnb_ref[...] + bon_ref[...]) * g_ref[...]
    yc = yc_ref[0, 0].T if yc_transposed else yc_ref[...]
    merged = None
    for j, yj in enumerate((ya_ref[...], yb, yc)):
        gate = jax.nn.sigmoid(gate_ref[:, j * D_MODEL:(j + 1) * D_MODEL])
        term = _dot(yj.astype(BF16), wb_ref[j]) * gate
        merged = term if merged is None else merged + term
    o_ref[...] = x_ref[...] + _dot(merged.astype(BF16), wo_ref[...])


def _merge(x, ya, yr, bonus, g, yc, gates, mw, tm):
    m = x.shape[0]
    tok = lambda w: pl.BlockSpec((tm, w), lambda i: (i, 0))
    vec = _const((1, BRANCH_W))
    yc_transposed = yc.ndim == 4
    if yc_transposed:
        nblk = yc.shape[1]
        assert yc.shape[3] == tm
        yc_spec = pl.BlockSpec((1, 1, BRANCH_W, tm), lambda i: (i // nblk, i % nblk, 0, 0))
    else:
        yc_spec = tok(BRANCH_W)
    return pl.pallas_call(
        functools.partial(_merge_kernel, yc_transposed=yc_transposed),
        grid=(m // tm,),
        in_specs=[tok(D_MODEL)] + [tok(BRANCH_W)] * 4 + [yc_spec, tok(N_BRANCH * D_MODEL), vec, vec,
                  _const((BRANCH_W, BRANCH_W)), _const((N_BRANCH, BRANCH_W, D_MODEL)), _const((D_MODEL, D_MODEL))],
        out_specs=tok(D_MODEL),
        out_shape=jax.ShapeDtypeStruct((m, D_MODEL), F32),
        compiler_params=_params(1),
        name="merge",
    )(x, ya, yr, bonus, g, yc, gates, mw["ln_w"], mw["ln_b"], mw["ones_bd"], mw["w_branch"], mw["w_out"])


FFN_CHUNK = 512


def _ffn_kernel(*refs, seq_mode, final_norm, tm, tiles_per_seq):
    (x_ref, p_ref, ln2_ref, wup_ref, fcw_ref, fcb_ref, wdn_ref, ln3_ref, pg_ref, pp_ref, lnf_ref) = refs[:11]
    if seq_mode:
        o_ref, fc_ref, ext_ref = refs[11:14]
        i = pl.program_id(0)

        @pl.when(i % tiles_per_seq == 0)
        def _():
            ext_ref[0:SUBLANES, :] = jnp.zeros((SUBLANES, 2 * D_FF), F32)

        @pl.when(i % tiles_per_seq != 0)
        def _():
            ext_ref[0:SUBLANES, :] = ext_ref[tm:tm + SUBLANES, :]
    else:
        prev_ref, o_ref, u_ref = refs[11:14]

    x = x_ref[...]
    hb = _rms(x, ln2_ref[...]).astype(BF16)

    def conv_cols(lo, hi):
        u = _dot(hb, wup_ref[:, lo:hi])
        if seq_mode:
            ext_ref[SUBLANES:, lo:hi] = u
            u2 = ext_ref[pl.ds(SUBLANES - 2, tm), lo:hi]
            u1 = ext_ref[pl.ds(SUBLANES - 1, tm), lo:hi]
        else:
            u_ref[:, lo:hi] = u
            u2 = prev_ref[:, lo:hi]
            u1 = prev_ref[:, 2 * D_FF + lo:2 * D_FF + hi]
        return u2 * fcw_ref[0:1, lo:hi] + u1 * fcw_ref[1:2, lo:hi] + u * fcw_ref[2:3, lo:hi] + fcb_ref[:, lo:hi]

    acc = None
    for c in range(D_FF // FFN_CHUNK):
        lo, hi = c * FFN_CHUNK, (c + 1) * FFN_CHUNK
        act = _gelu_tanh(conv_cols(lo, hi)) * conv_cols(D_FF + lo, D_FF + hi)
        term = _dot(act.astype(BF16), wdn_ref[lo:hi, :])
        acc = term if acc is None else acc + term
    x = x + acc
    x = x + jax.nn.sigmoid(_dot(_rms(x, ln3_ref[...]).astype(BF16), pg_ref[...])) * _dot(p_ref[...].astype(BF16), pp_ref[...])
    o_ref[...] = _rms(x, lnf_ref[...]) if final_norm else x
    if seq_mode:
        fc_ref[0] = ext_ref[tm + SUBLANES - 2:tm + SUBLANES, :]


def _ffn(x, p, fw, tm, seq_len, prev=None, final_norm=False):
    m = x.shape[0]
    seq_mode = seq_len is not None
    tok = lambda w: pl.BlockSpec((tm, w), lambda i: (i, 0))
    vecd = _const((1, D_MODEL))
    in_specs = [tok(D_MODEL), tok(PLE_DIM), vecd, _const((D_MODEL, 2 * D_FF)), _const((FFN_CONV, 2 * D_FF)),
                _const((1, 2 * D_FF)), _const((D_FF, D_MODEL)), vecd, _const((D_MODEL, D_MODEL)),
                _const((PLE_DIM, D_MODEL)), vecd]
    args = [x, p, fw["ln2"], fw["ffn_up"], fw["conv_w"], fw["conv_b"], fw["ffn_down"], fw["ln3"],
            fw["ple_gate"], fw["ple_proj"], fw["ln_f"]]
    if seq_mode:
        tiles_per_seq = seq_len // tm
        out_specs = [tok(D_MODEL), pl.BlockSpec((1, FFN_CONV - 1, 2 * D_FF), lambda i: (i // tiles_per_seq, 0, 0))]
        out_shape = [jax.ShapeDtypeStruct((m, D_MODEL), F32),
                     jax.ShapeDtypeStruct((m // seq_len, FFN_CONV - 1, 2 * D_FF), F32)]
        scratch = [pltpu.VMEM((tm + SUBLANES, 2 * D_FF), F32)]
    else:
        tiles_per_seq = 1
        in_specs.append(tok((FFN_CONV - 1) * 2 * D_FF))
        args.append(prev)
        out_specs = [tok(D_MODEL), tok(2 * D_FF)]
        out_shape = [jax.ShapeDtypeStruct((m, D_MODEL), F32), jax.ShapeDtypeStruct((m, 2 * D_FF), F32)]
        scratch = []
    return pl.pallas_call(
        functools.partial(_ffn_kernel, seq_mode=seq_mode, final_norm=final_norm, tm=tm, tiles_per_seq=tiles_per_seq),
        grid=(m // tm,),
        in_specs=in_specs,
        out_specs=out_specs,
        out_shape=out_shape,
        scratch_shapes=scratch,
        compiler_params=_params(1),
        name="ffn",
    )(*args)


def _block_diag(w):
    g, n, _ = w.shape
    eye = jnp.eye(g, dtype=w.dtype)
    return (eye[:, None, :, None] * w[:, :, None, :]).reshape(g * n, g * n)


def _layer_weights(i, wt):
    row = lambda v: v.reshape(1, -1)
    ones_bd = _block_diag(jnp.ones((RWKV_HEADS, RWKV_HEAD, RWKV_HEAD), BF16))
    zeros_lora = jnp.zeros((DECAY_LORA, BRANCH_W), BF16)
    lw = {"conv_w": wt["lru_conv_w"][i], "conv_b": row(wt["lru_conv_b"][i]),
          "w_r": _block_diag(wt["lru_w_r"][i]).astype(BF16), "b_r": row(wt["lru_b_r"][i]),
          "w_i": _block_diag(wt["lru_w_i"][i]).astype(BF16), "b_i": row(wt["lru_b_i"][i]),
          "lam": row(wt["lru_lambda"][i])}
    rw = {"mu": row(wt["rwkv_mu"][i]), "w0": row(wt["rwkv_w0"][i]),
          "w2p": jnp.concatenate([wt["rwkv_w2"][i].astype(BF16), zeros_lora], axis=0),
          "a0": row(wt["rwkv_a0"][i]),
          "a2p": jnp.concatenate([zeros_lora, wt["rwkv_a2"][i].astype(BF16)], axis=0),
          "g2": wt["rwkv_g2"][i].astype(BF16), "k_k": row(wt["rwkv_k_k"][i]), "k_a": row(wt["rwkv_k_a"][i]),
          "r_k": row(wt["rwkv_r_k"][i]), "ones_bd": ones_bd}
    mw = {"ln_w": row(wt["rwkv_ln_w"][i]), "ln_b": row(wt["rwkv_ln_b"][i]), "ones_bd": ones_bd,
          "w_branch": wt["w_branch"][i].astype(BF16), "w_out": wt["w_out"][i].astype(BF16)}
    fw = {"ln2": row(wt["ln2"][i]), "ffn_up": wt["ffn_up"][i].astype(BF16), "conv_w": wt["ffn_conv_w"][i],
          "conv_b": row(wt["ffn_conv_b"][i]), "ffn_down": wt["ffn_down"][i].astype(BF16),
          "ln3": row(wt["ln3"][i]), "ple_gate": wt["ple_gate"][i].astype(BF16),
          "ple_proj": wt["ple_proj"][i].astype(BF16), "ln_f": row(wt["ln_f"])}
    w_in = wt["w_in"][i].astype(BF16)
    seg = lambda j: w_in[:, _IN_SEGS[j][0]:_IN_SEGS[j][1]]
    wk_heads = seg(3).reshape(D_MODEL, ATT_HEADS, ATT_HEAD).transpose(1, 0, 2)
    pw = {"w_in": w_in, "wq_t": seg(2).T, "wv_t": seg(4).T,
          "wk_aug": jnp.pad(wk_heads, ((0, 0), (0, 0), (0, LANES - ATT_HEAD)))}
    return {"ln1": row(wt["ln1"][i]), "w_in": w_in, "proj": pw, "lru": lw, "rwkv": rw, "merge": mw, "ffn": fw}


def _prompt_layer(x, p, lp, rel_bias, final_norm):
    b, t, _ = x.shape
    m = b * t
    nh, hd = RWKV_HEADS, RWKV_HEAD
    lru_c, rw_c, k, v, gates, q_t, v_t, k_aug = _in_proj_seq(x.reshape(m, D_MODEL), lp["ln1"], lp["proj"], b, t)

    ya, conv_new, h_new = _lru_seq(lru_c.reshape(b, t, 2 * BRANCH_W), lp["lru"], tc=256)

    r_, w_, k_, v_, a_, b_, g_, bonus = _rwkv_pre_seq(rw_c.reshape(b, t, RWKV_COLS), lp["rwkv"], tc=256)
    grp, kr = SCAN_GROUPS, SCAN_KEY_ROWS
    assert b * nh * grp == LANES

    def key_packed(z):
        return z.reshape(b, t, nh, grp, kr).transpose(1, 4, 3, 0, 2).reshape(t, kr, LANES)

    def row_major(z):
        return z.reshape(b, t, nh, hd // grp, grp).transpose(1, 3, 4, 0, 2).reshape(t, hd // grp, LANES)

    y_l, s_l = _rwkv_scan(key_packed(r_), key_packed(w_), key_packed(k_), key_packed(a_), key_packed(b_),
                          row_major(v_), tc=32)
    yr = y_l.reshape(t, hd // grp, grp, b, nh).transpose(3, 0, 4, 1, 2).reshape(m, BRANCH_W)
    s_l = s_l.reshape(hd // grp, grp, kr, grp, b, nh)
    s_l = jnp.stack([jnp.roll(s_l[:, :, :, c], c, axis=1) for c in range(grp)], axis=3)
    s_new = s_l.transpose(4, 5, 0, 3, 1, 2).reshape(b, nh, hd, hd)

    yc_t = _moba_seq(q_t, k_aug, v_t, rel_bias)

    flat = lambda z: z.reshape(m, BRANCH_W)
    x1 = _merge(x.reshape(m, D_MODEL), flat(ya), yr, flat(bonus), flat(g_), yc_t, gates, lp["merge"], tm=MOBA_BLOCK)
    x2, fc_new = _ffn(x1, p.reshape(m, PLE_DIM), lp["ffn"], tm=256, seq_len=t, final_norm=final_norm)
    new = {"k": k.reshape(b, t, ATT_HEADS, ATT_HEAD), "v": v.reshape(b, t, ATT_HEADS, ATT_HEAD),
           "lru_h": h_new[:, 0, :], "lru_conv": conv_new, "rwkv": s_new,
           "rwkv_shift": rw_c.reshape(b, t, RWKV_COLS)[:, -1, :], "ffn_conv": fc_new}
    return x2.reshape(b, t, D_MODEL), new


def _sample_layer(x, p, st, lp, rel_bias, final_norm):
    n = x.shape[0]
    lru_c, rw_c, q, k, v, gates = _in_proj(x.reshape(n, D_MODEL), lp["ln1"], lp["w_in"], tm=n)

    ya, conv_new, h_new = _lru_step(lru_c, st["lru_conv"].reshape(n, (LRU_CONV - 1) * BRANCH_W), st["lru_h"], lp["lru"])

    r_, w_, k_, v_, a_, b_, g_, bonus = _rwkv_pre_step(rw_c, st["rwkv_shift"], lp["rwkv"])
    s_new, y4 = _rwkv_step(st["rwkv"], r_, w_, k_, a_, b_, v_, bb=8)
    yr = y4.reshape(n, BRANCH_W)

    yc = _moba_step(q, k, v, st["cache_k"], st["cache_v"], st["page_table"], rel_bias)

    x1 = _merge(x.reshape(n, D_MODEL), ya, yr, bonus, g_, yc, gates, lp["merge"], tm=n)
    fc_prev = st["ffn_conv"].reshape(n, (FFN_CONV - 1) * 2 * D_FF)
    x2, u = _ffn(x1, p.reshape(n, PLE_DIM), lp["ffn"], tm=n, seq_len=None, prev=fc_prev, final_norm=final_norm)
    new = {"k": k.reshape(n, 1, ATT_HEADS, ATT_HEAD), "v": v.reshape(n, 1, ATT_HEADS, ATT_HEAD),
           "lru_h": h_new, "lru_conv": conv_new.reshape(n, LRU_CONV - 1, BRANCH_W), "rwkv": s_new,
           "rwkv_shift": rw_c, "ffn_conv": jnp.stack([st["ffn_conv"][:, 1, :], u], axis=1)}
    return x2.reshape(n, 1, D_MODEL), new


def kernel(x_prompt, x_sample, cache_k, cache_v, state_lru_h, state_lru_conv, state_rwkv, state_rwkv_shift, state_ffn_conv, page_table, p_prompt, p_sample, ln1, w_in, lru_conv_w, lru_conv_b, lru_w_r, lru_b_r, lru_w_i, lru_b_i, lru_lambda, rwkv_mu, rwkv_w0, rwkv_w2, rwkv_a0, rwkv_a2, rwkv_g2, rwkv_k_k, rwkv_k_a, rwkv_r_k, rwkv_ln_w, rwkv_ln_b, rel_bias, w_branch, w_out, ln2, ffn_up, ffn_conv_w, ffn_conv_b, ffn_down, ln3, ple_gate, ple_proj, ln_f):
    wt = dict(ln1=ln1, w_in=w_in, lru_conv_w=lru_conv_w, lru_conv_b=lru_conv_b, lru_w_r=lru_w_r, lru_b_r=lru_b_r,
              lru_w_i=lru_w_i, lru_b_i=lru_b_i, lru_lambda=lru_lambda, rwkv_mu=rwkv_mu, rwkv_w0=rwkv_w0,
              rwkv_w2=rwkv_w2, rwkv_a0=rwkv_a0, rwkv_a2=rwkv_a2, rwkv_g2=rwkv_g2, rwkv_k_k=rwkv_k_k,
              rwkv_k_a=rwkv_k_a, rwkv_r_k=rwkv_r_k, rwkv_ln_w=rwkv_ln_w, rwkv_ln_b=rwkv_ln_b, w_branch=w_branch,
              w_out=w_out, ln2=ln2, ffn_up=ffn_up, ffn_conv_w=ffn_conv_w, ffn_conv_b=ffn_conv_b, ffn_down=ffn_down,
              ln3=ln3, ple_gate=ple_gate, ple_proj=ple_proj, ln_f=ln_f)
    depth = w_in.shape[0]
    xp, xs = x_prompt, x_sample[:, 0, :]
    outs_p, outs_s = [], []
    for i in range(depth):
        lp = _layer_weights(i, wt)
        last = i == depth - 1
        st = {"cache_k": cache_k[i], "cache_v": cache_v[i], "page_table": page_table,
              "lru_h": state_lru_h[i], "lru_conv": state_lru_conv[i], "rwkv": state_rwkv[i],
              "rwkv_shift": state_rwkv_shift[i], "ffn_conv": state_ffn_conv[i]}
        xp, new_p = _prompt_layer(xp, p_prompt[i], lp, rel_bias, last)
        xs3, new_s = _sample_layer(xs, p_sample[i], st, lp, rel_bias, last)
        xs = xs3[:, 0, :]
        outs_p.append(new_p)
        outs_s.append(new_s)
    stack = lambda outs, name: jnp.stack([o[name] for o in outs])
    res = [xp, xs[:, None, :]]
    for name in ("k", "v", "lru_h", "lru_conv", "rwkv", "rwkv_shift", "ffn_conv"):
        res.append(stack(outs_p, name))
        res.append(stack(outs_s, name))
    return tuple(res)
```

```python
import functools
import math

import numpy as np
import jax
import jax.numpy as jnp
from jax import lax
from jax.experimental import pallas as pl
from jax.experimental.pallas import tpu as pltpu

F32 = jnp.float32
BF16 = jnp.bfloat16
I32 = jnp.int32

D_MODEL = 1024
DEPTH = 2
PAGE_SIZE = 128
N_BRANCH = 3
BRANCH_W = D_MODEL // 2
LRU_BLOCKS = 8
LRU_CONV = 4
LRU_C = 8.0
RWKV_HEAD = 64
RWKV_HEADS = BRANCH_W // RWKV_HEAD
DECAY_LORA = 64
AAA_LORA = 64
GATE_LORA = 128
RWKV_COLS = 3 * BRANCH_W + DECAY_LORA + AAA_LORA + GATE_LORA
RWKV_GN_EPS = 64e-5
ATT_HEAD = 64
ATT_HEADS = BRANCH_W // ATT_HEAD
MOBA_BLOCK = 256
MOBA_TOPK = 3
N_BUCKETS = 32
MAX_DISTANCE = 128
D_FF = 3 * D_MODEL
FFN_CONV = 3
PLE_DIM = 256
RMS_EPS = 1e-6
N_IN = 2 * BRANCH_W + RWKV_COLS + 3 * BRANCH_W + N_BRANCH * D_MODEL

_SEG_EDGES = (0, 2 * BRANCH_W, 2 * BRANCH_W + RWKV_COLS, 2 * BRANCH_W + RWKV_COLS + BRANCH_W,
              2 * BRANCH_W + RWKV_COLS + 2 * BRANCH_W, 2 * BRANCH_W + RWKV_COLS + 3 * BRANCH_W, N_IN)
_IN_SEGS = tuple(zip(_SEG_EDGES[:-1], _SEG_EDGES[1:]))

LANES = 128
SUBLANES = 8
VMEM_LIMIT = 56 * 1024 * 1024
NEG = -1e30
LORA_W = DECAY_LORA + AAA_LORA


def _params(n_axes):
    return pltpu.CompilerParams(dimension_semantics=("arbitrary",) * n_axes, vmem_limit_bytes=VMEM_LIMIT)


def _const(shape):
    return pl.BlockSpec(shape, lambda *_: (0,) * len(shape), pipeline_mode=pl.Buffered(1))


def _softplus(x):
    return jnp.maximum(x, 0.0) + jnp.log1p(jnp.exp(-jnp.abs(x)))


def _gelu_tanh(x):
    return 0.5 * x * (1.0 + jnp.tanh(math.sqrt(2.0 / math.pi) * (x + 0.044715 * (x * x * x))))


def _rms(x, g):
    return x * lax.rsqrt(jnp.mean(x * x, axis=-1, keepdims=True) + RMS_EPS) * g


def _dot(a, b):
    return jnp.dot(a, b, preferred_element_type=F32)


def _dot_nt(a, b, precision=None):
    return lax.dot_general(a, b, (((1,), (1,)), ((), ())), precision=precision, preferred_element_type=F32)


def _head_sum(x, ones_bd):
    hi = x.astype(BF16)
    lo = (x - hi.astype(F32)).astype(BF16)
    return _dot(hi, ones_bd) + _dot(lo, ones_bd)


def _in_proj_kernel(x_ref, g_ref, w_ref, *out_refs):
    xn = _rms(x_ref[...], g_ref[...]).astype(BF16)
    for ref, (lo, hi) in zip(out_refs, _IN_SEGS):
        ref[...] = _dot(xn, w_ref[:, lo:hi])


def _in_proj(x, g, w_bf16, tm):
    m = x.shape[0]
    widths = [hi - lo for lo, hi in _IN_SEGS]
    return pl.pallas_call(
        _in_proj_kernel,
        grid=(m // tm,),
        in_specs=[pl.BlockSpec((tm, D_MODEL), lambda i: (i, 0)), _const((1, D_MODEL)), _const((D_MODEL, N_IN))],
        out_specs=[pl.BlockSpec((tm, w), lambda i: (i, 0)) for w in widths],
        out_shape=[jax.ShapeDtypeStruct((m, w), F32) for w in widths],
        compiler_params=_params(1),
        name="in_proj",
    )(x, g, w_bf16)


_SEQ_SEGS = (0, 1, 5)


def _in_proj_seq_kernel(x_ref, g_ref, w_ref, wt_ref, wka_ref, *out_refs, nblk):
    xn = _rms(x_ref[...], g_ref[...]).astype(BF16)
    for ref, s in zip(out_refs[:3], _SEQ_SEGS):
        lo, hi = _IN_SEGS[s]
        ref[...] = _dot(xn, w_ref[:, lo:hi])
    for j, ref in enumerate(out_refs[3:6]):
        ref[0] = _dot_nt(wt_ref[j], xn)
    ka_ref = out_refs[6]
    n = pl.program_id(0) % nblk
    onehot = (lax.broadcasted_iota(I32, (1, LANES), 1) == ATT_HEAD + n).astype(F32)
    for h in range(ATT_HEADS):
        ka_ref[0, h] = (_dot(xn, wka_ref[h]) + onehot).astype(BF16)


def _in_proj_seq(x, g, pw, b, t):
    m = x.shape[0]
    tm = MOBA_BLOCK
    nblk = t // tm
    widths = [_IN_SEGS[s][1] - _IN_SEGS[s][0] for s in _SEQ_SEGS]
    blk_t = pl.BlockSpec((1, BRANCH_W, tm), lambda i: (i // nblk, 0, i % nblk))
    return pl.pallas_call(
        functools.partial(_in_proj_seq_kernel, nblk=nblk),
        grid=(m // tm,),
        in_specs=[pl.BlockSpec((tm, D_MODEL), lambda i: (i, 0)), _const((1, D_MODEL)), _const((D_MODEL, N_IN)),
                  _const((3, BRANCH_W, D_MODEL)), _const((ATT_HEADS, D_MODEL, LANES))],
        out_specs=[pl.BlockSpec((tm, w), lambda i: (i, 0)) for w in widths]
                  + [blk_t] * 3 + [pl.BlockSpec((1, ATT_HEADS, tm, LANES), lambda i: (i // nblk, 0, i % nblk, 0))],
        out_shape=[jax.ShapeDtypeStruct((m, w), F32) for w in widths]
                  + [jax.ShapeDtypeStruct((b, BRANCH_W, t), F32)] * 3
                  + [jax.ShapeDtypeStruct((b, ATT_HEADS, t, LANES), BF16)],
        compiler_params=_params(1),
        name="in_proj_seq",
    )(x, g, pw["w_in"], pw["w_qkv_t"], pw["wk_aug"])


def _lru_gates(xc, ga_unused, wr, br, wi, bi, lam):
    xcb = xc.astype(BF16)
    r = jax.nn.sigmoid(_dot(xcb, wr) + br)
    ig = jax.nn.sigmoid(_dot(xcb, wi) + bi)
    log_a = (-LRU_C * r) * _softplus(-lam)
    a = jnp.exp(log_a)
    bx = jnp.sqrt(1.0 - jnp.exp(2.0 * log_a)) * (ig * xc)
    return a, bx


def _lru_seq_kernel(cols_ref, cw_ref, cb_ref, wr_ref, br_ref, wi_ref, bi_ref, lam_ref,
                    y_ref, conv_ref, h_ref, ext_ref, hc_ref, *, tc):
    t = pl.program_id(1)

    @pl.when(t == 0)
    def _():
        ext_ref[0:SUBLANES, :] = jnp.zeros((SUBLANES, BRANCH_W), F32)
        hc_ref[...] = jnp.zeros_like(hc_ref)

    @pl.when(t > 0)
    def _():
        ext_ref[0:SUBLANES, :] = ext_ref[tc:tc + SUBLANES, :]

    xa = cols_ref[0, :, 0:BRANCH_W]
    ga = cols_ref[0, :, BRANCH_W:]
    ext_ref[SUBLANES:, :] = xa
    xc = ext_ref[pl.ds(SUBLANES - 3, tc), :] * cw_ref[0:1, :]
    xc = xc + ext_ref[pl.ds(SUBLANES - 2, tc), :] * cw_ref[1:2, :]
    xc = xc + ext_ref[pl.ds(SUBLANES - 1, tc), :] * cw_ref[2:3, :]
    xc = xc + xa * cw_ref[3:4, :] + cb_ref[...]
    a, bx = _lru_gates(xc, ga, wr_ref[...], br_ref[...], wi_ref[...], bi_ref[...], lam_ref[...])
    row = lax.broadcasted_iota(I32, (tc, BRANCH_W), 0)
    d = 1
    while d < tc:
        keep = row >= d
        a_s = jnp.where(keep, pltpu.roll(a, d, 0), 1.0)
        b_s = jnp.where(keep, pltpu.roll(bx, d, 0), 0.0)
        bx = a * b_s + bx
        a = a * a_s
        d *= 2
    h = a * hc_ref[...] + bx
    hc_ref[...] = h[tc - 1:tc, :]
    y_ref[0] = h * _gelu_tanh(ga)
    conv_ref[0] = ext_ref[tc + SUBLANES - 3:tc + SUBLANES, :]
    h_ref[0] = h[tc - 1:tc, :]


def _lru_seq(cols, lw, tc):
    b, t, _ = cols.shape
    vec = _const((1, BRANCH_W))
    mat = _const((BRANCH_W, BRANCH_W))
    return pl.pallas_call(
        functools.partial(_lru_seq_kernel, tc=tc),
        grid=(b, t // tc),
        in_specs=[pl.BlockSpec((1, tc, 2 * BRANCH_W), lambda i, j: (i, j, 0)),
                  _const((LRU_CONV, BRANCH_W)), vec, mat, vec, mat, vec, vec],
        out_specs=[pl.BlockSpec((1, tc, BRANCH_W), lambda i, j: (i, j, 0)),
                   pl.BlockSpec((1, LRU_CONV - 1, BRANCH_W), lambda i, j: (i, 0, 0)),
                   pl.BlockSpec((1, 1, BRANCH_W), lambda i, j: (i, 0, 0))],
        out_shape=[jax.ShapeDtypeStruct((b, t, BRANCH_W), F32),
                   jax.ShapeDtypeStruct((b, LRU_CONV - 1, BRANCH_W), F32),
                   jax.ShapeDtypeStruct((b, 1, BRANCH_W), F32)],
        scratch_shapes=[pltpu.VMEM((tc + SUBLANES, BRANCH_W), F32), pltpu.VMEM((1, BRANCH_W), F32)],
        compiler_params=_params(2),
        name="lru_seq",
    )(cols, lw["conv_w"], lw["conv_b"], lw["w_r"], lw["b_r"], lw["w_i"], lw["b_i"], lw["lam"])


def _lru_step_kernel(cols_ref, buf_ref, h0_ref, cw_ref, cb_ref, wr_ref, br_ref, wi_ref, bi_ref, lam_ref,
                     y_ref, conv_ref, h_ref):
    xa = cols_ref[:, 0:BRANCH_W]
    ga = cols_ref[:, BRANCH_W:]
    xc = buf_ref[:, 0:BRANCH_W] * cw_ref[0:1, :]
    xc = xc + buf_ref[:, BRANCH_W:2 * BRANCH_W] * cw_ref[1:2, :]
    xc = xc + buf_ref[:, 2 * BRANCH_W:] * cw_ref[2:3, :]
    xc = xc + xa * cw_ref[3:4, :] + cb_ref[...]
    a, bx = _lru_gates(xc, ga, wr_ref[...], br_ref[...], wi_ref[...], bi_ref[...], lam_ref[...])
    h = a * h0_ref[...] + bx
    y_ref[...] = h * _gelu_tanh(ga)
    conv_ref[:, 0:2 * BRANCH_W] = buf_ref[:, BRANCH_W:]
    conv_ref[:, 2 * BRANCH_W:] = xa
    h_ref[...] = h


def _lru_step(cols, buf, h0, lw):
    n = cols.shape[0]
    full = lambda w: pl.BlockSpec((n, w), lambda i: (0, 0))
    vec = _const((1, BRANCH_W))
    mat = _const((BRANCH_W, BRANCH_W))
    return pl.pallas_call(
        _lru_step_kernel,
        grid=(1,),
        in_specs=[full(2 * BRANCH_W), full(3 * BRANCH_W), full(BRANCH_W),
                  _const((LRU_CONV, BRANCH_W)), vec, mat, vec, mat, vec, vec],
        out_specs=[full(BRANCH_W), full(3 * BRANCH_W), full(BRANCH_W)],
        out_shape=[jax.ShapeDtypeStruct((n, BRANCH_W), F32), jax.ShapeDtypeStruct((n, 3 * BRANCH_W), F32),
                   jax.ShapeDtypeStruct((n, BRANCH_W), F32)],
        compiler_params=_params(1),
        name="lru_step",
    )(cols, buf, h0, lw["conv_w"], lw["conv_b"], lw["w_r"], lw["b_r"], lw["w_i"], lw["b_i"], lw["lam"])


def _rwkv_token_math(cols, prev, mu, w0, w2p, a0, a2p, g2, k_k, k_a, r_k, ones_bd):
    mixed = cols + (prev - cols) * mu
    r = mixed[:, 0:BRANCH_W]
    k = mixed[:, BRANCH_W:2 * BRANCH_W]
    v = mixed[:, 2 * BRANCH_W:3 * BRANCH_W]
    la = mixed[:, 3 * BRANCH_W:3 * BRANCH_W + LORA_W]
    gl = mixed[:, 3 * BRANCH_W + LORA_W:]
    w = -_softplus(-(w0 + _dot(jnp.tanh(la).astype(BF16), w2p))) - 0.5
    decay = jnp.exp(-jnp.exp(w))
    a = jax.nn.sigmoid(a0 + _dot(la.astype(BF16), a2p))
    g = _dot(jax.nn.sigmoid(gl).astype(BF16), g2)
    kk = k * k_k
    kk = kk / jnp.maximum(jnp.sqrt(_head_sum(kk * kk, ones_bd)), 1e-12)
    k = k * (1.0 + (a - 1.0) * k_a)
    bonus = _head_sum(r * k * r_k, ones_bd) * v
    return r, decay, k, v, -kk, kk * a, g, bonus


_RW_PARAM_ORDER = ("mu", "w0", "w2p", "a0", "a2p", "g2", "k_k", "k_a", "r_k", "ones_bd")


def _rw_param_specs():
    vec = _const((1, BRANCH_W))
    return [_const((1, RWKV_COLS)), vec, _const((LORA_W, BRANCH_W)), vec, _const((LORA_W, BRANCH_W)),
            _const((GATE_LORA, BRANCH_W)), vec, vec, vec, _const((BRANCH_W, BRANCH_W))]


def _rwkv_pre_seq_kernel(cols_ref, *refs, tc):
    prm = [r[...] for r in refs[:10]]
    outs = refs[10:18]
    ext_ref = refs[18]
    t = pl.program_id(1)

    @pl.when(t == 0)
    def _():
        ext_ref[0:SUBLANES, :] = jnp.zeros((SUBLANES, RWKV_COLS), F32)

    @pl.when(t > 0)
    def _():
        ext_ref[0:SUBLANES, :] = ext_ref[tc:tc + SUBLANES, :]

    cols = cols_ref[0]
    ext_ref[SUBLANES:, :] = cols
    prev = ext_ref[pl.ds(SUBLANES - 1, tc), :]
    for ref, val in zip(outs, _rwkv_token_math(cols, prev, *prm)):
        ref[0] = val


def _rwkv_pre_seq(cols, rw, tc):
    b, t, _ = cols.shape
    blk = pl.BlockSpec((1, tc, BRANCH_W), lambda i, j: (i, j, 0))
    return pl.pallas_call(
        functools.partial(_rwkv_pre_seq_kernel, tc=tc),
        grid=(b, t // tc),
        in_specs=[pl.BlockSpec((1, tc, RWKV_COLS), lambda i, j: (i, j, 0))] + _rw_param_specs(),
        out_specs=[blk] * 8,
        out_shape=[jax.ShapeDtypeStruct((b, t, BRANCH_W), F32)] * 8,
        scratch_shapes=[pltpu.VMEM((tc + SUBLANES, RWKV_COLS), F32)],
        compiler_params=_params(2),
        name="rwkv_pre_seq",
    )(cols, *[rw[k] for k in _RW_PARAM_ORDER])


def _rwkv_pre_step_kernel(cols_ref, prev_ref, *refs):
    prm = [r[...] for r in refs[:10]]
    for ref, val in zip(refs[10:18], _rwkv_token_math(cols_ref[...], prev_ref[...], *prm)):
        ref[...] = val


def _rwkv_pre_step(cols, prev, rw):
    n = cols.shape[0]
    full = lambda w: pl.BlockSpec((n, w), lambda i: (0, 0))
    return pl.pallas_call(
        _rwkv_pre_step_kernel,
        grid=(1,),
        in_specs=[full(RWKV_COLS), full(RWKV_COLS)] + _rw_param_specs(),
        out_specs=[full(BRANCH_W)] * 8,
        out_shape=[jax.ShapeDtypeStruct((n, BRANCH_W), F32)] * 8,
        compiler_params=_params(1),
        name="rwkv_pre_step",
    )(cols, prev, *[rw[k] for k in _RW_PARAM_ORDER])


SCAN_GROUPS = 4
SCAN_GROUP_LANES = LANES // SCAN_GROUPS
SCAN_KEY_ROWS = RWKV_HEAD // SCAN_GROUPS


def _rwkv_scan_kernel(r_ref, w_ref, k_ref, a_ref, b_ref, v_ref, y_ref, so_ref, s_ref, exp_ref, *, tc, ni):
    t = pl.program_id(0)

    @pl.when(t == 0)
    def _():
        s_ref[...] = jnp.zeros_like(s_ref)

    def unpack(slot, tt):
        for kk, ref in enumerate((a_ref, b_ref, w_ref, k_ref, r_ref)):
            x = ref[tt]
            for g in range(SCAN_GROUPS):
                xg = x if g == 0 else pltpu.roll(x, LANES - SCAN_GROUP_LANES * g, 1)
                exp_ref[kk, slot, g * SCAN_KEY_ROWS:(g + 1) * SCAN_KEY_ROWS, :] = xg

    unpack(0, 0)

    def step(tt, slot):
        unpack(1 - slot, jnp.minimum(tt + 1, tc - 1))
        for i in range(ni):
            s = s_ref[i]
            sa = jnp.sum(s * exp_ref[0, slot], axis=0, keepdims=True)
            vi = v_ref[tt, pl.ds(i, 1), :]
            s = s * exp_ref[2, slot] + sa * exp_ref[1, slot] + vi * exp_ref[3, slot]
            s_ref[i] = s
            y_ref[tt, pl.ds(i, 1), :] = jnp.sum(s * exp_ref[4, slot], axis=0, keepdims=True)

    def two_steps(pair, carry):
        step(2 * pair, 0)
        step(2 * pair + 1, 1)
        return carry

    lax.fori_loop(0, tc // 2, two_steps, 0)

    @pl.when(t == pl.num_programs(0) - 1)
    def _():
        so_ref[...] = s_ref[...]


def _rwkv_scan(r, w, k, a, b, v, tc):
    t = r.shape[0]
    ni = v.shape[1]
    vec = pl.BlockSpec((tc, SCAN_KEY_ROWS, LANES), lambda j: (j, 0, 0))
    row = pl.BlockSpec((tc, ni, LANES), lambda j: (j, 0, 0))
    st = pl.BlockSpec((ni, RWKV_HEAD, LANES), lambda j: (0, 0, 0))
    return pl.pallas_call(
        functools.partial(_rwkv_scan_kernel, tc=tc, ni=ni),
        grid=(t // tc,),
        in_specs=[vec] * 5 + [row],
        out_specs=[row, st],
        out_shape=[jax.ShapeDtypeStruct((t, ni, LANES), F32), jax.ShapeDtypeStruct((ni, RWKV_HEAD, LANES), F32)],
        scratch_shapes=[pltpu.VMEM((ni, RWKV_HEAD, LANES), F32), pltpu.VMEM((5, 2, RWKV_HEAD, LANES), F32)],
        compiler_params=_params(1),
        name="rwkv_scan",
    )(r, w, k, a, b, v)


def _rwkv_step_kernel(s_ref, r_ref, w_ref, k_ref, a_ref, b_ref, v_ref, so_ref, y_ref):
    a = a_ref[0]
    b = b_ref[0]
    w = w_ref[0]
    k = k_ref[0]
    r = r_ref[0]
    for i in range(RWKV_HEAD):
        s = s_ref[0, i]
        sa = jnp.sum(s * a, axis=0, keepdims=True)
        s = s * w + sa * b + v_ref[0, pl.ds(i, 1), :] * k
        so_ref[0, i] = s
        y_ref[0, pl.ds(i, 1), :] = jnp.sum(s * r, axis=0, keepdims=True)


def _rwkv_step(s, layer, r, w, k, a, b, v):
    _, nh, hd, _, n = s.shape
    st = pl.BlockSpec((1, hd, hd, n), lambda i: (i, 0, 0, 0))
    vec = pl.BlockSpec((1, hd, n), lambda i: (i, 0, 0))
    return pl.pallas_call(
        _rwkv_step_kernel,
        grid=(nh,),
        in_specs=[pl.BlockSpec((None, 1, hd, hd, n), lambda i: (layer, i, 0, 0, 0))] + [vec] * 6,
        out_specs=[st, vec],
        out_shape=[jax.ShapeDtypeStruct((nh, hd, hd, n), F32), jax.ShapeDtypeStruct((nh, hd, n), F32)],
        compiler_params=_params(1),
        name="rwkv_step",
    )(s, r, w, k, a, b, v)


def _t5_bucket_np(dist):
    n = np.maximum(dist, 0)
    max_exact = N_BUCKETS // 2
    nf = np.maximum(n, 1).astype(np.float32)
    large = max_exact + (np.log(nf / np.float32(max_exact)) / np.float32(math.log(MAX_DISTANCE / max_exact))
                         * np.float32(N_BUCKETS - max_exact)).astype(np.int32)
    large = np.minimum(large, N_BUCKETS - 1)
    return np.where(n < max_exact, n, large).astype(np.int32)


def _moba_seq_kernel(tab_ref, bko_ref, bkp_ref, qt_ref, ka_ref, vt_ref, o_ref,
                     km_ref, bias_ref, qa_ref, m_ref, l_ref, acc_ref, s_ref, *, nblk):
    bi = pl.program_id(0)
    i = pl.program_id(1)
    blk = MOBA_BLOCK
    far_bucket = N_BUCKETS - 1
    sel_rows = 2 * SUBLANES
    scale = ATT_HEAD ** -0.5

    @pl.when((bi == 0) & (i == 0))
    def _():
        bko = bko_ref[...]
        bkp = bkp_ref[...]
        for h in range(ATT_HEADS):
            c = tab_ref[far_bucket, h]
            own = jnp.zeros((blk, blk), F32)
            prev = jnp.zeros((blk, blk), F32)
            for j in range(N_BUCKETS):
                val = tab_ref[j, h] - c
                own = jnp.where(bko == j, val, own)
                prev = jnp.where(bkp == j, val, prev)
            bias_ref[h, 0] = jnp.where(bko < 0, NEG, own)
            bias_ref[h, 1] = prev

    @pl.when(i == 0)
    def _():
        for h in range(ATT_HEADS):
            km_ref[h] = jnp.zeros((sel_rows, ATT_HEAD), F32)
            for n in range(nblk):
                ks = jnp.sum(ka_ref[0, h, n * blk:(n + 1) * blk, :].astype(F32), axis=0, keepdims=True)
                km_ref[h, n:n + 1, :] = ks[:, 0:ATT_HEAD] * (1.0 / blk)

    row = lax.broadcasted_iota(I32, (sel_rows, blk), 0)
    row_f = row.astype(F32)
    for h in range(ATT_HEADS):
        qt = qt_ref[0, h * ATT_HEAD:(h + 1) * ATT_HEAD, :]
        bs = jnp.dot(km_ref[h], qt, precision=lax.Precision.HIGHEST, preferred_element_type=F32)
        work = jnp.where(row < i, bs, -jnp.inf)
        sel = row == i
        for _ in range(MOBA_TOPK):
            mx = jnp.max(work, axis=0, keepdims=True)
            is_m = (work == mx) & (work > -jnp.inf)
            idx = jnp.min(jnp.where(is_m, row_f, 4.0 * LANES), axis=0, keepdims=True)
            pick = row_f == idx
            sel = sel | pick
            work = jnp.where(pick, -jnp.inf, work)
        selb = jnp.where(sel, 0.0, NEG)
        pad = jnp.zeros((LANES - ATT_HEAD - sel_rows, blk), F32)
        qa_ref[h] = jnp.concatenate([qt * scale, selb, pad], axis=0).astype(BF16)

    def block_step(n, kind):
        for h in range(ATT_HEADS):
            kb = ka_ref[0, h, pl.ds(pl.multiple_of(n * blk, blk), blk), :]
            s_ref[h] = _dot(kb, qa_ref[h])
        for h in range(ATT_HEADS):
            s = s_ref[h]
            vt = vt_ref[0, h * ATT_HEAD:(h + 1) * ATT_HEAD, pl.ds(pl.multiple_of(n * blk, blk), blk)].astype(BF16)
            if kind == "own":
                s = s + bias_ref[h, 0]
                m_new = jnp.max(s, axis=0, keepdims=True)
                p = jnp.exp(s - m_new)
                l_ref[h] = jnp.sum(p, axis=0, keepdims=True)
                acc_ref[h] = _dot(vt, p.astype(BF16))
            else:
                if kind == "prev":
                    s = s + bias_ref[h, 1] + jnp.where(i >= 1, 0.0, NEG)
                m_old = m_ref[h]
                m_new = jnp.maximum(m_old, jnp.max(s, axis=0, keepdims=True))
                alpha = jnp.exp(m_old - m_new)
                p = jnp.exp(s - m_new)
                l_ref[h] = alpha * l_ref[h] + jnp.sum(p, axis=0, keepdims=True)
                acc_ref[h] = alpha * acc_ref[h] + _dot(vt, p.astype(BF16))
            m_ref[h] = m_new

    block_step(i, "own")
    block_step(jnp.maximum(i - 1, 0), "prev")

    def far(n, carry):
        block_step(n, "far")
        return carry

    lax.fori_loop(0, jnp.maximum(i - 1, 0), far, 0)
    for h in range(ATT_HEADS):
        o_ref[0, h * ATT_HEAD:(h + 1) * ATT_HEAD, :] = acc_ref[h] / l_ref[h]


def _moba_seq(q_t, k_aug, v_t, rel_bias):
    b, nh, t, _ = k_aug.shape
    blk = MOBA_BLOCK
    nblk = t // blk
    assert nblk <= 2 * SUBLANES
    d = np.arange(blk)[None, :] - np.arange(blk)[:, None]
    bko = jnp.asarray(np.where(d >= 0, _t5_bucket_np(d), -1).astype(np.int32))
    bkp = jnp.asarray(_t5_bucket_np(d + blk))
    q_blk = pl.BlockSpec((1, BRANCH_W, blk), lambda i, j: (i, 0, j))
    return pl.pallas_call(
        functools.partial(_moba_seq_kernel, nblk=nblk),
        grid=(b, nblk),
        in_specs=[pl.BlockSpec(memory_space=pltpu.SMEM), _const((blk, blk)), _const((blk, blk)), q_blk,
                  pl.BlockSpec((1, nh, t, LANES), lambda i, j: (i, 0, 0, 0), pipeline_mode=pl.Buffered(1)),
                  pl.BlockSpec((1, BRANCH_W, t), lambda i, j: (i, 0, 0), pipeline_mode=pl.Buffered(1))],
        out_specs=q_blk,
        out_shape=jax.ShapeDtypeStruct((b, BRANCH_W, t), F32),
        scratch_shapes=[pltpu.VMEM((nh, 2 * SUBLANES, ATT_HEAD), F32), pltpu.VMEM((nh, 2, blk, blk), F32),
                        pltpu.VMEM((nh, LANES, blk), BF16), pltpu.VMEM((nh, 1, blk), F32),
                        pltpu.VMEM((nh, 1, blk), F32), pltpu.VMEM((nh, ATT_HEAD, blk), F32),
                        pltpu.VMEM((nh, blk, blk), F32)],
        compiler_params=_params(2),
        name="moba_seq",
    )(rel_bias, bko, bkp, q_t, k_aug, v_t)


def _moba_step_kernel(pt_ref, q_ref, kn_ref, vn_ref, relt_ref, bkt_ref, *refs, n_pages):
    kp = refs[:n_pages]
    vp = refs[n_pages:2 * n_pages]
    o_ref, acc_ref = refs[2 * n_pages:2 * n_pages + 2]
    pages_per_block = MOBA_BLOCK // PAGE_SIZE
    n_blocks = n_pages // pages_per_block
    scale = ATT_HEAD ** -0.5
    heads = range(ATT_HEADS)

    def per_head(fn):
        return jnp.concatenate([fn(h) for h in heads], axis=0)

    relt = relt_ref[...]
    c_far = relt[:, N_BUCKETS - 1:N_BUCKETS]
    bkt = bkt_ref[...]
    bias_last = jnp.zeros((ATT_HEADS, PAGE_SIZE), F32)
    for j in range(N_BUCKETS):
        bias_last = jnp.where(bkt == j, relt[:, j:j + 1] - c_far, bias_last)

    q = q_ref[0]
    qb = [jnp.broadcast_to(q[h], (ATT_HEAD, PAGE_SIZE)) for h in heads]
    logit, rowsum = [], []
    for p in range(n_pages):
        raw = per_head(lambda h: jnp.sum(kp[p][0, 0, h] * qb[h], axis=0, keepdims=True))
        rowsum.append(jnp.sum(raw, axis=1, keepdims=True))
        lg = raw * scale
        logit.append(lg + bias_last if p == n_pages - 1 else lg)

    sc = []
    for n in range(n_blocks):
        tot = rowsum[n * pages_per_block]
        for j in range(1, pages_per_block):
            tot = tot + rowsum[n * pages_per_block + j]
        sc.append(tot * (1.0 / MOBA_BLOCK))
    sel = []
    for n in range(n_blocks):
        rank = jnp.zeros((ATT_HEADS, 1), I32)
        for j in range(n_blocks):
            if j != n:
                ahead = sc[j] > sc[n]
                if j < n:
                    ahead = ahead | (sc[j] == sc[n])
                rank = rank + ahead.astype(I32)
        sel.append(rank < MOBA_TOPK)

    s_self = per_head(lambda h: jnp.sum(q[h] * kn_ref[0, h], axis=0, keepdims=True)) * scale + (relt[:, 0:1] - c_far)
    m_all = s_self
    for p in range(n_pages):
        m_all = jnp.maximum(m_all, jnp.where(sel[p // pages_per_block], jnp.max(logit[p], axis=1, keepdims=True), -jnp.inf))
    w_self = jnp.exp(s_self - m_all)
    l_all = w_self
    acc_ref[...] = jnp.zeros_like(acc_ref)
    for p in range(n_pages):
        e = jnp.where(sel[p // pages_per_block], jnp.exp(logit[p] - m_all), 0.0)
        l_all = l_all + jnp.sum(e, axis=1, keepdims=True)
        for h in heads:
            acc_ref[h] += vp[p][0, 0, h] * e[h:h + 1, :]
    for h in heads:
        out = jnp.sum(acc_ref[h], axis=1, keepdims=True) + w_self[h:h + 1, :] * vn_ref[0, h]
        o_ref[0, h] = out / l_all[h:h + 1, :]


def _moba_step(q, k_new, v_new, cache_k, cache_v, layer, page_table, rel_bias):
    n, n_pages = page_table.shape
    col = lambda z: z.reshape(n, ATT_HEADS, ATT_HEAD, 1)
    col_spec = pl.BlockSpec((1, ATT_HEADS, ATT_HEAD, 1), lambda i, pt: (i, 0, 0, 0))
    relt = jnp.pad(rel_bias.T, ((0, 0), (0, LANES - N_BUCKETS)))
    bkt = jnp.asarray(_t5_bucket_np(PAGE_SIZE - np.arange(PAGE_SIZE))[None, :])

    def page_spec(p):
        return pl.BlockSpec((1, 1, ATT_HEADS, ATT_HEAD, PAGE_SIZE),
                            lambda i, pt: (layer, pt[i * n_pages + p], 0, 0, 0))

    grid_spec = pltpu.PrefetchScalarGridSpec(
        num_scalar_prefetch=1,
        grid=(n,),
        in_specs=[col_spec, col_spec, col_spec,
                  pl.BlockSpec((ATT_HEADS, LANES), lambda i, pt: (0, 0)),
                  pl.BlockSpec((1, PAGE_SIZE), lambda i, pt: (0, 0))]
                 + [page_spec(p) for p in range(n_pages)] * 2,
        out_specs=col_spec,
        scratch_shapes=[pltpu.VMEM((ATT_HEADS, ATT_HEAD, PAGE_SIZE), F32)],
    )
    rows_last = lambda c: c.transpose(0, 1, 3, 4, 2)
    out = pl.pallas_call(
        functools.partial(_moba_step_kernel, n_pages=n_pages),
        grid_spec=grid_spec,
        out_shape=jax.ShapeDtypeStruct((n, ATT_HEADS, ATT_HEAD, 1), F32),
        compiler_params=_params(1),
        name="moba_step",
    )(page_table.reshape(-1), col(q), col(k_new), col(v_new), relt, bkt,
      *([rows_last(cache_k)] * n_pages), *([rows_last(cache_v)] * n_pages))
    return out.reshape(n, BRANCH_W)


def _merge_kernel(x_ref, ya_ref, yr_ref, bon_ref, g_ref, yc_ref, gate_ref,
                  lnw_ref, lnb_ref, ones_ref, wb_ref, wo_ref, o_ref, *, yc_transposed):
    ones_bd = ones_ref[...]
    y = yr_ref[...]
    mu = _head_sum(y, ones_bd) * (1.0 / RWKV_HEAD)
    d = y - mu
    var = _head_sum(d * d, ones_bd) * (1.0 / RWKV_HEAD)
    yb = (d * lax.rsqrt(var + RWKV_GN_EPS) * lnw_ref[...] + lnb_ref[...] + bon_ref[...]) * g_ref[...]
    yc = yc_ref[0].T if yc_transposed else yc_ref[...]
    merged = None
    for j, yj in enumerate((ya_ref[...], yb, yc)):
        gate = jax.nn.sigmoid(gate_ref[:, j * D_MODEL:(j + 1) * D_MODEL])
        term = _dot(yj.astype(BF16), wb_ref[j]) * gate
        merged = term if merged is None else merged + term
    o_ref[...] = x_ref[...] + _dot(merged.astype(BF16), wo_ref[...])


def _merge(x, ya, yr, bonus, g, yc, gates, mw, tm):
    m = x.shape[0]
    tok = lambda w: pl.BlockSpec((tm, w), lambda i: (i, 0))
    vec = _const((1, BRANCH_W))
    yc_transposed = yc.ndim == 3
    if yc_transposed:
        nblk = yc.shape[2] // tm
        yc_spec = pl.BlockSpec((1, BRANCH_W, tm), lambda i: (i // nblk, 0, i % nblk))
    else:
        yc_spec = tok(BRANCH_W)
    return pl.pallas_call(
        functools.partial(_merge_kernel, yc_transposed=yc_transposed),
        grid=(m // tm,),
        in_specs=[tok(D_MODEL)] + [tok(BRANCH_W)] * 4 + [yc_spec, tok(N_BRANCH * D_MODEL), vec, vec,
                  _const((BRANCH_W, BRANCH_W)), _const((N_BRANCH, BRANCH_W, D_MODEL)), _const((D_MODEL, D_MODEL))],
        out_specs=tok(D_MODEL),
        out_shape=jax.ShapeDtypeStruct((m, D_MODEL), F32),
        compiler_params=_params(1),
        name="merge",
    )(x, ya, yr, bonus, g, yc, gates, mw["ln_w"], mw["ln_b"], mw["ones_bd"], mw["w_branch"], mw["w_out"])


FFN_CHUNK = 512


def _ffn_kernel(*refs, seq_mode, final_norm, tm, tiles_per_seq):
    (x_ref, p_ref, ln2_ref, wup_ref, fcw_ref, fcb_ref, wdn_ref, ln3_ref, pg_ref, pp_ref, lnf_ref) = refs[:11]
    if seq_mode:
        o_ref, fc_ref, ext_ref = refs[11:14]
        i = pl.program_id(0)

        @pl.when(i % tiles_per_seq == 0)
        def _():
            ext_ref[0:SUBLANES, :] = jnp.zeros((SUBLANES, 2 * D_FF), F32)

        @pl.when(i % tiles_per_seq != 0)
        def _():
            ext_ref[0:SUBLANES, :] = ext_ref[tm:tm + SUBLANES, :]
    else:
        prev_ref, o_ref, u_ref = refs[11:14]

    x = x_ref[...]
    hb = _rms(x, ln2_ref[...]).astype(BF16)

    def conv_cols(lo, hi):
        u = _dot(hb, wup_ref[:, lo:hi])
        if seq_mode:
            ext_ref[SUBLANES:, lo:hi] = u
            u2 = ext_ref[pl.ds(SUBLANES - 2, tm), lo:hi]
            u1 = ext_ref[pl.ds(SUBLANES - 1, tm), lo:hi]
        else:
            u_ref[:, lo:hi] = u
            u2 = prev_ref[:, lo:hi]
            u1 = prev_ref[:, 2 * D_FF + lo:2 * D_FF + hi]
        return u2 * fcw_ref[0:1, lo:hi] + u1 * fcw_ref[1:2, lo:hi] + u * fcw_ref[2:3, lo:hi] + fcb_ref[:, lo:hi]

    acc = None
    for c in range(D_FF // FFN_CHUNK):
        lo, hi = c * FFN_CHUNK, (c + 1) * FFN_CHUNK
        act = _gelu_tanh(conv_cols(lo, hi)) * conv_cols(D_FF + lo, D_FF + hi)
        term = _dot(act.astype(BF16), wdn_ref[lo:hi, :])
        acc = term if acc is None else acc + term
    x = x + acc
    x = x + jax.nn.sigmoid(_dot(_rms(x, ln3_ref[...]).astype(BF16), pg_ref[...])) * _dot(p_ref[...].astype(BF16), pp_ref[...])
    o_ref[...] = _rms(x, lnf_ref[...]) if final_norm else x
    if seq_mode:
        fc_ref[0] = ext_ref[tm + SUBLANES - 2:tm + SUBLANES, :]


def _ffn(x, p, fw, tm, seq_len, prev=None, final_norm=False):
    m = x.shape[0]
    seq_mode = seq_len is not None
    tok = lambda w: pl.BlockSpec((tm, w), lambda i: (i, 0))
    vecd = _const((1, D_MODEL))
    in_specs = [tok(D_MODEL), tok(PLE_DIM), vecd, _const((D_MODEL, 2 * D_FF)), _const((FFN_CONV, 2 * D_FF)),
                _const((1, 2 * D_FF)), _const((D_FF, D_MODEL)), vecd, _const((D_MODEL, D_MODEL)),
                _const((PLE_DIM, D_MODEL)), vecd]
    args = [x, p, fw["ln2"], fw["ffn_up"], fw["conv_w"], fw["conv_b"], fw["ffn_down"], fw["ln3"],
            fw["ple_gate"], fw["ple_proj"], fw["ln_f"]]
    if seq_mode:
        tiles_per_seq = seq_len // tm
        out_specs = [tok(D_MODEL), pl.BlockSpec((1, FFN_CONV - 1, 2 * D_FF), lambda i: (i // tiles_per_seq, 0, 0))]
        out_shape = [jax.ShapeDtypeStruct((m, D_MODEL), F32),
                     jax.ShapeDtypeStruct((m // seq_len, FFN_CONV - 1, 2 * D_FF), F32)]
        scratch = [pltpu.VMEM((tm + SUBLANES, 2 * D_FF), F32)]
    else:
        tiles_per_seq = 1
        in_specs.append(tok((FFN_CONV - 1) * 2 * D_FF))
        args.append(prev)
        out_specs = [tok(D_MODEL), tok(2 * D_FF)]
        out_shape = [jax.ShapeDtypeStruct((m, D_MODEL), F32), jax.ShapeDtypeStruct((m, 2 * D_FF), F32)]
        scratch = []
    return pl.pallas_call(
        functools.partial(_ffn_kernel, seq_mode=seq_mode, final_norm=final_norm, tm=tm, tiles_per_seq=tiles_per_seq),
        grid=(m // tm,),
        in_specs=in_specs,
        out_specs=out_specs,
        out_shape=out_shape,
        scratch_shapes=scratch,
        compiler_params=_params(1),
        name="ffn",
    )(*args)


def _block_diag(w):
    g, n, _ = w.shape
    eye = jnp.eye(g, dtype=w.dtype)
    return (eye[:, None, :, None] * w[:, :, None, :]).reshape(g * n, g * n)


def _layer_weights(i, wt):
    row = lambda v: v.reshape(1, -1)
    ones_bd = _block_diag(jnp.ones((RWKV_HEADS, RWKV_HEAD, RWKV_HEAD), BF16))
    zeros_lora = jnp.zeros((DECAY_LORA, BRANCH_W), BF16)
    lw = {"conv_w": wt["lru_conv_w"][i], "conv_b": row(wt["lru_conv_b"][i]),
          "w_r": _block_diag(wt["lru_w_r"][i]).astype(BF16), "b_r": row(wt["lru_b_r"][i]),
          "w_i": _block_diag(wt["lru_w_i"][i]).astype(BF16), "b_i": row(wt["lru_b_i"][i]),
          "lam": row(wt["lru_lambda"][i])}
    rw = {"mu": row(wt["rwkv_mu"][i]), "w0": row(wt["rwkv_w0"][i]),
          "w2p": jnp.concatenate([wt["rwkv_w2"][i].astype(BF16), zeros_lora], axis=0),
          "a0": row(wt["rwkv_a0"][i]),
          "a2p": jnp.concatenate([zeros_lora, wt["rwkv_a2"][i].astype(BF16)], axis=0),
          "g2": wt["rwkv_g2"][i].astype(BF16), "k_k": row(wt["rwkv_k_k"][i]), "k_a": row(wt["rwkv_k_a"][i]),
          "r_k": row(wt["rwkv_r_k"][i]), "ones_bd": ones_bd}
    mw = {"ln_w": row(wt["rwkv_ln_w"][i]), "ln_b": row(wt["rwkv_ln_b"][i]), "ones_bd": ones_bd,
          "w_branch": wt["w_branch"][i].astype(BF16), "w_out": wt["w_out"][i].astype(BF16)}
    fw = {"ln2": row(wt["ln2"][i]), "ffn_up": wt["ffn_up"][i].astype(BF16), "conv_w": wt["ffn_conv_w"][i],
          "conv_b": row(wt["ffn_conv_b"][i]), "ffn_down": wt["ffn_down"][i].astype(BF16),
          "ln3": row(wt["ln3"][i]), "ple_gate": wt["ple_gate"][i].astype(BF16),
          "ple_proj": wt["ple_proj"][i].astype(BF16), "ln_f": row(wt["ln_f"])}
    w_in = wt["w_in"][i].astype(BF16)
    seg = lambda j: w_in[:, _IN_SEGS[j][0]:_IN_SEGS[j][1]]
    wk_heads = seg(3).reshape(D_MODEL, ATT_HEADS, ATT_HEAD).transpose(1, 0, 2)
    pw = {"w_in": w_in, "w_qkv_t": jnp.stack([seg(2).T, seg(3).T, seg(4).T]),
          "wk_aug": jnp.pad(wk_heads, ((0, 0), (0, 0), (0, LANES - ATT_HEAD)))}
    return {"ln1": row(wt["ln1"][i]), "w_in": w_in, "proj": pw, "lru": lw, "rwkv": rw, "merge": mw, "ffn": fw}


def _prompt_layer(x, p, lp, rel_bias, final_norm):
    b, t, _ = x.shape
    m = b * t
    nh, hd = RWKV_HEADS, RWKV_HEAD
    lru_c, rw_c, gates, q_t, k_t, v_t, k_aug = _in_proj_seq(x.reshape(m, D_MODEL), lp["ln1"], lp["proj"], b, t)

    ya, conv_new, h_new = _lru_seq(lru_c.reshape(b, t, 2 * BRANCH_W), lp["lru"], tc=256)

    r_, w_, k_, v_, a_, b_, g_, bonus = _rwkv_pre_seq(rw_c.reshape(b, t, RWKV_COLS), lp["rwkv"], tc=256)
    grp, kr = SCAN_GROUPS, SCAN_KEY_ROWS
    assert b * nh * grp == LANES

    def key_packed(z):
        return z.reshape(b, t, nh, grp, kr).transpose(1, 4, 3, 0, 2).reshape(t, kr, LANES)

    def row_major(z):
        return z.reshape(b, t, nh, hd // grp, grp).transpose(1, 3, 4, 0, 2).reshape(t, hd // grp, LANES)

    y_l, s_l = _rwkv_scan(key_packed(r_), key_packed(w_), key_packed(k_), key_packed(a_), key_packed(b_),
                          row_major(v_), tc=32)
    yr = y_l.reshape(t, hd // grp, grp, b, nh).transpose(3, 0, 4, 1, 2).reshape(m, BRANCH_W)
    s_l = s_l.reshape(hd // grp, grp, kr, grp, b, nh)
    s_l = jnp.stack([jnp.roll(s_l[:, :, :, c], c, axis=1) for c in range(grp)], axis=3)
    s_new = s_l.transpose(4, 5, 0, 3, 1, 2).reshape(b, nh, hd, hd)

    yc_t = _moba_seq(q_t, k_aug, v_t, rel_bias)

    flat = lambda z: z.reshape(m, BRANCH_W)
    x1 = _merge(x.reshape(m, D_MODEL), flat(ya), yr, flat(bonus), flat(g_), yc_t, gates, lp["merge"], tm=MOBA_BLOCK)
    x2, fc_new = _ffn(x1, p.reshape(m, PLE_DIM), lp["ffn"], tm=512, seq_len=t, final_norm=final_norm)
    tokens_major = lambda z: z.reshape(b, ATT_HEADS, ATT_HEAD, t).transpose(0, 3, 1, 2)
    new = {"k": tokens_major(k_t), "v": tokens_major(v_t),
           "lru_h": h_new[:, 0, :], "lru_conv": conv_new, "rwkv": s_new,
           "rwkv_shift": rw_c.reshape(b, t, RWKV_COLS)[:, -1, :], "ffn_conv": fc_new}
    return x2.reshape(b, t, D_MODEL), new


def _sample_layer(x, p, st, lp, rel_bias, final_norm):
    n = x.shape[0]
    lru_c, rw_c, q, k, v, gates = _in_proj(x.reshape(n, D_MODEL), lp["ln1"], lp["w_in"], tm=n)

    ya, conv_new, h_new = _lru_step(lru_c, st["lru_conv"].reshape(n, (LRU_CONV - 1) * BRANCH_W), st["lru_h"], lp["lru"])

    r_, w_, k_, v_, a_, b_, g_, bonus = _rwkv_pre_step(rw_c, st["rwkv_shift"], lp["rwkv"])
    seq_last = lambda z: z.reshape(n, RWKV_HEADS, RWKV_HEAD).transpose(1, 2, 0)
    s_l, y_l = _rwkv_step(st["rwkv"].transpose(0, 2, 3, 4, 1), st["layer"], seq_last(r_), seq_last(w_), seq_last(k_),
                          seq_last(a_), seq_last(b_), seq_last(v_))
    s_new = s_l.transpose(3, 0, 1, 2)
    yr = y_l.transpose(2, 0, 1).reshape(n, BRANCH_W)

    yc = _moba_step(q, k, v, st["cache_k"], st["cache_v"], st["layer"], st["page_table"], rel_bias)

    x1 = _merge(x.reshape(n, D_MODEL), ya, yr, bonus, g_, yc, gates, lp["merge"], tm=n)
    fc_prev = st["ffn_conv"].reshape(n, (FFN_CONV - 1) * 2 * D_FF)
    x2, u = _ffn(x1, p.reshape(n, PLE_DIM), lp["ffn"], tm=n, seq_len=None, prev=fc_prev, final_norm=final_norm)
    new = {"k": k.reshape(n, 1, ATT_HEADS, ATT_HEAD), "v": v.reshape(n, 1, ATT_HEADS, ATT_HEAD),
           "lru_h": h_new, "lru_conv": conv_new.reshape(n, LRU_CONV - 1, BRANCH_W), "rwkv": s_new,
           "rwkv_shift": rw_c, "ffn_conv": jnp.stack([st["ffn_conv"][:, 1, :], u], axis=1)}
    return x2.reshape(n, 1, D_MODEL), new


def kernel(x_prompt, x_sample, cache_k, cache_v, state_lru_h, state_lru_conv, state_rwkv, state_rwkv_shift, state_ffn_conv, page_table, p_prompt, p_sample, ln1, w_in, lru_conv_w, lru_conv_b, lru_w_r, lru_b_r, lru_w_i, lru_b_i, lru_lambda, rwkv_mu, rwkv_w0, rwkv_w2, rwkv_a0, rwkv_a2, rwkv_g2, rwkv_k_k, rwkv_k_a, rwkv_r_k, rwkv_ln_w, rwkv_ln_b, rel_bias, w_branch, w_out, ln2, ffn_up, ffn_conv_w, ffn_conv_b, ffn_down, ln3, ple_gate, ple_proj, ln_f):
    wt = dict(ln1=ln1, w_in=w_in, lru_conv_w=lru_conv_w, lru_conv_b=lru_conv_b, lru_w_r=lru_w_r, lru_b_r=lru_b_r,
              lru_w_i=lru_w_i, lru_b_i=lru_b_i, lru_lambda=lru_lambda, rwkv_mu=rwkv_mu, rwkv_w0=rwkv_w0,
              rwkv_w2=rwkv_w2, rwkv_a0=rwkv_a0, rwkv_a2=rwkv_a2, rwkv_g2=rwkv_g2, rwkv_k_k=rwkv_k_k,
              rwkv_k_a=rwkv_k_a, rwkv_r_k=rwkv_r_k, rwkv_ln_w=rwkv_ln_w, rwkv_ln_b=rwkv_ln_b, w_branch=w_branch,
              w_out=w_out, ln2=ln2, ffn_up=ffn_up, ffn_conv_w=ffn_conv_w, ffn_conv_b=ffn_conv_b, ffn_down=ffn_down,
              ln3=ln3, ple_gate=ple_gate, ple_proj=ple_proj, ln_f=ln_f)
    depth = w_in.shape[0]
    xp, xs = x_prompt, x_sample[:, 0, :]
    outs_p, outs_s = [], []
    for i in range(depth):
        lp = _layer_weights(i, wt)
        last = i == depth - 1
        st = {"cache_k": cache_k, "cache_v": cache_v, "layer": i, "page_table": page_table,
              "lru_h": state_lru_h[i], "lru_conv": state_lru_conv[i], "rwkv": state_rwkv,
              "rwkv_shift": state_rwkv_shift[i], "ffn_conv": state_ffn_conv[i]}
        xp, new_p = _prompt_layer(xp, p_prompt[i], lp, rel_bias, last)
        xs3, new_s = _sample_layer(xs, p_sample[i], st, lp, rel_bias, last)
        xs = xs3[:, 0, :]
        outs_p.append(new_p)
        outs_s.append(new_s)
    stack = lambda outs, name: jnp.stack([o[name] for o in outs])
    res = [xp, xs[:, None, :]]
    for name in ("k", "v", "lru_h", "lru_conv", "rwkv", "rwkv_shift", "ffn_conv"):
        res.append(stack(outs_p, name))
        res.append(stack(outs_s, name))
    return tuple(res)
```

```python
import functools
import math

import numpy as np
import jax
import jax.numpy as jnp
from jax import lax
from jax.experimental import pallas as pl
from jax.experimental.pallas import tpu as pltpu

F32 = jnp.float32
BF16 = jnp.bfloat16
I32 = jnp.int32

D_MODEL = 1024
DEPTH = 2
PAGE_SIZE = 128
N_BRANCH = 3
BRANCH_W = D_MODEL // 2
LRU_BLOCKS = 8
LRU_CONV = 4
LRU_C = 8.0
RWKV_HEAD = 64
RWKV_HEADS = BRANCH_W // RWKV_HEAD
DECAY_LORA = 64
AAA_LORA = 64
GATE_LORA = 128
RWKV_COLS = 3 * BRANCH_W + DECAY_LORA + AAA_LORA + GATE_LORA
RWKV_GN_EPS = 64e-5
ATT_HEAD = 64
ATT_HEADS = BRANCH_W // ATT_HEAD
MOBA_BLOCK = 256
MOBA_TOPK = 3
N_BUCKETS = 32
MAX_DISTANCE = 128
D_FF = 3 * D_MODEL
FFN_CONV = 3
PLE_DIM = 256
RMS_EPS = 1e-6
N_IN = 2 * BRANCH_W + RWKV_COLS + 3 * BRANCH_W + N_BRANCH * D_MODEL

_SEG_EDGES = (0, 2 * BRANCH_W, 2 * BRANCH_W + RWKV_COLS, 2 * BRANCH_W + RWKV_COLS + BRANCH_W,
              2 * BRANCH_W + RWKV_COLS + 2 * BRANCH_W, 2 * BRANCH_W + RWKV_COLS + 3 * BRANCH_W, N_IN)
_IN_SEGS = tuple(zip(_SEG_EDGES[:-1], _SEG_EDGES[1:]))

LANES = 128
SUBLANES = 8
VMEM_LIMIT = 56 * 1024 * 1024
NEG = -1e30
LORA_W = DECAY_LORA + AAA_LORA


def _params(n_axes):
    return pltpu.CompilerParams(dimension_semantics=("arbitrary",) * n_axes, vmem_limit_bytes=VMEM_LIMIT)


def _const(shape):
    return pl.BlockSpec(shape, lambda *_: (0,) * len(shape), pipeline_mode=pl.Buffered(1))


def _softplus(x):
    return jnp.maximum(x, 0.0) + jnp.log1p(jnp.exp(-jnp.abs(x)))


def _gelu_tanh(x):
    return 0.5 * x * (1.0 + jnp.tanh(math.sqrt(2.0 / math.pi) * (x + 0.044715 * (x * x * x))))


def _rms(x, g):
    return x * lax.rsqrt(jnp.mean(x * x, axis=-1, keepdims=True) + RMS_EPS) * g


def _dot(a, b):
    return jnp.dot(a, b, preferred_element_type=F32)


def _dot_nt(a, b, precision=None):
    return lax.dot_general(a, b, (((1,), (1,)), ((), ())), precision=precision, preferred_element_type=F32)


def _head_sum(x, ones_bd):
    hi = x.astype(BF16)
    lo = (x - hi.astype(F32)).astype(BF16)
    return _dot(hi, ones_bd) + _dot(lo, ones_bd)


def _in_proj_kernel(x_ref, g_ref, w_ref, *out_refs):
    xn = _rms(x_ref[...], g_ref[...]).astype(BF16)
    for ref, (lo, hi) in zip(out_refs, _IN_SEGS):
        ref[...] = _dot(xn, w_ref[:, lo:hi])


def _in_proj(x, g, w_bf16, tm):
    m = x.shape[0]
    widths = [hi - lo for lo, hi in _IN_SEGS]
    return pl.pallas_call(
        _in_proj_kernel,
        grid=(m // tm,),
        in_specs=[pl.BlockSpec((tm, D_MODEL), lambda i: (i, 0)), _const((1, D_MODEL)), _const((D_MODEL, N_IN))],
        out_specs=[pl.BlockSpec((tm, w), lambda i: (i, 0)) for w in widths],
        out_shape=[jax.ShapeDtypeStruct((m, w), F32) for w in widths],
        compiler_params=_params(1),
        name="in_proj",
    )(x, g, w_bf16)


_SEQ_SEGS = (0, 1, 5)


def _in_proj_seq_kernel(x_ref, g_ref, w_ref, wt_ref, wka_ref, *out_refs, nblk):
    xn = _rms(x_ref[...], g_ref[...]).astype(BF16)
    for ref, s in zip(out_refs[:3], _SEQ_SEGS):
        lo, hi = _IN_SEGS[s]
        ref[...] = _dot(xn, w_ref[:, lo:hi])
    for j, ref in enumerate(out_refs[3:6]):
        ref[0] = _dot_nt(wt_ref[j], xn)
    ka_ref = out_refs[6]
    n = pl.program_id(0) % nblk
    onehot = (lax.broadcasted_iota(I32, (1, LANES), 1) == ATT_HEAD + n).astype(F32)
    for h in range(ATT_HEADS):
        ka_ref[0, h] = (_dot(xn, wka_ref[h]) + onehot).astype(BF16)


def _in_proj_seq(x, g, pw, b, t):
    m = x.shape[0]
    tm = MOBA_BLOCK
    nblk = t // tm
    widths = [_IN_SEGS[s][1] - _IN_SEGS[s][0] for s in _SEQ_SEGS]
    blk_t = pl.BlockSpec((1, BRANCH_W, tm), lambda i: (i // nblk, 0, i % nblk))
    return pl.pallas_call(
        functools.partial(_in_proj_seq_kernel, nblk=nblk),
        grid=(m // tm,),
        in_specs=[pl.BlockSpec((tm, D_MODEL), lambda i: (i, 0)), _const((1, D_MODEL)), _const((D_MODEL, N_IN)),
                  _const((3, BRANCH_W, D_MODEL)), _const((ATT_HEADS, D_MODEL, LANES))],
        out_specs=[pl.BlockSpec((tm, w), lambda i: (i, 0)) for w in widths]
                  + [blk_t] * 3 + [pl.BlockSpec((1, ATT_HEADS, tm, LANES), lambda i: (i // nblk, 0, i % nblk, 0))],
        out_shape=[jax.ShapeDtypeStruct((m, w), F32) for w in widths]
                  + [jax.ShapeDtypeStruct((b, BRANCH_W, t), F32)] * 3
                  + [jax.ShapeDtypeStruct((b, ATT_HEADS, t, LANES), BF16)],
        compiler_params=_params(1),
        name="in_proj_seq",
    )(x, g, pw["w_in"], pw["w_qkv_t"], pw["wk_aug"])


def _lru_gates(xc, ga_unused, wr, br, wi, bi, lam):
    xcb = xc.astype(BF16)
    r = jax.nn.sigmoid(_dot(xcb, wr) + br)
    ig = jax.nn.sigmoid(_dot(xcb, wi) + bi)
    log_a = (-LRU_C * r) * _softplus(-lam)
    a = jnp.exp(log_a)
    bx = jnp.sqrt(1.0 - jnp.exp(2.0 * log_a)) * (ig * xc)
    return a, bx


def _lru_seq_kernel(cols_ref, cw_ref, cb_ref, wr_ref, br_ref, wi_ref, bi_ref, lam_ref,
                    y_ref, conv_ref, h_ref, ext_ref, hc_ref, *, tc):
    t = pl.program_id(1)

    @pl.when(t == 0)
    def _():
        ext_ref[0:SUBLANES, :] = jnp.zeros((SUBLANES, BRANCH_W), F32)
        hc_ref[...] = jnp.zeros_like(hc_ref)

    @pl.when(t > 0)
    def _():
        ext_ref[0:SUBLANES, :] = ext_ref[tc:tc + SUBLANES, :]

    xa = cols_ref[0, :, 0:BRANCH_W]
    ga = cols_ref[0, :, BRANCH_W:]
    ext_ref[SUBLANES:, :] = xa
    xc = ext_ref[pl.ds(SUBLANES - 3, tc), :] * cw_ref[0:1, :]
    xc = xc + ext_ref[pl.ds(SUBLANES - 2, tc), :] * cw_ref[1:2, :]
    xc = xc + ext_ref[pl.ds(SUBLANES - 1, tc), :] * cw_ref[2:3, :]
    xc = xc + xa * cw_ref[3:4, :] + cb_ref[...]
    a, bx = _lru_gates(xc, ga, wr_ref[...], br_ref[...], wi_ref[...], bi_ref[...], lam_ref[...])
    row = lax.broadcasted_iota(I32, (tc, BRANCH_W), 0)
    d = 1
    while d < tc:
        keep = row >= d
        a_s = jnp.where(keep, pltpu.roll(a, d, 0), 1.0)
        b_s = jnp.where(keep, pltpu.roll(bx, d, 0), 0.0)
        bx = a * b_s + bx
        a = a * a_s
        d *= 2
    h = a * hc_ref[...] + bx
    hc_ref[...] = h[tc - 1:tc, :]
    y_ref[0] = h * _gelu_tanh(ga)
    conv_ref[0] = ext_ref[tc + SUBLANES - 3:tc + SUBLANES, :]
    h_ref[0] = h[tc - 1:tc, :]


def _lru_seq(cols, lw, tc):
    b, t, _ = cols.shape
    vec = _const((1, BRANCH_W))
    mat = _const((BRANCH_W, BRANCH_W))
    return pl.pallas_call(
        functools.partial(_lru_seq_kernel, tc=tc),
        grid=(b, t // tc),
        in_specs=[pl.BlockSpec((1, tc, 2 * BRANCH_W), lambda i, j: (i, j, 0)),
                  _const((LRU_CONV, BRANCH_W)), vec, mat, vec, mat, vec, vec],
        out_specs=[pl.BlockSpec((1, tc, BRANCH_W), lambda i, j: (i, j, 0)),
                   pl.BlockSpec((1, LRU_CONV - 1, BRANCH_W), lambda i, j: (i, 0, 0)),
                   pl.BlockSpec((1, 1, BRANCH_W), lambda i, j: (i, 0, 0))],
        out_shape=[jax.ShapeDtypeStruct((b, t, BRANCH_W), F32),
                   jax.ShapeDtypeStruct((b, LRU_CONV - 1, BRANCH_W), F32),
                   jax.ShapeDtypeStruct((b, 1, BRANCH_W), F32)],
        scratch_shapes=[pltpu.VMEM((tc + SUBLANES, BRANCH_W), F32), pltpu.VMEM((1, BRANCH_W), F32)],
        compiler_params=_params(2),
        name="lru_seq",
    )(cols, lw["conv_w"], lw["conv_b"], lw["w_r"], lw["b_r"], lw["w_i"], lw["b_i"], lw["lam"])


def _lru_step_kernel(cols_ref, buf_ref, h0_ref, cw_ref, cb_ref, wr_ref, br_ref, wi_ref, bi_ref, lam_ref,
                     y_ref, conv_ref, h_ref):
    xa = cols_ref[:, 0:BRANCH_W]
    ga = cols_ref[:, BRANCH_W:]
    xc = buf_ref[:, 0:BRANCH_W] * cw_ref[0:1, :]
    xc = xc + buf_ref[:, BRANCH_W:2 * BRANCH_W] * cw_ref[1:2, :]
    xc = xc + buf_ref[:, 2 * BRANCH_W:] * cw_ref[2:3, :]
    xc = xc + xa * cw_ref[3:4, :] + cb_ref[...]
    a, bx = _lru_gates(xc, ga, wr_ref[...], br_ref[...], wi_ref[...], bi_ref[...], lam_ref[...])
    h = a * h0_ref[...] + bx
    y_ref[...] = h * _gelu_tanh(ga)
    conv_ref[:, 0:2 * BRANCH_W] = buf_ref[:, BRANCH_W:]
    conv_ref[:, 2 * BRANCH_W:] = xa
    h_ref[...] = h


def _lru_step(cols, buf, h0, lw):
    n = cols.shape[0]
    full = lambda w: pl.BlockSpec((n, w), lambda i: (0, 0))
    vec = _const((1, BRANCH_W))
    mat = _const((BRANCH_W, BRANCH_W))
    return pl.pallas_call(
        _lru_step_kernel,
        grid=(1,),
        in_specs=[full(2 * BRANCH_W), full(3 * BRANCH_W), full(BRANCH_W),
                  _const((LRU_CONV, BRANCH_W)), vec, mat, vec, mat, vec, vec],
        out_specs=[full(BRANCH_W), full(3 * BRANCH_W), full(BRANCH_W)],
        out_shape=[jax.ShapeDtypeStruct((n, BRANCH_W), F32), jax.ShapeDtypeStruct((n, 3 * BRANCH_W), F32),
                   jax.ShapeDtypeStruct((n, BRANCH_W), F32)],
        compiler_params=_params(1),
        name="lru_step",
    )(cols, buf, h0, lw["conv_w"], lw["conv_b"], lw["w_r"], lw["b_r"], lw["w_i"], lw["b_i"], lw["lam"])


def _rwkv_token_math(cols, prev, mu, w0, w2p, a0, a2p, g2, k_k, k_a, r_k, ones_bd):
    mixed = cols + (prev - cols) * mu
    r = mixed[:, 0:BRANCH_W]
    k = mixed[:, BRANCH_W:2 * BRANCH_W]
    v = mixed[:, 2 * BRANCH_W:3 * BRANCH_W]
    la = mixed[:, 3 * BRANCH_W:3 * BRANCH_W + LORA_W]
    gl = mixed[:, 3 * BRANCH_W + LORA_W:]
    w = -_softplus(-(w0 + _dot(jnp.tanh(la).astype(BF16), w2p))) - 0.5
    decay = jnp.exp(-jnp.exp(w))
    a = jax.nn.sigmoid(a0 + _dot(la.astype(BF16), a2p))
    g = _dot(jax.nn.sigmoid(gl).astype(BF16), g2)
    kk = k * k_k
    kk = kk / jnp.maximum(jnp.sqrt(_head_sum(kk * kk, ones_bd)), 1e-12)
    k = k * (1.0 + (a - 1.0) * k_a)
    bonus = _head_sum(r * k * r_k, ones_bd) * v
    return r, decay, k, v, -kk, kk * a, g, bonus


_RW_PARAM_ORDER = ("mu", "w0", "w2p", "a0", "a2p", "g2", "k_k", "k_a", "r_k", "ones_bd")


def _rw_param_specs():
    vec = _const((1, BRANCH_W))
    return [_const((1, RWKV_COLS)), vec, _const((LORA_W, BRANCH_W)), vec, _const((LORA_W, BRANCH_W)),
            _const((GATE_LORA, BRANCH_W)), vec, vec, vec, _const((BRANCH_W, BRANCH_W))]


def _rwkv_pre_seq_kernel(cols_ref, *refs, tc):
    prm = [r[...] for r in refs[:10]]
    outs = refs[10:18]
    ext_ref = refs[18]
    t = pl.program_id(1)

    @pl.when(t == 0)
    def _():
        ext_ref[0:SUBLANES, :] = jnp.zeros((SUBLANES, RWKV_COLS), F32)

    @pl.when(t > 0)
    def _():
        ext_ref[0:SUBLANES, :] = ext_ref[tc:tc + SUBLANES, :]

    cols = cols_ref[0]
    ext_ref[SUBLANES:, :] = cols
    prev = ext_ref[pl.ds(SUBLANES - 1, tc), :]
    for ref, val in zip(outs, _rwkv_token_math(cols, prev, *prm)):
        ref[0] = val


def _rwkv_pre_seq(cols, rw, tc):
    b, t, _ = cols.shape
    blk = pl.BlockSpec((1, tc, BRANCH_W), lambda i, j: (i, j, 0))
    return pl.pallas_call(
        functools.partial(_rwkv_pre_seq_kernel, tc=tc),
        grid=(b, t // tc),
        in_specs=[pl.BlockSpec((1, tc, RWKV_COLS), lambda i, j: (i, j, 0))] + _rw_param_specs(),
        out_specs=[blk] * 8,
        out_shape=[jax.ShapeDtypeStruct((b, t, BRANCH_W), F32)] * 8,
        scratch_shapes=[pltpu.VMEM((tc + SUBLANES, RWKV_COLS), F32)],
        compiler_params=_params(2),
        name="rwkv_pre_seq",
    )(cols, *[rw[k] for k in _RW_PARAM_ORDER])


def _rwkv_pre_step_kernel(cols_ref, prev_ref, *refs):
    prm = [r[...] for r in refs[:10]]
    for ref, val in zip(refs[10:18], _rwkv_token_math(cols_ref[...], prev_ref[...], *prm)):
        ref[...] = val


def _rwkv_pre_step(cols, prev, rw):
    n = cols.shape[0]
    full = lambda w: pl.BlockSpec((n, w), lambda i: (0, 0))
    return pl.pallas_call(
        _rwkv_pre_step_kernel,
        grid=(1,),
        in_specs=[full(RWKV_COLS), full(RWKV_COLS)] + _rw_param_specs(),
        out_specs=[full(BRANCH_W)] * 8,
        out_shape=[jax.ShapeDtypeStruct((n, BRANCH_W), F32)] * 8,
        compiler_params=_params(1),
        name="rwkv_pre_step",
    )(cols, prev, *[rw[k] for k in _RW_PARAM_ORDER])


SCAN_GROUPS = 4
SCAN_GROUP_LANES = LANES // SCAN_GROUPS
SCAN_KEY_ROWS = RWKV_HEAD // SCAN_GROUPS


def _rwkv_scan_kernel(r_ref, w_ref, k_ref, a_ref, b_ref, v_ref, y_ref, so_ref, s_ref, exp_ref, *, tc, ni):
    t = pl.program_id(0)

    @pl.when(t == 0)
    def _():
        s_ref[...] = jnp.zeros_like(s_ref)

    def unpack(slot, tt):
        for kk, ref in enumerate((a_ref, b_ref, w_ref, k_ref, r_ref)):
            x = ref[tt]
            for g in range(SCAN_GROUPS):
                xg = x if g == 0 else pltpu.roll(x, LANES - SCAN_GROUP_LANES * g, 1)
                exp_ref[kk, slot, g * SCAN_KEY_ROWS:(g + 1) * SCAN_KEY_ROWS, :] = xg

    unpack(0, 0)

    n_acc = 4

    def fold(accs):
        return (accs[0] + accs[1]) + (accs[2] + accs[3])

    def step(tt, slot):
        unpack(1 - slot, jnp.minimum(tt + 1, tc - 1))
        row = lambda kk, j: exp_ref[kk, slot, pl.ds(j, 1), :]
        groups = range(ni // SUBLANES)
        acc = [[None] * n_acc for _ in groups]
        for j in range(RWKV_HEAD):
            a = row(0, j)
            for g in groups:
                term = s_ref[g, j] * a
                acc[g][j % n_acc] = term if j < n_acc else acc[g][j % n_acc] + term
        sa = [fold(acc[g]) for g in groups]
        vg = [v_ref[tt, g * SUBLANES:(g + 1) * SUBLANES, :] for g in groups]
        acc = [[None] * n_acc for _ in groups]
        for j in range(RWKV_HEAD):
            b, w, k, r = row(1, j), row(2, j), row(3, j), row(4, j)
            for g in groups:
                s = s_ref[g, j] * w + sa[g] * b + vg[g] * k
                s_ref[g, j] = s
                term = s * r
                acc[g][j % n_acc] = term if j < n_acc else acc[g][j % n_acc] + term
        for g in groups:
            y_ref[tt, g * SUBLANES:(g + 1) * SUBLANES, :] = fold(acc[g])

    def two_steps(pair, carry):
        step(2 * pair, 0)
        step(2 * pair + 1, 1)
        return carry

    lax.fori_loop(0, tc // 2, two_steps, 0)

    @pl.when(t == pl.num_programs(0) - 1)
    def _():
        so_ref[...] = s_ref[...]


def _rwkv_scan(r, w, k, a, b, v, tc):
    t = r.shape[0]
    ni = v.shape[1]
    vec = pl.BlockSpec((tc, SCAN_KEY_ROWS, LANES), lambda j: (j, 0, 0))
    row = pl.BlockSpec((tc, ni, LANES), lambda j: (j, 0, 0))
    st = pl.BlockSpec((ni // SUBLANES, RWKV_HEAD, SUBLANES, LANES), lambda j: (0, 0, 0, 0))
    return pl.pallas_call(
        functools.partial(_rwkv_scan_kernel, tc=tc, ni=ni),
        grid=(t // tc,),
        in_specs=[vec] * 5 + [row],
        out_specs=[row, st],
        out_shape=[jax.ShapeDtypeStruct((t, ni, LANES), F32),
                   jax.ShapeDtypeStruct((ni // SUBLANES, RWKV_HEAD, SUBLANES, LANES), F32)],
        scratch_shapes=[pltpu.VMEM((ni // SUBLANES, RWKV_HEAD, SUBLANES, LANES), F32),
                        pltpu.VMEM((5, 2, RWKV_HEAD, LANES), F32)],
        compiler_params=_params(1),
        name="rwkv_scan",
    )(r, w, k, a, b, v)


def _rwkv_step_kernel(s_ref, r_ref, w_ref, k_ref, a_ref, b_ref, v_ref, so_ref, y_ref):
    a = a_ref[0]
    b = b_ref[0]
    w = w_ref[0]
    k = k_ref[0]
    r = r_ref[0]
    for i in range(RWKV_HEAD):
        s = s_ref[0, i]
        sa = jnp.sum(s * a, axis=0, keepdims=True)
        s = s * w + sa * b + v_ref[0, pl.ds(i, 1), :] * k
        so_ref[0, i] = s
        y_ref[0, pl.ds(i, 1), :] = jnp.sum(s * r, axis=0, keepdims=True)


def _rwkv_step(s, layer, r, w, k, a, b, v):
    _, nh, hd, _, n = s.shape
    st = pl.BlockSpec((1, hd, hd, n), lambda i: (i, 0, 0, 0))
    vec = pl.BlockSpec((1, hd, n), lambda i: (i, 0, 0))
    return pl.pallas_call(
        _rwkv_step_kernel,
        grid=(nh,),
        in_specs=[pl.BlockSpec((None, 1, hd, hd, n), lambda i: (layer, i, 0, 0, 0))] + [vec] * 6,
        out_specs=[st, vec],
        out_shape=[jax.ShapeDtypeStruct((nh, hd, hd, n), F32), jax.ShapeDtypeStruct((nh, hd, n), F32)],
        compiler_params=_params(1),
        name="rwkv_step",
    )(s, r, w, k, a, b, v)


def _t5_bucket_np(dist):
    n = np.maximum(dist, 0)
    max_exact = N_BUCKETS // 2
    nf = np.maximum(n, 1).astype(np.float32)
    large = max_exact + (np.log(nf / np.float32(max_exact)) / np.float32(math.log(MAX_DISTANCE / max_exact))
                         * np.float32(N_BUCKETS - max_exact)).astype(np.int32)
    large = np.minimum(large, N_BUCKETS - 1)
    return np.where(n < max_exact, n, large).astype(np.int32)


def _moba_seq_kernel(tab_ref, bko_ref, bkp_ref, qt_ref, ka_ref, vt_ref, o_ref,
                     km_ref, bias_ref, qa_ref, m_ref, l_ref, acc_ref, s_ref, *, nblk):
    bi = pl.program_id(0)
    i = pl.program_id(1)
    blk = MOBA_BLOCK
    far_bucket = N_BUCKETS - 1
    sel_rows = 2 * SUBLANES
    log2e = math.log2(math.e)
    scale = ATT_HEAD ** -0.5 * log2e
    ones_rows = jnp.ones((SUBLANES, blk), BF16)

    @pl.when((bi == 0) & (i == 0))
    def _():
        bko = bko_ref[...]
        bkp = bkp_ref[...]
        for h in range(ATT_HEADS):
            c = tab_ref[far_bucket, h]
            own = jnp.zeros((blk, blk), F32)
            prev = jnp.zeros((blk, blk), F32)
            for j in range(N_BUCKETS):
                val = (tab_ref[j, h] - c) * log2e
                own = jnp.where(bko == j, val, own)
                prev = jnp.where(bkp == j, val, prev)
            bias_ref[h, 0] = jnp.where(bko < 0, NEG, own)
            bias_ref[h, 1] = prev

    @pl.when(i == 0)
    def _():
        for h in range(ATT_HEADS):
            km_ref[h] = jnp.zeros((sel_rows, ATT_HEAD), F32)
            for n in range(nblk):
                ks = jnp.sum(ka_ref[0, h, n * blk:(n + 1) * blk, :].astype(F32), axis=0, keepdims=True)
                km_ref[h, n:n + 1, :] = ks[:, 0:ATT_HEAD] * (1.0 / blk)

    row = lax.broadcasted_iota(I32, (sel_rows, blk), 0)
    row_f = row.astype(F32)
    for h in range(ATT_HEADS):
        qt = qt_ref[0, h * ATT_HEAD:(h + 1) * ATT_HEAD, :]
        bs = jnp.dot(km_ref[h], qt, precision=lax.Precision.HIGHEST, preferred_element_type=F32)
        work = jnp.where(row < i, bs, -jnp.inf)
        sel = row == i
        for _ in range(MOBA_TOPK):
            mx = jnp.max(work, axis=0, keepdims=True)
            is_m = (work == mx) & (work > -jnp.inf)
            idx = jnp.min(jnp.where(is_m, row_f, 4.0 * LANES), axis=0, keepdims=True)
            pick = row_f == idx
            sel = sel | pick
            work = jnp.where(pick, -jnp.inf, work)
        selb = jnp.where(sel, 0.0, NEG)
        pad = jnp.zeros((LANES - ATT_HEAD - sel_rows, blk), F32)
        qa_ref[h] = jnp.concatenate([qt * scale, selb, pad], axis=0).astype(BF16)

    key_rows = lambda n: pl.ds(pl.multiple_of(n * blk, blk), blk)

    def block_step(n, kind):
        for h in range(ATT_HEADS):
            s_ref[h] = _dot(ka_ref[0, h, key_rows(n), :], qa_ref[h])
        for h in range(ATT_HEADS):
            s = s_ref[h]
            vt = vt_ref[0, h * ATT_HEAD:(h + 1) * ATT_HEAD, key_rows(n)].astype(BF16)
            if kind == "own":
                s = s + bias_ref[h, 0]
                m_new = jnp.max(s, axis=0, keepdims=True)
                pb = jnp.exp2(s - m_new).astype(BF16)
                l_ref[h] = _dot(ones_rows, pb)[0:1, :]
                acc_ref[h] = _dot(vt, pb)
            else:
                if kind == "prev":
                    s = s + bias_ref[h, 1] + jnp.where(i >= 1, 0.0, NEG)
                m_old = m_ref[h]
                m_new = jnp.maximum(m_old, jnp.max(s, axis=0, keepdims=True))
                alpha = jnp.exp2(m_old - m_new)
                pb = jnp.exp2(s - m_new).astype(BF16)
                l_ref[h] = alpha * l_ref[h] + _dot(ones_rows, pb)[0:1, :]
                acc_ref[h] = alpha * acc_ref[h] + _dot(vt, pb)
            m_ref[h] = m_new

    block_step(i, "own")
    block_step(jnp.maximum(i - 1, 0), "prev")

    def far(n, carry):
        block_step(n, "far")
        return carry

    lax.fori_loop(0, jnp.maximum(i - 1, 0), far, 0)
    for h in range(ATT_HEADS):
        o_ref[0, h * ATT_HEAD:(h + 1) * ATT_HEAD, :] = acc_ref[h] / l_ref[h]


def _moba_seq(q_t, k_aug, v_t, rel_bias):
    b, nh, t, _ = k_aug.shape
    blk = MOBA_BLOCK
    nblk = t // blk
    assert nblk <= 2 * SUBLANES
    d = np.arange(blk)[None, :] - np.arange(blk)[:, None]
    bko = jnp.asarray(np.where(d >= 0, _t5_bucket_np(d), -1).astype(np.int32))
    bkp = jnp.asarray(_t5_bucket_np(d + blk))
    q_blk = pl.BlockSpec((1, BRANCH_W, blk), lambda i, j: (i, 0, j))
    return pl.pallas_call(
        functools.partial(_moba_seq_kernel, nblk=nblk),
        grid=(b, nblk),
        in_specs=[pl.BlockSpec(memory_space=pltpu.SMEM), _const((blk, blk)), _const((blk, blk)), q_blk,
                  pl.BlockSpec((1, nh, t, LANES), lambda i, j: (i, 0, 0, 0), pipeline_mode=pl.Buffered(1)),
                  pl.BlockSpec((1, BRANCH_W, t), lambda i, j: (i, 0, 0), pipeline_mode=pl.Buffered(1))],
        out_specs=q_blk,
        out_shape=jax.ShapeDtypeStruct((b, BRANCH_W, t), F32),
        scratch_shapes=[pltpu.VMEM((nh, 2 * SUBLANES, ATT_HEAD), F32), pltpu.VMEM((nh, 2, blk, blk), F32),
                        pltpu.VMEM((nh, LANES, blk), BF16), pltpu.VMEM((nh, 1, blk), F32),
                        pltpu.VMEM((nh, 1, blk), F32), pltpu.VMEM((nh, ATT_HEAD, blk), F32),
                        pltpu.VMEM((nh, blk, blk), F32)],
        compiler_params=_params(2),
        name="moba_seq",
    )(rel_bias, bko, bkp, q_t, k_aug, v_t)


def _moba_step_kernel(pt_ref, q_ref, kn_ref, vn_ref, relt_ref, bkt_ref, *refs, n_pages):
    kp = refs[:n_pages]
    vp = refs[n_pages:2 * n_pages]
    o_ref, acc_ref = refs[2 * n_pages:2 * n_pages + 2]
    pages_per_block = MOBA_BLOCK // PAGE_SIZE
    n_blocks = n_pages // pages_per_block
    scale = ATT_HEAD ** -0.5
    heads = range(ATT_HEADS)

    def per_head(fn):
        return jnp.concatenate([fn(h) for h in heads], axis=0)

    relt = relt_ref[...]
    c_far = relt[:, N_BUCKETS - 1:N_BUCKETS]
    bkt = bkt_ref[...]
    bias_last = jnp.zeros((ATT_HEADS, PAGE_SIZE), F32)
    for j in range(N_BUCKETS):
        bias_last = jnp.where(bkt == j, relt[:, j:j + 1] - c_far, bias_last)

    q = q_ref[0]
    qb = [jnp.broadcast_to(q[h], (ATT_HEAD, PAGE_SIZE)) for h in heads]
    logit, rowsum = [], []
    for p in range(n_pages):
        raw = per_head(lambda h: jnp.sum(kp[p][0, 0, h] * qb[h], axis=0, keepdims=True))
        rowsum.append(jnp.sum(raw, axis=1, keepdims=True))
        lg = raw * scale
        logit.append(lg + bias_last if p == n_pages - 1 else lg)

    sc = []
    for n in range(n_blocks):
        tot = rowsum[n * pages_per_block]
        for j in range(1, pages_per_block):
            tot = tot + rowsum[n * pages_per_block + j]
        sc.append(tot * (1.0 / MOBA_BLOCK))
    sel = []
    for n in range(n_blocks):
        rank = jnp.zeros((ATT_HEADS, 1), I32)
        for j in range(n_blocks):
            if j != n:
                ahead = sc[j] > sc[n]
                if j < n:
                    ahead = ahead | (sc[j] == sc[n])
                rank = rank + ahead.astype(I32)
        sel.append(rank < MOBA_TOPK)

    s_self = per_head(lambda h: jnp.sum(q[h] * kn_ref[0, h], axis=0, keepdims=True)) * scale + (relt[:, 0:1] - c_far)
    m_all = s_self
    for p in range(n_pages):
        m_all = jnp.maximum(m_all, jnp.where(sel[p // pages_per_block], jnp.max(logit[p], axis=1, keepdims=True), -jnp.inf))
    w_self = jnp.exp(s_self - m_all)
    l_all = w_self
    acc_ref[...] = jnp.zeros_like(acc_ref)
    for p in range(n_pages):
        e = jnp.where(sel[p // pages_per_block], jnp.exp(logit[p] - m_all), 0.0)
        l_all = l_all + jnp.sum(e, axis=1, keepdims=True)
        for h in heads:
            acc_ref[h] += vp[p][0, 0, h] * e[h:h + 1, :]
    for h in heads:
        out = jnp.sum(acc_ref[h], axis=1, keepdims=True) + w_self[h:h + 1, :] * vn_ref[0, h]
        o_ref[0, h] = out / l_all[h:h + 1, :]


def _moba_step(q, k_new, v_new, cache_k, cache_v, layer, page_table, rel_bias):
    n, n_pages = page_table.shape
    col = lambda z: z.reshape(n, ATT_HEADS, ATT_HEAD, 1)
    col_spec = pl.BlockSpec((1, ATT_HEADS, ATT_HEAD, 1), lambda i, pt: (i, 0, 0, 0))
    relt = jnp.pad(rel_bias.T, ((0, 0), (0, LANES - N_BUCKETS)))
    bkt = jnp.asarray(_t5_bucket_np(PAGE_SIZE - np.arange(PAGE_SIZE))[None, :])

    def page_spec(p):
        return pl.BlockSpec((1, 1, ATT_HEADS, ATT_HEAD, PAGE_SIZE),
                            lambda i, pt: (layer, pt[i * n_pages + p], 0, 0, 0))

    grid_spec = pltpu.PrefetchScalarGridSpec(
        num_scalar_prefetch=1,
        grid=(n,),
        in_specs=[col_spec, col_spec, col_spec,
                  pl.BlockSpec((ATT_HEADS, LANES), lambda i, pt: (0, 0)),
                  pl.BlockSpec((1, PAGE_SIZE), lambda i, pt: (0, 0))]
                 + [page_spec(p) for p in range(n_pages)] * 2,
        out_specs=col_spec,
        scratch_shapes=[pltpu.VMEM((ATT_HEADS, ATT_HEAD, PAGE_SIZE), F32)],
    )
    rows_last = lambda c: c.transpose(0, 1, 3, 4, 2)
    out = pl.pallas_call(
        functools.partial(_moba_step_kernel, n_pages=n_pages),
        grid_spec=grid_spec,
        out_shape=jax.ShapeDtypeStruct((n, ATT_HEADS, ATT_HEAD, 1), F32),
        compiler_params=_params(1),
        name="moba_step",
    )(page_table.reshape(-1), col(q), col(k_new), col(v_new), relt, bkt,
      *([rows_last(cache_k)] * n_pages), *([rows_last(cache_v)] * n_pages))
    return out.reshape(n, BRANCH_W)


def _merge_kernel(x_ref, ya_ref, yr_ref, bon_ref, g_ref, yc_ref, gate_ref,
                  lnw_ref, lnb_ref, ones_ref, wb_ref, wo_ref, o_ref, *, yc_transposed):
    ones_bd = ones_ref[...]
    y = yr_ref[...]
    mu = _head_sum(y, ones_bd) * (1.0 / RWKV_HEAD)
    d = y - mu
    var = _head_sum(d * d, ones_bd) * (1.0 / RWKV_HEAD)
    yb = (d * lax.rsqrt(var + RWKV_GN_EPS) * lnw_ref[...] + lnb_ref[...] + bon_ref[...]) * g_ref[...]
    yc = yc_ref[0].T if yc_transposed else yc_ref[...]
    merged = None
    for j, yj in enumerate((ya_ref[...], yb, yc)):
        gate = jax.nn.sigmoid(gate_ref[:, j * D_MODEL:(j + 1) * D_MODEL])
        term = _dot(yj.astype(BF16), wb_ref[j]) * gate
        merged = term if merged is None else merged + term
    o_ref[...] = x_ref[...] + _dot(merged.astype(BF16), wo_ref[...])


def _merge(x, ya, yr, bonus, g, yc, gates, mw, tm):
    m = x.shape[0]
    tok = lambda w: pl.BlockSpec((tm, w), lambda i: (i, 0))
    vec = _const((1, BRANCH_W))
    yc_transposed = yc.ndim == 3
    if yc_transposed:
        nblk = yc.shape[2] // tm
        yc_spec = pl.BlockSpec((1, BRANCH_W, tm), lambda i: (i // nblk, 0, i % nblk))
    else:
        yc_spec = tok(BRANCH_W)
    return pl.pallas_call(
        functools.partial(_merge_kernel, yc_transposed=yc_transposed),
        grid=(m // tm,),
        in_specs=[tok(D_MODEL)] + [tok(BRANCH_W)] * 4 + [yc_spec, tok(N_BRANCH * D_MODEL), vec, vec,
                  _const((BRANCH_W, BRANCH_W)), _const((N_BRANCH, BRANCH_W, D_MODEL)), _const((D_MODEL, D_MODEL))],
        out_specs=tok(D_MODEL),
        out_shape=jax.ShapeDtypeStruct((m, D_MODEL), F32),
        compiler_params=_params(1),
        name="merge",
    )(x, ya, yr, bonus, g, yc, gates, mw["ln_w"], mw["ln_b"], mw["ones_bd"], mw["w_branch"], mw["w_out"])


FFN_CHUNK = 512


def _ffn_kernel(*refs, seq_mode, final_norm, tm, tiles_per_seq):
    (x_ref, p_ref, ln2_ref, wup_ref, fcw_ref, fcb_ref, wdn_ref, ln3_ref, pg_ref, pp_ref, lnf_ref) = refs[:11]
    if seq_mode:
        o_ref, fc_ref, ext_ref = refs[11:14]
        i = pl.program_id(0)

        @pl.when(i % tiles_per_seq == 0)
        def _():
            ext_ref[0:SUBLANES, :] = jnp.zeros((SUBLANES, 2 * D_FF), F32)

        @pl.when(i % tiles_per_seq != 0)
        def _():
            ext_ref[0:SUBLANES, :] = ext_ref[tm:tm + SUBLANES, :]
    else:
        prev_ref, o_ref, u_ref = refs[11:14]

    x = x_ref[...]
    hb = _rms(x, ln2_ref[...]).astype(BF16)

    def up_cols(lo, hi):
        u = _dot(hb, wup_ref[:, lo:hi])
        if seq_mode:
            ext_ref[SUBLANES:, lo:hi] = u
        else:
            u_ref[:, lo:hi] = u

    def conv_cols(lo, hi):
        if seq_mode:
            u = ext_ref[SUBLANES:, lo:hi]
            u2 = ext_ref[pl.ds(SUBLANES - 2, tm), lo:hi]
            u1 = ext_ref[pl.ds(SUBLANES - 1, tm), lo:hi]
        else:
            u = u_ref[:, lo:hi]
            u2 = prev_ref[:, lo:hi]
            u1 = prev_ref[:, 2 * D_FF + lo:2 * D_FF + hi]
        return u2 * fcw_ref[0:1, lo:hi] + u1 * fcw_ref[1:2, lo:hi] + u * fcw_ref[2:3, lo:hi] + fcb_ref[:, lo:hi]

    def up_chunk(c):
        up_cols(c * FFN_CHUNK, (c + 1) * FFN_CHUNK)
        up_cols(D_FF + c * FFN_CHUNK, D_FF + (c + 1) * FFN_CHUNK)

    n_chunks = D_FF // FFN_CHUNK
    up_chunk(0)
    acc = None
    for c in range(n_chunks):
        if c + 1 < n_chunks:
            up_chunk(c + 1)
        lo, hi = c * FFN_CHUNK, (c + 1) * FFN_CHUNK
        act = _gelu_tanh(conv_cols(lo, hi)) * conv_cols(D_FF + lo, D_FF + hi)
        term = _dot(act.astype(BF16), wdn_ref[lo:hi, :])
        acc = term if acc is None else acc + term
    x = x + acc
    x = x + jax.nn.sigmoid(_dot(_rms(x, ln3_ref[...]).astype(BF16), pg_ref[...])) * _dot(p_ref[...].astype(BF16), pp_ref[...])
    o_ref[...] = _rms(x, lnf_ref[...]) if final_norm else x
    if seq_mode:
        fc_ref[0] = ext_ref[tm + SUBLANES - 2:tm + SUBLANES, :]


def _ffn(x, p, fw, tm, seq_len, prev=None, final_norm=False):
    m = x.shape[0]
    seq_mode = seq_len is not None
    tok = lambda w: pl.BlockSpec((tm, w), lambda i: (i, 0))
    vecd = _const((1, D_MODEL))
    in_specs = [tok(D_MODEL), tok(PLE_DIM), vecd, _const((D_MODEL, 2 * D_FF)), _const((FFN_CONV, 2 * D_FF)),
                _const((1, 2 * D_FF)), _const((D_FF, D_MODEL)), vecd, _const((D_MODEL, D_MODEL)),
                _const((PLE_DIM, D_MODEL)), vecd]
    args = [x, p, fw["ln2"], fw["ffn_up"], fw["conv_w"], fw["conv_b"], fw["ffn_down"], fw["ln3"],
            fw["ple_gate"], fw["ple_proj"], fw["ln_f"]]
    if seq_mode:
        tiles_per_seq = seq_len // tm
        out_specs = [tok(D_MODEL), pl.BlockSpec((1, FFN_CONV - 1, 2 * D_FF), lambda i: (i // tiles_per_seq, 0, 0))]
        out_shape = [jax.ShapeDtypeStruct((m, D_MODEL), F32),
                     jax.ShapeDtypeStruct((m // seq_len, FFN_CONV - 1, 2 * D_FF), F32)]
        scratch = [pltpu.VMEM((tm + SUBLANES, 2 * D_FF), F32)]
    else:
        tiles_per_seq = 1
        in_specs.append(tok((FFN_CONV - 1) * 2 * D_FF))
        args.append(prev)
        out_specs = [tok(D_MODEL), tok(2 * D_FF)]
        out_shape = [jax.ShapeDtypeStruct((m, D_MODEL), F32), jax.ShapeDtypeStruct((m, 2 * D_FF), F32)]
        scratch = []
    return pl.pallas_call(
        functools.partial(_ffn_kernel, seq_mode=seq_mode, final_norm=final_norm, tm=tm, tiles_per_seq=tiles_per_seq),
        grid=(m // tm,),
        in_specs=in_specs,
        out_specs=out_specs,
        out_shape=out_shape,
        scratch_shapes=scratch,
        compiler_params=_params(1),
        name="ffn",
    )(*args)


def _block_diag(w):
    g, n, _ = w.shape
    eye = jnp.eye(g, dtype=w.dtype)
    return (eye[:, None, :, None] * w[:, :, None, :]).reshape(g * n, g * n)


def _layer_weights(i, wt):
    row = lambda v: v.reshape(1, -1)
    ones_bd = _block_diag(jnp.ones((RWKV_HEADS, RWKV_HEAD, RWKV_HEAD), BF16))
    zeros_lora = jnp.zeros((DECAY_LORA, BRANCH_W), BF16)
    lw = {"conv_w": wt["lru_conv_w"][i], "conv_b": row(wt["lru_conv_b"][i]),
          "w_r": _block_diag(wt["lru_w_r"][i]).astype(BF16), "b_r": row(wt["lru_b_r"][i]),
          "w_i": _block_diag(wt["lru_w_i"][i]).astype(BF16), "b_i": row(wt["lru_b_i"][i]),
          "lam": row(wt["lru_lambda"][i])}
    rw = {"mu": row(wt["rwkv_mu"][i]), "w0": row(wt["rwkv_w0"][i]),
          "w2p": jnp.concatenate([wt["rwkv_w2"][i].astype(BF16), zeros_lora], axis=0),
          "a0": row(wt["rwkv_a0"][i]),
          "a2p": jnp.concatenate([zeros_lora, wt["rwkv_a2"][i].astype(BF16)], axis=0),
          "g2": wt["rwkv_g2"][i].astype(BF16), "k_k": row(wt["rwkv_k_k"][i]), "k_a": row(wt["rwkv_k_a"][i]),
          "r_k": row(wt["rwkv_r_k"][i]), "ones_bd": ones_bd}
    mw = {"ln_w": row(wt["rwkv_ln_w"][i]), "ln_b": row(wt["rwkv_ln_b"][i]), "ones_bd": ones_bd,
          "w_branch": wt["w_branch"][i].astype(BF16), "w_out": wt["w_out"][i].astype(BF16)}
    fw = {"ln2": row(wt["ln2"][i]), "ffn_up": wt["ffn_up"][i].astype(BF16), "conv_w": wt["ffn_conv_w"][i],
          "conv_b": row(wt["ffn_conv_b"][i]), "ffn_down": wt["ffn_down"][i].astype(BF16),
          "ln3": row(wt["ln3"][i]), "ple_gate": wt["ple_gate"][i].astype(BF16),
          "ple_proj": wt["ple_proj"][i].astype(BF16), "ln_f": row(wt["ln_f"])}
    w_in = wt["w_in"][i].astype(BF16)
    seg = lambda j: w_in[:, _IN_SEGS[j][0]:_IN_SEGS[j][1]]
    wk_heads = seg(3).reshape(D_MODEL, ATT_HEADS, ATT_HEAD).transpose(1, 0, 2)
    pw = {"w_in": w_in, "w_qkv_t": jnp.stack([seg(2).T, seg(3).T, seg(4).T]),
          "wk_aug": jnp.pad(wk_heads, ((0, 0), (0, 0), (0, LANES - ATT_HEAD)))}
    return {"ln1": row(wt["ln1"][i]), "w_in": w_in, "proj": pw, "lru": lw, "rwkv": rw, "merge": mw, "ffn": fw}


def _prompt_layer(x, p, lp, rel_bias, final_norm):
    b, t, _ = x.shape
    m = b * t
    nh, hd = RWKV_HEADS, RWKV_HEAD
    lru_c, rw_c, gates, q_t, k_t, v_t, k_aug = _in_proj_seq(x.reshape(m, D_MODEL), lp["ln1"], lp["proj"], b, t)

    ya, conv_new, h_new = _lru_seq(lru_c.reshape(b, t, 2 * BRANCH_W), lp["lru"], tc=256)

    r_, w_, k_, v_, a_, b_, g_, bonus = _rwkv_pre_seq(rw_c.reshape(b, t, RWKV_COLS), lp["rwkv"], tc=256)
    grp, kr = SCAN_GROUPS, SCAN_KEY_ROWS
    assert b * nh * grp == LANES

    def key_packed(z):
        return z.reshape(b, t, nh, grp, kr).transpose(1, 4, 3, 0, 2).reshape(t, kr, LANES)

    def row_major(z):
        return z.reshape(b, t, nh, hd // grp, grp).transpose(1, 3, 4, 0, 2).reshape(t, hd // grp, LANES)

    y_l, s_l = _rwkv_scan(key_packed(r_), key_packed(w_), key_packed(k_), key_packed(a_), key_packed(b_),
                          row_major(v_), tc=32)
    yr = y_l.reshape(t, hd // grp, grp, b, nh).transpose(3, 0, 4, 1, 2).reshape(m, BRANCH_W)
    s_l = s_l.reshape(-1, grp, kr, SUBLANES, grp, b, nh)
    s_l = jnp.stack([jnp.roll(s_l[:, :, :, :, c], c, axis=1) for c in range(grp)], axis=4)
    s_new = s_l.transpose(5, 6, 0, 3, 4, 1, 2).reshape(b, nh, hd, hd)

    yc_t = _moba_seq(q_t, k_aug, v_t, rel_bias)

    flat = lambda z: z.reshape(m, BRANCH_W)
    x1 = _merge(x.reshape(m, D_MODEL), flat(ya), yr, flat(bonus), flat(g_), yc_t, gates, lp["merge"], tm=MOBA_BLOCK)
    x2, fc_new = _ffn(x1, p.reshape(m, PLE_DIM), lp["ffn"], tm=512, seq_len=t, final_norm=final_norm)
    tokens_major = lambda z: z.reshape(b, ATT_HEADS, ATT_HEAD, t).transpose(0, 3, 1, 2)
    new = {"k": tokens_major(k_t), "v": tokens_major(v_t),
           "lru_h": h_new[:, 0, :], "lru_conv": conv_new, "rwkv": s_new,
           "rwkv_shift": rw_c.reshape(b, t, RWKV_COLS)[:, -1, :], "ffn_conv": fc_new}
    return x2.reshape(b, t, D_MODEL), new


def _sample_layer(x, p, st, lp, rel_bias, final_norm):
    n = x.shape[0]
    lru_c, rw_c, q, k, v, gates = _in_proj(x.reshape(n, D_MODEL), lp["ln1"], lp["w_in"], tm=n)

    ya, conv_new, h_new = _lru_step(lru_c, st["lru_conv"].reshape(n, (LRU_CONV - 1) * BRANCH_W), st["lru_h"], lp["lru"])

    r_, w_, k_, v_, a_, b_, g_, bonus = _rwkv_pre_step(rw_c, st["rwkv_shift"], lp["rwkv"])
    seq_last = lambda z: z.reshape(n, RWKV_HEADS, RWKV_HEAD).transpose(1, 2, 0)
    s_l, y_l = _rwkv_step(st["rwkv"].transpose(0, 2, 3, 4, 1), st["layer"], seq_last(r_), seq_last(w_), seq_last(k_),
                          seq_last(a_), seq_last(b_), seq_last(v_))
    s_new = s_l.transpose(3, 0, 1, 2)
    yr = y_l.transpose(2, 0, 1).reshape(n, BRANCH_W)

    yc = _moba_step(q, k, v, st["cache_k"], st["cache_v"], st["layer"], st["page_table"], rel_bias)

    x1 = _merge(x.reshape(n, D_MODEL), ya, yr, bonus, g_, yc, gates, lp["merge"], tm=n)
    fc_prev = st["ffn_conv"].reshape(n, (FFN_CONV - 1) * 2 * D_FF)
    x2, u = _ffn(x1, p.reshape(n, PLE_DIM), lp["ffn"], tm=n, seq_len=None, prev=fc_prev, final_norm=final_norm)
    new = {"k": k.reshape(n, 1, ATT_HEADS, ATT_HEAD), "v": v.reshape(n, 1, ATT_HEADS, ATT_HEAD),
           "lru_h": h_new, "lru_conv": conv_new.reshape(n, LRU_CONV - 1, BRANCH_W), "rwkv": s_new,
           "rwkv_shift": rw_c, "ffn_conv": jnp.stack([st["ffn_conv"][:, 1, :], u], axis=1)}
    return x2.reshape(n, 1, D_MODEL), new


def kernel(x_prompt, x_sample, cache_k, cache_v, state_lru_h, state_lru_conv, state_rwkv, state_rwkv_shift, state_ffn_conv, page_table, p_prompt, p_sample, ln1, w_in, lru_conv_w, lru_conv_b, lru_w_r, lru_b_r, lru_w_i, lru_b_i, lru_lambda, rwkv_mu, rwkv_w0, rwkv_w2, rwkv_a0, rwkv_a2, rwkv_g2, rwkv_k_k, rwkv_k_a, rwkv_r_k, rwkv_ln_w, rwkv_ln_b, rel_bias, w_branch, w_out, ln2, ffn_up, ffn_conv_w, ffn_conv_b, ffn_down, ln3, ple_gate, ple_proj, ln_f):
    wt = dict(ln1=ln1, w_in=w_in, lru_conv_w=lru_conv_w, lru_conv_b=lru_conv_b, lru_w_r=lru_w_r, lru_b_r=lru_b_r,
              lru_w_i=lru_w_i, lru_b_i=lru_b_i, lru_lambda=lru_lambda, rwkv_mu=rwkv_mu, rwkv_w0=rwkv_w0,
              rwkv_w2=rwkv_w2, rwkv_a0=rwkv_a0, rwkv_a2=rwkv_a2, rwkv_g2=rwkv_g2, rwkv_k_k=rwkv_k_k,
              rwkv_k_a=rwkv_k_a, rwkv_r_k=rwkv_r_k, rwkv_ln_w=rwkv_ln_w, rwkv_ln_b=rwkv_ln_b, w_branch=w_branch,
              w_out=w_out, ln2=ln2, ffn_up=ffn_up, ffn_conv_w=ffn_conv_w, ffn_conv_b=ffn_conv_b, ffn_down=ffn_down,
              ln3=ln3, ple_gate=ple_gate, ple_proj=ple_proj, ln_f=ln_f)
    depth = w_in.shape[0]
    xp, xs = x_prompt, x_sample[:, 0, :]
    outs_p, outs_s = [], []
    for i in range(depth):
        lp = _layer_weights(i, wt)
        last = i == depth - 1
        st = {"cache_k": cache_k, "cache_v": cache_v, "layer": i, "page_table": page_table,
              "lru_h": state_lru_h[i], "lru_conv": state_lru_conv[i], "rwkv": state_rwkv,
              "rwkv_shift": state_rwkv_shift[i], "ffn_conv": state_ffn_conv[i]}
        xp, new_p = _prompt_layer(xp, p_prompt[i], lp, rel_bias, last)
        xs3, new_s = _sample_layer(xs, p_sample[i], st, lp, rel_bias, last)
        xs = xs3[:, 0, :]
        outs_p.append(new_p)
        outs_s.append(new_s)
    stack = lambda outs, name: jnp.stack([o[name] for o in outs])
    res = [xp, xs[:, None, :]]
    for name in ("k", "v", "lru_h", "lru_conv", "rwkv", "rwkv_shift", "ffn_conv"):
        res.append(stack(outs_p, name))
        res.append(stack(outs_s, name))
    return tuple(res)
```

```python
import functools
import math

import numpy as np
import jax
import jax.numpy as jnp
from jax import lax
from jax.experimental import pallas as pl
from jax.experimental.pallas import tpu as pltpu

F32 = jnp.float32
BF16 = jnp.bfloat16
I32 = jnp.int32

D_MODEL = 1024
DEPTH = 2
PAGE_SIZE = 128
N_BRANCH = 3
BRANCH_W = D_MODEL // 2
LRU_BLOCKS = 8
LRU_CONV = 4
LRU_C = 8.0
RWKV_HEAD = 64
RWKV_HEADS = BRANCH_W // RWKV_HEAD
DECAY_LORA = 64
AAA_LORA = 64
GATE_LORA = 128
RWKV_COLS = 3 * BRANCH_W + DECAY_LORA + AAA_LORA + GATE_LORA
RWKV_GN_EPS = 64e-5
ATT_HEAD = 64
ATT_HEADS = BRANCH_W // ATT_HEAD
MOBA_BLOCK = 256
MOBA_TOPK = 3
N_BUCKETS = 32
MAX_DISTANCE = 128
D_FF = 3 * D_MODEL
FFN_CONV = 3
PLE_DIM = 256
RMS_EPS = 1e-6
N_IN = 2 * BRANCH_W + RWKV_COLS + 3 * BRANCH_W + N_BRANCH * D_MODEL

_SEG_EDGES = (0, 2 * BRANCH_W, 2 * BRANCH_W + RWKV_COLS, 2 * BRANCH_W + RWKV_COLS + BRANCH_W,
              2 * BRANCH_W + RWKV_COLS + 2 * BRANCH_W, 2 * BRANCH_W + RWKV_COLS + 3 * BRANCH_W, N_IN)
_IN_SEGS = tuple(zip(_SEG_EDGES[:-1], _SEG_EDGES[1:]))

LANES = 128
SUBLANES = 8
VMEM_LIMIT = 56 * 1024 * 1024
NEG = -1e30
LORA_W = DECAY_LORA + AAA_LORA


def _params(n_axes):
    return pltpu.CompilerParams(dimension_semantics=("arbitrary",) * n_axes, vmem_limit_bytes=VMEM_LIMIT)


def _const(shape):
    return pl.BlockSpec(shape, lambda *_: (0,) * len(shape), pipeline_mode=pl.Buffered(1))


def _softplus(x):
    return jnp.maximum(x, 0.0) + jnp.log1p(jnp.exp(-jnp.abs(x)))


def _gelu_tanh(x):
    return 0.5 * x * (1.0 + jnp.tanh(math.sqrt(2.0 / math.pi) * (x + 0.044715 * (x * x * x))))


def _rms(x, g):
    return x * lax.rsqrt(jnp.mean(x * x, axis=-1, keepdims=True) + RMS_EPS) * g


def _dot(a, b):
    return jnp.dot(a, b, preferred_element_type=F32)


def _dot_nt(a, b, precision=None):
    return lax.dot_general(a, b, (((1,), (1,)), ((), ())), precision=precision, preferred_element_type=F32)


def _head_sum(x, ones_bd):
    hi = x.astype(BF16)
    lo = (x - hi.astype(F32)).astype(BF16)
    return _dot(hi, ones_bd) + _dot(lo, ones_bd)


def _in_proj_kernel(x_ref, g_ref, w_ref, *out_refs):
    xn = _rms(x_ref[...], g_ref[...]).astype(BF16)
    for ref, (lo, hi) in zip(out_refs, _IN_SEGS):
        ref[...] = _dot(xn, w_ref[:, lo:hi])


def _in_proj(x, g, w_bf16, tm):
    m = x.shape[0]
    widths = [hi - lo for lo, hi in _IN_SEGS]
    return pl.pallas_call(
        _in_proj_kernel,
        grid=(m // tm,),
        in_specs=[pl.BlockSpec((tm, D_MODEL), lambda i: (i, 0)), _const((1, D_MODEL)), _const((D_MODEL, N_IN))],
        out_specs=[pl.BlockSpec((tm, w), lambda i: (i, 0)) for w in widths],
        out_shape=[jax.ShapeDtypeStruct((m, w), F32) for w in widths],
        compiler_params=_params(1),
        name="in_proj",
    )(x, g, w_bf16)


_SEQ_SEGS = (0, 1, 5)


def _in_proj_seq_kernel(x_ref, g_ref, w_ref, wt_ref, wka_ref, *refs, nblk, n_carried):
    out_refs = refs[n_carried:]
    xn = _rms(x_ref[...], g_ref[...]).astype(BF16)
    for ref, s in zip(out_refs[:3], _SEQ_SEGS):
        lo, hi = _IN_SEGS[s]
        ref[...] = _dot(xn, w_ref[:, lo:hi])
    for j, ref in enumerate(out_refs[3:6]):
        ref[0] = _dot_nt(wt_ref[j], xn)
    ka_ref = out_refs[6]
    n = pl.program_id(0) % nblk
    onehot = (lax.broadcasted_iota(I32, (1, LANES), 1) == ATT_HEAD + n).astype(F32)
    for h in range(ATT_HEADS):
        ka_ref[0, h] = (_dot(xn, wka_ref[h]) + onehot).astype(BF16)


def _in_proj_seq(x, g, pw, b, t, layer, depth, kv_bufs):
    m = x.shape[0]
    tm = MOBA_BLOCK
    nblk = t // tm
    widths = [_IN_SEGS[s][1] - _IN_SEGS[s][0] for s in _SEQ_SEGS]
    blk_t = pl.BlockSpec((1, BRANCH_W, tm), lambda i: (i // nblk, 0, i % nblk))
    buf_t = pl.BlockSpec((None, 1, BRANCH_W, tm), lambda i: (layer, i // nblk, 0, i % nblk))
    in_specs = [pl.BlockSpec((tm, D_MODEL), lambda i: (i, 0)), _const((1, D_MODEL)), _const((D_MODEL, N_IN)),
                _const((3, BRANCH_W, D_MODEL)), _const((ATT_HEADS, D_MODEL, LANES))]
    args = [x, g, pw["w_in"], pw["w_qkv_t"], pw["wk_aug"]]
    aliases = {}
    if kv_bufs is not None:
        in_specs += [pl.BlockSpec(memory_space=pl.ANY)] * 2
        args += list(kv_bufs)
        aliases = {5: 4, 6: 5}
    buf = jax.ShapeDtypeStruct((depth, b, BRANCH_W, t), F32)
    return pl.pallas_call(
        functools.partial(_in_proj_seq_kernel, nblk=nblk, n_carried=len(aliases)),
        grid=(m // tm,),
        in_specs=in_specs,
        out_specs=[pl.BlockSpec((tm, w), lambda i: (i, 0)) for w in widths]
                  + [blk_t, buf_t, buf_t, pl.BlockSpec((1, ATT_HEADS, tm, LANES), lambda i: (i // nblk, 0, i % nblk, 0))],
        out_shape=[jax.ShapeDtypeStruct((m, w), F32) for w in widths]
                  + [jax.ShapeDtypeStruct((b, BRANCH_W, t), F32), buf, buf,
                     jax.ShapeDtypeStruct((b, ATT_HEADS, t, LANES), BF16)],
        input_output_aliases=aliases,
        compiler_params=_params(1),
        name="in_proj_seq",
    )(*args)


def _lru_gates(xc, ga_unused, wr, br, wi, bi, lam):
    xcb = xc.astype(BF16)
    r = jax.nn.sigmoid(_dot(xcb, wr) + br)
    ig = jax.nn.sigmoid(_dot(xcb, wi) + bi)
    log_a = (-LRU_C * r) * _softplus(-lam)
    a = jnp.exp(log_a)
    bx = jnp.sqrt(1.0 - jnp.exp(2.0 * log_a)) * (ig * xc)
    return a, bx


def _lru_seq_kernel(cols_ref, cw_ref, cb_ref, wr_ref, br_ref, wi_ref, bi_ref, lam_ref,
                    y_ref, conv_ref, h_ref, ext_ref, hc_ref, *, tc):
    t = pl.program_id(1)

    @pl.when(t == 0)
    def _():
        ext_ref[0:SUBLANES, :] = jnp.zeros((SUBLANES, BRANCH_W), F32)
        hc_ref[...] = jnp.zeros_like(hc_ref)

    @pl.when(t > 0)
    def _():
        ext_ref[0:SUBLANES, :] = ext_ref[tc:tc + SUBLANES, :]

    xa = cols_ref[0, :, 0:BRANCH_W]
    ga = cols_ref[0, :, BRANCH_W:]
    ext_ref[SUBLANES:, :] = xa
    xc = ext_ref[pl.ds(SUBLANES - 3, tc), :] * cw_ref[0:1, :]
    xc = xc + ext_ref[pl.ds(SUBLANES - 2, tc), :] * cw_ref[1:2, :]
    xc = xc + ext_ref[pl.ds(SUBLANES - 1, tc), :] * cw_ref[2:3, :]
    xc = xc + xa * cw_ref[3:4, :] + cb_ref[...]
    a, bx = _lru_gates(xc, ga, wr_ref[...], br_ref[...], wi_ref[...], bi_ref[...], lam_ref[...])
    sub = lax.broadcasted_iota(I32, (tc, BRANCH_W), 0) % SUBLANES
    d = 1
    while d < SUBLANES:
        keep = sub >= d
        a_s = jnp.where(keep, pltpu.roll(a, d, 0), 1.0)
        b_s = jnp.where(keep, pltpu.roll(bx, d, 0), 0.0)
        bx = a * b_s + bx
        a = a * a_s
        d *= 2
    carry = hc_ref[...]
    tiles = []
    for g in range(tc // SUBLANES):
        rows = slice(g * SUBLANES, (g + 1) * SUBLANES)
        hg = a[rows] * carry + bx[rows]
        carry = hg[SUBLANES - 1:SUBLANES, :]
        tiles.append(hg)
    h = jnp.concatenate(tiles, axis=0)
    hc_ref[...] = carry
    y_ref[0] = h * _gelu_tanh(ga)
    conv_ref[0] = ext_ref[tc + SUBLANES - 3:tc + SUBLANES, :]
    h_ref[0] = h[tc - 1:tc, :]


def _lru_seq(cols, lw, tc):
    b, t, _ = cols.shape
    vec = _const((1, BRANCH_W))
    mat = _const((BRANCH_W, BRANCH_W))
    return pl.pallas_call(
        functools.partial(_lru_seq_kernel, tc=tc),
        grid=(b, t // tc),
        in_specs=[pl.BlockSpec((1, tc, 2 * BRANCH_W), lambda i, j: (i, j, 0)),
                  _const((LRU_CONV, BRANCH_W)), vec, mat, vec, mat, vec, vec],
        out_specs=[pl.BlockSpec((1, tc, BRANCH_W), lambda i, j: (i, j, 0)),
                   pl.BlockSpec((1, LRU_CONV - 1, BRANCH_W), lambda i, j: (i, 0, 0)),
                   pl.BlockSpec((1, 1, BRANCH_W), lambda i, j: (i, 0, 0))],
        out_shape=[jax.ShapeDtypeStruct((b, t, BRANCH_W), F32),
                   jax.ShapeDtypeStruct((b, LRU_CONV - 1, BRANCH_W), F32),
                   jax.ShapeDtypeStruct((b, 1, BRANCH_W), F32)],
        scratch_shapes=[pltpu.VMEM((tc + SUBLANES, BRANCH_W), F32), pltpu.VMEM((1, BRANCH_W), F32)],
        compiler_params=_params(2),
        name="lru_seq",
    )(cols, lw["conv_w"], lw["conv_b"], lw["w_r"], lw["b_r"], lw["w_i"], lw["b_i"], lw["lam"])


def _lru_step_kernel(cols_ref, buf_ref, h0_ref, cw_ref, cb_ref, wr_ref, br_ref, wi_ref, bi_ref, lam_ref,
                     y_ref, conv_ref, h_ref):
    xa = cols_ref[:, 0:BRANCH_W]
    ga = cols_ref[:, BRANCH_W:]
    xc = buf_ref[:, 0:BRANCH_W] * cw_ref[0:1, :]
    xc = xc + buf_ref[:, BRANCH_W:2 * BRANCH_W] * cw_ref[1:2, :]
    xc = xc + buf_ref[:, 2 * BRANCH_W:] * cw_ref[2:3, :]
    xc = xc + xa * cw_ref[3:4, :] + cb_ref[...]
    a, bx = _lru_gates(xc, ga, wr_ref[...], br_ref[...], wi_ref[...], bi_ref[...], lam_ref[...])
    h = a * h0_ref[...] + bx
    y_ref[...] = h * _gelu_tanh(ga)
    conv_ref[:, 0:2 * BRANCH_W] = buf_ref[:, BRANCH_W:]
    conv_ref[:, 2 * BRANCH_W:] = xa
    h_ref[...] = h


def _lru_step(cols, buf, h0, lw):
    n = cols.shape[0]
    full = lambda w: pl.BlockSpec((n, w), lambda i: (0, 0))
    vec = _const((1, BRANCH_W))
    mat = _const((BRANCH_W, BRANCH_W))
    return pl.pallas_call(
        _lru_step_kernel,
        grid=(1,),
        in_specs=[full(2 * BRANCH_W), full(3 * BRANCH_W), full(BRANCH_W),
                  _const((LRU_CONV, BRANCH_W)), vec, mat, vec, mat, vec, vec],
        out_specs=[full(BRANCH_W), full(3 * BRANCH_W), full(BRANCH_W)],
        out_shape=[jax.ShapeDtypeStruct((n, BRANCH_W), F32), jax.ShapeDtypeStruct((n, 3 * BRANCH_W), F32),
                   jax.ShapeDtypeStruct((n, BRANCH_W), F32)],
        compiler_params=_params(1),
        name="lru_step",
    )(cols, buf, h0, lw["conv_w"], lw["conv_b"], lw["w_r"], lw["b_r"], lw["w_i"], lw["b_i"], lw["lam"])


def _rwkv_token_math(cols, prev, mu, w0, w2p, a0, a2p, g2, k_k, k_a, r_k, ones_bd):
    mixed = cols + (prev - cols) * mu
    r = mixed[:, 0:BRANCH_W]
    k = mixed[:, BRANCH_W:2 * BRANCH_W]
    v = mixed[:, 2 * BRANCH_W:3 * BRANCH_W]
    la = mixed[:, 3 * BRANCH_W:3 * BRANCH_W + LORA_W]
    gl = mixed[:, 3 * BRANCH_W + LORA_W:]
    w = -_softplus(-(w0 + _dot(jnp.tanh(la).astype(BF16), w2p))) - 0.5
    decay = jnp.exp(-jnp.exp(w))
    a = jax.nn.sigmoid(a0 + _dot(la.astype(BF16), a2p))
    g = _dot(jax.nn.sigmoid(gl).astype(BF16), g2)
    kk = k * k_k
    kk = kk / jnp.maximum(jnp.sqrt(_head_sum(kk * kk, ones_bd)), 1e-12)
    k = k * (1.0 + (a - 1.0) * k_a)
    bonus = _head_sum(r * k * r_k, ones_bd) * v
    return r, decay, k, v, -kk, kk * a, g, bonus


_RW_PARAM_ORDER = ("mu", "w0", "w2p", "a0", "a2p", "g2", "k_k", "k_a", "r_k", "ones_bd")


def _rw_param_specs():
    vec = _const((1, BRANCH_W))
    return [_const((1, RWKV_COLS)), vec, _const((LORA_W, BRANCH_W)), vec, _const((LORA_W, BRANCH_W)),
            _const((GATE_LORA, BRANCH_W)), vec, vec, vec, _const((BRANCH_W, BRANCH_W))]


def _rwkv_pre_seq_kernel(cols_ref, *refs, tc):
    prm = [r[...] for r in refs[:10]]
    outs = refs[10:18]
    ext_ref = refs[18]
    t = pl.program_id(1)

    @pl.when(t == 0)
    def _():
        ext_ref[0:SUBLANES, :] = jnp.zeros((SUBLANES, RWKV_COLS), F32)

    @pl.when(t > 0)
    def _():
        ext_ref[0:SUBLANES, :] = ext_ref[tc:tc + SUBLANES, :]

    cols = cols_ref[0]
    ext_ref[SUBLANES:, :] = cols
    prev = ext_ref[pl.ds(SUBLANES - 1, tc), :]
    for ref, val in zip(outs, _rwkv_token_math(cols, prev, *prm)):
        ref[0] = val


def _rwkv_pre_seq(cols, rw, tc):
    b, t, _ = cols.shape
    blk = pl.BlockSpec((1, tc, BRANCH_W), lambda i, j: (i, j, 0))
    return pl.pallas_call(
        functools.partial(_rwkv_pre_seq_kernel, tc=tc),
        grid=(b, t // tc),
        in_specs=[pl.BlockSpec((1, tc, RWKV_COLS), lambda i, j: (i, j, 0))] + _rw_param_specs(),
        out_specs=[blk] * 8,
        out_shape=[jax.ShapeDtypeStruct((b, t, BRANCH_W), F32)] * 8,
        scratch_shapes=[pltpu.VMEM((tc + SUBLANES, RWKV_COLS), F32)],
        compiler_params=_params(2),
        name="rwkv_pre_seq",
    )(cols, *[rw[k] for k in _RW_PARAM_ORDER])


def _rwkv_pre_step_kernel(cols_ref, prev_ref, *refs):
    prm = [r[...] for r in refs[:10]]
    for ref, val in zip(refs[10:18], _rwkv_token_math(cols_ref[...], prev_ref[...], *prm)):
        ref[...] = val


def _rwkv_pre_step(cols, prev, rw):
    n = cols.shape[0]
    full = lambda w: pl.BlockSpec((n, w), lambda i: (0, 0))
    return pl.pallas_call(
        _rwkv_pre_step_kernel,
        grid=(1,),
        in_specs=[full(RWKV_COLS), full(RWKV_COLS)] + _rw_param_specs(),
        out_specs=[full(BRANCH_W)] * 8,
        out_shape=[jax.ShapeDtypeStruct((n, BRANCH_W), F32)] * 8,
        compiler_params=_params(1),
        name="rwkv_pre_step",
    )(cols, prev, *[rw[k] for k in _RW_PARAM_ORDER])


SCAN_GROUPS = 4
SCAN_GROUP_LANES = LANES // SCAN_GROUPS
SCAN_KEY_ROWS = RWKV_HEAD // SCAN_GROUPS


def _rwkv_scan_kernel(r_ref, w_ref, k_ref, a_ref, b_ref, v_ref, y_ref, so_ref, s_ref, exp_ref, *, tc, ni):
    t = pl.program_id(0)

    @pl.when(t == 0)
    def _():
        s_ref[...] = jnp.zeros_like(s_ref)

    def unpack(slot, tt):
        for kk, ref in enumerate((a_ref, b_ref, w_ref, k_ref, r_ref)):
            x = ref[tt]
            for g in range(SCAN_GROUPS):
                xg = x if g == 0 else pltpu.roll(x, LANES - SCAN_GROUP_LANES * g, 1)
                exp_ref[kk, slot, g * SCAN_KEY_ROWS:(g + 1) * SCAN_KEY_ROWS, :] = xg

    unpack(0, 0)

    n_acc = 4

    def fold(accs):
        return (accs[0] + accs[1]) + (accs[2] + accs[3])

    def step(tt, slot):
        unpack(1 - slot, jnp.minimum(tt + 1, tc - 1))
        row = lambda kk, j: exp_ref[kk, slot, pl.ds(j, 1), :]
        groups = range(ni // SUBLANES)
        acc = [[None] * n_acc for _ in groups]
        for j in range(RWKV_HEAD):
            a = row(0, j)
            for g in groups:
                term = s_ref[g, j] * a
                acc[g][j % n_acc] = term if j < n_acc else acc[g][j % n_acc] + term
        sa = [fold(acc[g]) for g in groups]
        vg = [v_ref[tt, g * SUBLANES:(g + 1) * SUBLANES, :] for g in groups]
        acc = [[None] * n_acc for _ in groups]
        for j in range(RWKV_HEAD):
            b, w, k, r = row(1, j), row(2, j), row(3, j), row(4, j)
            for g in groups:
                s = s_ref[g, j] * w + sa[g] * b + vg[g] * k
                s_ref[g, j] = s
                term = s * r
                acc[g][j % n_acc] = term if j < n_acc else acc[g][j % n_acc] + term
        for g in groups:
            y_ref[tt, g * SUBLANES:(g + 1) * SUBLANES, :] = fold(acc[g])

    def two_steps(pair, carry):
        step(2 * pair, 0)
        step(2 * pair + 1, 1)
        return carry

    lax.fori_loop(0, tc // 2, two_steps, 0)

    @pl.when(t == pl.num_programs(0) - 1)
    def _():
        so_ref[...] = s_ref[...]


def _rwkv_scan(r, w, k, a, b, v, tc):
    t = r.shape[0]
    ni = v.shape[1]
    vec = pl.BlockSpec((tc, SCAN_KEY_ROWS, LANES), lambda j: (j, 0, 0))
    row = pl.BlockSpec((tc, ni, LANES), lambda j: (j, 0, 0))
    st = pl.BlockSpec((ni // SUBLANES, RWKV_HEAD, SUBLANES, LANES), lambda j: (0, 0, 0, 0))
    return pl.pallas_call(
        functools.partial(_rwkv_scan_kernel, tc=tc, ni=ni),
        grid=(t // tc,),
        in_specs=[vec] * 5 + [row],
        out_specs=[row, st],
        out_shape=[jax.ShapeDtypeStruct((t, ni, LANES), F32),
                   jax.ShapeDtypeStruct((ni // SUBLANES, RWKV_HEAD, SUBLANES, LANES), F32)],
        scratch_shapes=[pltpu.VMEM((ni // SUBLANES, RWKV_HEAD, SUBLANES, LANES), F32),
                        pltpu.VMEM((5, 2, RWKV_HEAD, LANES), F32)],
        compiler_params=_params(1),
        name="rwkv_scan",
    )(r, w, k, a, b, v)


def _rwkv_step_kernel(s_ref, r_ref, w_ref, k_ref, a_ref, b_ref, v_ref, *refs):
    so_ref, y_ref = refs[-2:]
    a = a_ref[0]
    b = b_ref[0]
    w = w_ref[0]
    k = k_ref[0]
    r = r_ref[0]
    for i in range(RWKV_HEAD):
        s = s_ref[0, i]
        sa = jnp.sum(s * a, axis=0, keepdims=True)
        s = s * w + sa * b + v_ref[0, pl.ds(i, 1), :] * k
        so_ref[0, i] = s
        y_ref[0, pl.ds(i, 1), :] = jnp.sum(s * r, axis=0, keepdims=True)


def _rwkv_step(s, layer, r, w, k, a, b, v, s_out):
    depth, nh, hd, _, n = s.shape
    st = pl.BlockSpec((None, 1, hd, hd, n), lambda i: (layer, i, 0, 0, 0))
    vec = pl.BlockSpec((1, hd, n), lambda i: (i, 0, 0))
    in_specs = [st] + [vec] * 6
    args = [s, r, w, k, a, b, v]
    aliases = {}
    if s_out is not None:
        in_specs.append(pl.BlockSpec(memory_space=pl.ANY))
        args.append(s_out)
        aliases = {7: 0}
    return pl.pallas_call(
        _rwkv_step_kernel,
        grid=(nh,),
        in_specs=in_specs,
        out_specs=[st, vec],
        out_shape=[jax.ShapeDtypeStruct((depth, nh, hd, hd, n), F32), jax.ShapeDtypeStruct((nh, hd, n), F32)],
        input_output_aliases=aliases,
        compiler_params=_params(1),
        name="rwkv_step",
    )(*args)


def _t5_bucket_np(dist):
    n = np.maximum(dist, 0)
    max_exact = N_BUCKETS // 2
    nf = np.maximum(n, 1).astype(np.float32)
    large = max_exact + (np.log(nf / np.float32(max_exact)) / np.float32(math.log(MAX_DISTANCE / max_exact))
                         * np.float32(N_BUCKETS - max_exact)).astype(np.int32)
    large = np.minimum(large, N_BUCKETS - 1)
    return np.where(n < max_exact, n, large).astype(np.int32)


def _moba_seq_kernel(tab_ref, bko_ref, bkp_ref, qt_ref, ka_ref, vt_ref, o_ref,
                     km_ref, bias_ref, qa_ref, m_ref, l_ref, acc_ref, s_ref, *, nblk):
    bi = pl.program_id(0)
    i = pl.program_id(1)
    blk = MOBA_BLOCK
    far_bucket = N_BUCKETS - 1
    sel_rows = 2 * SUBLANES
    log2e = math.log2(math.e)
    scale = ATT_HEAD ** -0.5 * log2e
    ones_rows = jnp.ones((SUBLANES, blk), BF16)

    @pl.when((bi == 0) & (i == 0))
    def _():
        bko = bko_ref[...]
        bkp = bkp_ref[...]
        for h in range(ATT_HEADS):
            c = tab_ref[far_bucket, h]
            own = jnp.zeros((blk, blk), F32)
            prev = jnp.zeros((blk, blk), F32)
            for j in range(N_BUCKETS):
                val = (tab_ref[j, h] - c) * log2e
                own = jnp.where(bko == j, val, own)
                prev = jnp.where(bkp == j, val, prev)
            bias_ref[h, 0] = jnp.where(bko < 0, NEG, own)
            bias_ref[h, 1] = prev

    @pl.when(i == 0)
    def _():
        for h in range(ATT_HEADS):
            km_ref[h] = jnp.zeros((sel_rows, ATT_HEAD), F32)
            for n in range(nblk):
                ks = jnp.sum(ka_ref[0, h, n * blk:(n + 1) * blk, :].astype(F32), axis=0, keepdims=True)
                km_ref[h, n:n + 1, :] = ks[:, 0:ATT_HEAD] * (1.0 / blk)

    row = lax.broadcasted_iota(I32, (sel_rows, blk), 0)
    row_f = row.astype(F32)
    for h in range(ATT_HEADS):
        qt = qt_ref[0, h * ATT_HEAD:(h + 1) * ATT_HEAD, :]
        bs = jnp.dot(km_ref[h], qt, precision=lax.Precision.HIGHEST, preferred_element_type=F32)
        work = jnp.where(row < i, bs, -jnp.inf)
        sel = row == i
        for _ in range(MOBA_TOPK):
            mx = jnp.max(work, axis=0, keepdims=True)
            is_m = (work == mx) & (work > -jnp.inf)
            idx = jnp.min(jnp.where(is_m, row_f, 4.0 * LANES), axis=0, keepdims=True)
            pick = row_f == idx
            sel = sel | pick
            work = jnp.where(pick, -jnp.inf, work)
        selb = jnp.where(sel, 0.0, NEG)
        pad = jnp.zeros((LANES - ATT_HEAD - sel_rows, blk), F32)
        qa_ref[h] = jnp.concatenate([qt * scale, selb, pad], axis=0).astype(BF16)

    key_rows = lambda n: pl.ds(pl.multiple_of(n * blk, blk), blk)

    def block_step(n, kind):
        for h in range(ATT_HEADS):
            s_ref[h] = _dot(ka_ref[0, h, key_rows(n), :], qa_ref[h])
        for h in range(ATT_HEADS):
            s = s_ref[h]
            vt = vt_ref[0, h * ATT_HEAD:(h + 1) * ATT_HEAD, key_rows(n)].astype(BF16)
            if kind == "own":
                s = s + bias_ref[h, 0]
                m_new = jnp.max(s, axis=0, keepdims=True)
                pb = jnp.exp2(s - m_new).astype(BF16)
                l_ref[h] = _dot(ones_rows, pb)[0:1, :]
                acc_ref[h] = _dot(vt, pb)
            else:
                if kind == "prev":
                    s = s + bias_ref[h, 1] + jnp.where(i >= 1, 0.0, NEG)
                m_old = m_ref[h]
                m_new = jnp.maximum(m_old, jnp.max(s, axis=0, keepdims=True))
                alpha = jnp.exp2(m_old - m_new)
                pb = jnp.exp2(s - m_new).astype(BF16)
                l_ref[h] = alpha * l_ref[h] + _dot(ones_rows, pb)[0:1, :]
                acc_ref[h] = alpha * acc_ref[h] + _dot(vt, pb)
            m_ref[h] = m_new

    block_step(i, "own")
    block_step(jnp.maximum(i - 1, 0), "prev")

    def far(n, carry):
        block_step(n, "far")
        return carry

    lax.fori_loop(0, jnp.maximum(i - 1, 0), far, 0)
    for h in range(ATT_HEADS):
        o_ref[0, h * ATT_HEAD:(h + 1) * ATT_HEAD, :] = acc_ref[h] / l_ref[h]


def _moba_seq(q_t, k_aug, v_buf, layer, rel_bias):
    b, nh, t, _ = k_aug.shape
    blk = MOBA_BLOCK
    nblk = t // blk
    assert nblk <= 2 * SUBLANES
    d = np.arange(blk)[None, :] - np.arange(blk)[:, None]
    bko = jnp.asarray(np.where(d >= 0, _t5_bucket_np(d), -1).astype(np.int32))
    bkp = jnp.asarray(_t5_bucket_np(d + blk))
    q_blk = pl.BlockSpec((1, BRANCH_W, blk), lambda i, j: (i, 0, j))
    return pl.pallas_call(
        functools.partial(_moba_seq_kernel, nblk=nblk),
        grid=(b, nblk),
        in_specs=[pl.BlockSpec(memory_space=pltpu.SMEM), _const((blk, blk)), _const((blk, blk)), q_blk,
                  pl.BlockSpec((1, nh, t, LANES), lambda i, j: (i, 0, 0, 0), pipeline_mode=pl.Buffered(1)),
                  pl.BlockSpec((None, 1, BRANCH_W, t), lambda i, j: (layer, i, 0, 0), pipeline_mode=pl.Buffered(1))],
        out_specs=q_blk,
        out_shape=jax.ShapeDtypeStruct((b, BRANCH_W, t), F32),
        scratch_shapes=[pltpu.VMEM((nh, 2 * SUBLANES, ATT_HEAD), F32), pltpu.VMEM((nh, 2, blk, blk), F32),
                        pltpu.VMEM((nh, LANES, blk), BF16), pltpu.VMEM((nh, 1, blk), F32),
                        pltpu.VMEM((nh, 1, blk), F32), pltpu.VMEM((nh, ATT_HEAD, blk), F32),
                        pltpu.VMEM((nh, blk, blk), F32)],
        compiler_params=_params(2),
        name="moba_seq",
    )(rel_bias, bko, bkp, q_t, k_aug, v_buf)


def _moba_step_kernel(pt_ref, q_ref, kn_ref, vn_ref, relt_ref, bkt_ref, *refs, n_pages):
    kp = refs[:n_pages]
    vp = refs[n_pages:2 * n_pages]
    o_ref, acc_ref = refs[2 * n_pages:2 * n_pages + 2]
    pages_per_block = MOBA_BLOCK // PAGE_SIZE
    n_blocks = n_pages // pages_per_block
    scale = ATT_HEAD ** -0.5
    heads = range(ATT_HEADS)

    def per_head(fn):
        return jnp.concatenate([fn(h) for h in heads], axis=0)

    relt = relt_ref[...]
    c_far = relt[:, N_BUCKETS - 1:N_BUCKETS]
    bkt = bkt_ref[...]
    bias_last = jnp.zeros((ATT_HEADS, PAGE_SIZE), F32)
    for j in range(N_BUCKETS):
        bias_last = jnp.where(bkt == j, relt[:, j:j + 1] - c_far, bias_last)

    q = q_ref[0]
    qb = [jnp.broadcast_to(q[h], (ATT_HEAD, PAGE_SIZE)) for h in heads]
    logit, rowsum = [], []
    for p in range(n_pages):
        raw = per_head(lambda h: jnp.sum(kp[p][0, 0, h] * qb[h], axis=0, keepdims=True))
        rowsum.append(jnp.sum(raw, axis=1, keepdims=True))
        lg = raw * scale
        logit.append(lg + bias_last if p == n_pages - 1 else lg)

    sc = []
    for n in range(n_blocks):
        tot = rowsum[n * pages_per_block]
        for j in range(1, pages_per_block):
            tot = tot + rowsum[n * pages_per_block + j]
        sc.append(tot * (1.0 / MOBA_BLOCK))
    sel = []
    for n in range(n_blocks):
        rank = jnp.zeros((ATT_HEADS, 1), I32)
        for j in range(n_blocks):
            if j != n:
                ahead = sc[j] > sc[n]
                if j < n:
                    ahead = ahead | (sc[j] == sc[n])
                rank = rank + ahead.astype(I32)
        sel.append(rank < MOBA_TOPK)

    s_self = per_head(lambda h: jnp.sum(q[h] * kn_ref[0, h], axis=0, keepdims=True)) * scale + (relt[:, 0:1] - c_far)
    m_all = s_self
    for p in range(n_pages):
        m_all = jnp.maximum(m_all, jnp.where(sel[p // pages_per_block], jnp.max(logit[p], axis=1, keepdims=True), -jnp.inf))
    w_self = jnp.exp(s_self - m_all)
    l_all = w_self
    acc_ref[...] = jnp.zeros_like(acc_ref)
    for p in range(n_pages):
        e = jnp.where(sel[p // pages_per_block], jnp.exp(logit[p] - m_all), 0.0)
        l_all = l_all + jnp.sum(e, axis=1, keepdims=True)
        for h in heads:
            acc_ref[h] += vp[p][0, 0, h] * e[h:h + 1, :]
    for h in heads:
        out = jnp.sum(acc_ref[h], axis=1, keepdims=True) + w_self[h:h + 1, :] * vn_ref[0, h]
        o_ref[0, h] = out / l_all[h:h + 1, :]


def _moba_step(q, k_new, v_new, cache_k, cache_v, layer, page_table, rel_bias):
    n, n_pages = page_table.shape
    col = lambda z: z.reshape(n, ATT_HEADS, ATT_HEAD, 1)
    col_spec = pl.BlockSpec((1, ATT_HEADS, ATT_HEAD, 1), lambda i, pt: (i, 0, 0, 0))
    relt = jnp.pad(rel_bias.T, ((0, 0), (0, LANES - N_BUCKETS)))
    bkt = jnp.asarray(_t5_bucket_np(PAGE_SIZE - np.arange(PAGE_SIZE))[None, :])

    def page_spec(p):
        return pl.BlockSpec((1, 1, ATT_HEADS, ATT_HEAD, PAGE_SIZE),
                            lambda i, pt: (layer, pt[i * n_pages + p], 0, 0, 0))

    grid_spec = pltpu.PrefetchScalarGridSpec(
        num_scalar_prefetch=1,
        grid=(n,),
        in_specs=[col_spec, col_spec, col_spec,
                  pl.BlockSpec((ATT_HEADS, LANES), lambda i, pt: (0, 0)),
                  pl.BlockSpec((1, PAGE_SIZE), lambda i, pt: (0, 0))]
                 + [page_spec(p) for p in range(n_pages)] * 2,
        out_specs=col_spec,
        scratch_shapes=[pltpu.VMEM((ATT_HEADS, ATT_HEAD, PAGE_SIZE), F32)],
    )
    rows_last = lambda c: c.transpose(0, 1, 3, 4, 2)
    out = pl.pallas_call(
        functools.partial(_moba_step_kernel, n_pages=n_pages),
        grid_spec=grid_spec,
        out_shape=jax.ShapeDtypeStruct((n, ATT_HEADS, ATT_HEAD, 1), F32),
        compiler_params=_params(1),
        name="moba_step",
    )(page_table.reshape(-1), col(q), col(k_new), col(v_new), relt, bkt,
      *([rows_last(cache_k)] * n_pages), *([rows_last(cache_v)] * n_pages))
    return out.reshape(n, BRANCH_W)


def _merge_kernel(x_ref, ya_ref, yr_ref, bon_ref, g_ref, yc_ref, gate_ref,
                  lnw_ref, lnb_ref, ones_ref, wb_ref, wo_ref, o_ref, *, yc_transposed):
    ones_bd = ones_ref[...]
    y = yr_ref[...]
    mu = _head_sum(y, ones_bd) * (1.0 / RWKV_HEAD)
    d = y - mu
    var = _head_sum(d * d, ones_bd) * (1.0 / RWKV_HEAD)
    yb = (d * lax.rsqrt(var + RWKV_GN_EPS) * lnw_ref[...] + lnb_ref[...] + bon_ref[...]) * g_ref[...]
    yc = yc_ref[0].T if yc_transposed else yc_ref[...]
    merged = None
    for j, yj in enumerate((ya_ref[...], yb, yc)):
        gate = jax.nn.sigmoid(gate_ref[:, j * D_MODEL:(j + 1) * D_MODEL])
        term = _dot(yj.astype(BF16), wb_ref[j]) * gate
        merged = term if merged is None else merged + term
    o_ref[...] = x_ref[...] + _dot(merged.astype(BF16), wo_ref[...])


def _merge(x, ya, yr, bonus, g, yc, gates, mw, tm):
    m = x.shape[0]
    tok = lambda w: pl.BlockSpec((tm, w), lambda i: (i, 0))
    vec = _const((1, BRANCH_W))
    yc_transposed = yc.ndim == 3
    if yc_transposed:
        nblk = yc.shape[2] // tm
        yc_spec = pl.BlockSpec((1, BRANCH_W, tm), lambda i: (i // nblk, 0, i % nblk))
    else:
        yc_spec = tok(BRANCH_W)
    return pl.pallas_call(
        functools.partial(_merge_kernel, yc_transposed=yc_transposed),
        grid=(m // tm,),
        in_specs=[tok(D_MODEL)] + [tok(BRANCH_W)] * 4 + [yc_spec, tok(N_BRANCH * D_MODEL), vec, vec,
                  _const((BRANCH_W, BRANCH_W)), _const((N_BRANCH, BRANCH_W, D_MODEL)), _const((D_MODEL, D_MODEL))],
        out_specs=tok(D_MODEL),
        out_shape=jax.ShapeDtypeStruct((m, D_MODEL), F32),
        compiler_params=_params(1),
        name="merge",
    )(x, ya, yr, bonus, g, yc, gates, mw["ln_w"], mw["ln_b"], mw["ones_bd"], mw["w_branch"], mw["w_out"])


FFN_CHUNK = 512


def _ffn_kernel(*refs, seq_mode, final_norm, tm, tiles_per_seq):
    (x_ref, p_ref, ln2_ref, wup_ref, fcw_ref, fcb_ref, wdn_ref, ln3_ref, pg_ref, pp_ref, lnf_ref) = refs[:11]
    if seq_mode:
        o_ref, fc_ref, ext_ref = refs[11:14]
        i = pl.program_id(0)

        @pl.when(i % tiles_per_seq == 0)
        def _():
            ext_ref[0:SUBLANES, :] = jnp.zeros((SUBLANES, 2 * D_FF), F32)

        @pl.when(i % tiles_per_seq != 0)
        def _():
            ext_ref[0:SUBLANES, :] = ext_ref[tm:tm + SUBLANES, :]
    else:
        prev_ref, o_ref, u_ref = refs[11:14]

    x = x_ref[...]
    hb = _rms(x, ln2_ref[...]).astype(BF16)

    def up_cols(lo, hi):
        u = _dot(hb, wup_ref[:, lo:hi])
        if seq_mode:
            ext_ref[SUBLANES:, lo:hi] = u
        else:
            u_ref[:, lo:hi] = u

    def conv_cols(lo, hi):
        if seq_mode:
            u = ext_ref[SUBLANES:, lo:hi]
            u2 = ext_ref[pl.ds(SUBLANES - 2, tm), lo:hi]
            u1 = ext_ref[pl.ds(SUBLANES - 1, tm), lo:hi]
        else:
            u = u_ref[:, lo:hi]
            u2 = prev_ref[:, lo:hi]
            u1 = prev_ref[:, 2 * D_FF + lo:2 * D_FF + hi]
        return u2 * fcw_ref[0:1, lo:hi] + u1 * fcw_ref[1:2, lo:hi] + u * fcw_ref[2:3, lo:hi] + fcb_ref[:, lo:hi]

    def up_chunk(c):
        up_cols(c * FFN_CHUNK, (c + 1) * FFN_CHUNK)
        up_cols(D_FF + c * FFN_CHUNK, D_FF + (c + 1) * FFN_CHUNK)

    n_chunks = D_FF // FFN_CHUNK
    up_chunk(0)
    acc = None
    for c in range(n_chunks):
        if c + 1 < n_chunks:
            up_chunk(c + 1)
        lo, hi = c * FFN_CHUNK, (c + 1) * FFN_CHUNK
        act = _gelu_tanh(conv_cols(lo, hi)) * conv_cols(D_FF + lo, D_FF + hi)
        term = _dot(act.astype(BF16), wdn_ref[lo:hi, :])
        acc = term if acc is None else acc + term
    x = x + acc
    x = x + jax.nn.sigmoid(_dot(_rms(x, ln3_ref[...]).astype(BF16), pg_ref[...])) * _dot(p_ref[...].astype(BF16), pp_ref[...])
    o_ref[...] = _rms(x, lnf_ref[...]) if final_norm else x
    if seq_mode:
        fc_ref[0] = ext_ref[tm + SUBLANES - 2:tm + SUBLANES, :]


def _ffn(x, p, fw, tm, seq_len, prev=None, final_norm=False):
    m = x.shape[0]
    seq_mode = seq_len is not None
    tok = lambda w: pl.BlockSpec((tm, w), lambda i: (i, 0))
    vecd = _const((1, D_MODEL))
    in_specs = [tok(D_MODEL), tok(PLE_DIM), vecd, _const((D_MODEL, 2 * D_FF)), _const((FFN_CONV, 2 * D_FF)),
                _const((1, 2 * D_FF)), _const((D_FF, D_MODEL)), vecd, _const((D_MODEL, D_MODEL)),
                _const((PLE_DIM, D_MODEL)), vecd]
    args = [x, p, fw["ln2"], fw["ffn_up"], fw["conv_w"], fw["conv_b"], fw["ffn_down"], fw["ln3"],
            fw["ple_gate"], fw["ple_proj"], fw["ln_f"]]
    if seq_mode:
        tiles_per_seq = seq_len // tm
        out_specs = [tok(D_MODEL), pl.BlockSpec((1, FFN_CONV - 1, 2 * D_FF), lambda i: (i // tiles_per_seq, 0, 0))]
        out_shape = [jax.ShapeDtypeStruct((m, D_MODEL), F32),
                     jax.ShapeDtypeStruct((m // seq_len, FFN_CONV - 1, 2 * D_FF), F32)]
        scratch = [pltpu.VMEM((tm + SUBLANES, 2 * D_FF), F32)]
    else:
        tiles_per_seq = 1
        in_specs.append(tok((FFN_CONV - 1) * 2 * D_FF))
        args.append(prev)
        out_specs = [tok(D_MODEL), tok(2 * D_FF)]
        out_shape = [jax.ShapeDtypeStruct((m, D_MODEL), F32), jax.ShapeDtypeStruct((m, 2 * D_FF), F32)]
        scratch = []
    return pl.pallas_call(
        functools.partial(_ffn_kernel, seq_mode=seq_mode, final_norm=final_norm, tm=tm, tiles_per_seq=tiles_per_seq),
        grid=(m // tm,),
        in_specs=in_specs,
        out_specs=out_specs,
        out_shape=out_shape,
        scratch_shapes=scratch,
        compiler_params=_params(1),
        name="ffn",
    )(*args)


def _block_diag(w):
    g, n, _ = w.shape
    eye = jnp.eye(g, dtype=w.dtype)
    return (eye[:, None, :, None] * w[:, :, None, :]).reshape(g * n, g * n)


def _layer_weights(i, wt):
    row = lambda v: v.reshape(1, -1)
    ones_bd = _block_diag(jnp.ones((RWKV_HEADS, RWKV_HEAD, RWKV_HEAD), BF16))
    zeros_lora = jnp.zeros((DECAY_LORA, BRANCH_W), BF16)
    lw = {"conv_w": wt["lru_conv_w"][i], "conv_b": row(wt["lru_conv_b"][i]),
          "w_r": _block_diag(wt["lru_w_r"][i]).astype(BF16), "b_r": row(wt["lru_b_r"][i]),
          "w_i": _block_diag(wt["lru_w_i"][i]).astype(BF16), "b_i": row(wt["lru_b_i"][i]),
          "lam": row(wt["lru_lambda"][i])}
    rw = {"mu": row(wt["rwkv_mu"][i]), "w0": row(wt["rwkv_w0"][i]),
          "w2p": jnp.concatenate([wt["rwkv_w2"][i].astype(BF16), zeros_lora], axis=0),
          "a0": row(wt["rwkv_a0"][i]),
          "a2p": jnp.concatenate([zeros_lora, wt["rwkv_a2"][i].astype(BF16)], axis=0),
          "g2": wt["rwkv_g2"][i].astype(BF16), "k_k": row(wt["rwkv_k_k"][i]), "k_a": row(wt["rwkv_k_a"][i]),
          "r_k": row(wt["rwkv_r_k"][i]), "ones_bd": ones_bd}
    mw = {"ln_w": row(wt["rwkv_ln_w"][i]), "ln_b": row(wt["rwkv_ln_b"][i]), "ones_bd": ones_bd,
          "w_branch": wt["w_branch"][i].astype(BF16), "w_out": wt["w_out"][i].astype(BF16)}
    fw = {"ln2": row(wt["ln2"][i]), "ffn_up": wt["ffn_up"][i].astype(BF16), "conv_w": wt["ffn_conv_w"][i],
          "conv_b": row(wt["ffn_conv_b"][i]), "ffn_down": wt["ffn_down"][i].astype(BF16),
          "ln3": row(wt["ln3"][i]), "ple_gate": wt["ple_gate"][i].astype(BF16),
          "ple_proj": wt["ple_proj"][i].astype(BF16), "ln_f": row(wt["ln_f"])}
    w_in = wt["w_in"][i].astype(BF16)
    seg = lambda j: w_in[:, _IN_SEGS[j][0]:_IN_SEGS[j][1]]
    wk_heads = seg(3).reshape(D_MODEL, ATT_HEADS, ATT_HEAD).transpose(1, 0, 2)
    pw = {"w_in": w_in, "w_qkv_t": jnp.stack([seg(2).T, seg(3).T, seg(4).T]),
          "wk_aug": jnp.pad(wk_heads, ((0, 0), (0, 0), (0, LANES - ATT_HEAD)))}
    return {"ln1": row(wt["ln1"][i]), "w_in": w_in, "proj": pw, "lru": lw, "rwkv": rw, "merge": mw, "ffn": fw}


def _prompt_layer(x, p, lp, rel_bias, final_norm, layer=0, depth=1, kv_bufs=None):
    b, t, _ = x.shape
    m = b * t
    nh, hd = RWKV_HEADS, RWKV_HEAD
    lru_c, rw_c, gates, q_t, k_buf, v_buf, k_aug = _in_proj_seq(x.reshape(m, D_MODEL), lp["ln1"], lp["proj"], b, t,
                                                                 layer, depth, kv_bufs)

    ya, conv_new, h_new = _lru_seq(lru_c.reshape(b, t, 2 * BRANCH_W), lp["lru"], tc=512)

    r_, w_, k_, v_, a_, b_, g_, bonus = _rwkv_pre_seq(rw_c.reshape(b, t, RWKV_COLS), lp["rwkv"], tc=512)
    grp, kr = SCAN_GROUPS, SCAN_KEY_ROWS
    assert b * nh * grp == LANES

    def key_packed(z):
        return z.reshape(b, t, nh, grp, kr).transpose(1, 4, 3, 0, 2).reshape(t, kr, LANES)

    def row_major(z):
        return z.reshape(b, t, nh, hd // grp, grp).transpose(1, 3, 4, 0, 2).reshape(t, hd // grp, LANES)

    y_l, s_l = _rwkv_scan(key_packed(r_), key_packed(w_), key_packed(k_), key_packed(a_), key_packed(b_),
                          row_major(v_), tc=64)
    yr = y_l.reshape(t, hd // grp, grp, b, nh).transpose(3, 0, 4, 1, 2).reshape(m, BRANCH_W)
    s_l = s_l.reshape(-1, grp, kr, SUBLANES, grp, b, nh)
    s_l = jnp.stack([jnp.roll(s_l[:, :, :, :, c], c, axis=1) for c in range(grp)], axis=4)
    s_new = s_l.transpose(5, 6, 0, 3, 4, 1, 2).reshape(b, nh, hd, hd)

    yc_t = _moba_seq(q_t, k_aug, v_buf, layer, rel_bias)

    flat = lambda z: z.reshape(m, BRANCH_W)
    x1 = _merge(x.reshape(m, D_MODEL), flat(ya), yr, flat(bonus), flat(g_), yc_t, gates, lp["merge"], tm=512)
    x2, fc_new = _ffn(x1, p.reshape(m, PLE_DIM), lp["ffn"], tm=512, seq_len=t, final_norm=final_norm)
    new = {"kv_bufs": (k_buf, v_buf),
           "lru_h": h_new[:, 0, :], "lru_conv": conv_new, "rwkv": s_new,
           "rwkv_shift": rw_c.reshape(b, t, RWKV_COLS)[:, -1, :], "ffn_conv": fc_new}
    return x2.reshape(b, t, D_MODEL), new


def _sample_layer(x, p, st, lp, rel_bias, final_norm, s_out=None):
    n = x.shape[0]
    lru_c, rw_c, q, k, v, gates = _in_proj(x.reshape(n, D_MODEL), lp["ln1"], lp["w_in"], tm=n)

    ya, conv_new, h_new = _lru_step(lru_c, st["lru_conv"].reshape(n, (LRU_CONV - 1) * BRANCH_W), st["lru_h"], lp["lru"])

    r_, w_, k_, v_, a_, b_, g_, bonus = _rwkv_pre_step(rw_c, st["rwkv_shift"], lp["rwkv"])
    seq_last = lambda z: z.reshape(n, RWKV_HEADS, RWKV_HEAD).transpose(1, 2, 0)
    s_l, y_l = _rwkv_step(st["rwkv"].transpose(0, 2, 3, 4, 1), st["layer"], seq_last(r_), seq_last(w_), seq_last(k_),
                          seq_last(a_), seq_last(b_), seq_last(v_), s_out)
    yr = y_l.transpose(2, 0, 1).reshape(n, BRANCH_W)

    yc = _moba_step(q, k, v, st["cache_k"], st["cache_v"], st["layer"], st["page_table"], rel_bias)

    x1 = _merge(x.reshape(n, D_MODEL), ya, yr, bonus, g_, yc, gates, lp["merge"], tm=n)
    fc_prev = st["ffn_conv"].reshape(n, (FFN_CONV - 1) * 2 * D_FF)
    x2, u = _ffn(x1, p.reshape(n, PLE_DIM), lp["ffn"], tm=n, seq_len=None, prev=fc_prev, final_norm=final_norm)
    new = {"k": k.reshape(n, 1, ATT_HEADS, ATT_HEAD), "v": v.reshape(n, 1, ATT_HEADS, ATT_HEAD),
           "lru_h": h_new, "lru_conv": conv_new.reshape(n, LRU_CONV - 1, BRANCH_W), "rwkv_buf": s_l,
           "rwkv_shift": rw_c, "ffn_conv": jnp.stack([st["ffn_conv"][:, 1, :], u], axis=1)}
    return x2.reshape(n, 1, D_MODEL), new


def kernel(x_prompt, x_sample, cache_k, cache_v, state_lru_h, state_lru_conv, state_rwkv, state_rwkv_shift, state_ffn_conv, page_table, p_prompt, p_sample, ln1, w_in, lru_conv_w, lru_conv_b, lru_w_r, lru_b_r, lru_w_i, lru_b_i, lru_lambda, rwkv_mu, rwkv_w0, rwkv_w2, rwkv_a0, rwkv_a2, rwkv_g2, rwkv_k_k, rwkv_k_a, rwkv_r_k, rwkv_ln_w, rwkv_ln_b, rel_bias, w_branch, w_out, ln2, ffn_up, ffn_conv_w, ffn_conv_b, ffn_down, ln3, ple_gate, ple_proj, ln_f):
    wt = dict(ln1=ln1, w_in=w_in, lru_conv_w=lru_conv_w, lru_conv_b=lru_conv_b, lru_w_r=lru_w_r, lru_b_r=lru_b_r,
              lru_w_i=lru_w_i, lru_b_i=lru_b_i, lru_lambda=lru_lambda, rwkv_mu=rwkv_mu, rwkv_w0=rwkv_w0,
              rwkv_w2=rwkv_w2, rwkv_a0=rwkv_a0, rwkv_a2=rwkv_a2, rwkv_g2=rwkv_g2, rwkv_k_k=rwkv_k_k,
              rwkv_k_a=rwkv_k_a, rwkv_r_k=rwkv_r_k, rwkv_ln_w=rwkv_ln_w, rwkv_ln_b=rwkv_ln_b, w_branch=w_branch,
              w_out=w_out, ln2=ln2, ffn_up=ffn_up, ffn_conv_w=ffn_conv_w, ffn_conv_b=ffn_conv_b, ffn_down=ffn_down,
              ln3=ln3, ple_gate=ple_gate, ple_proj=ple_proj, ln_f=ln_f)
    depth = w_in.shape[0]
    b, t = x_prompt.shape[:2]
    xp, xs = x_prompt, x_sample[:, 0, :]
    outs_p, outs_s = [], []
    kv_bufs, s_buf = None, None
    for i in range(depth):
        lp = _layer_weights(i, wt)
        last = i == depth - 1
        st = {"cache_k": cache_k, "cache_v": cache_v, "layer": i, "page_table": page_table,
              "lru_h": state_lru_h[i], "lru_conv": state_lru_conv[i], "rwkv": state_rwkv,
              "rwkv_shift": state_rwkv_shift[i], "ffn_conv": state_ffn_conv[i]}
        xp, new_p = _prompt_layer(xp, p_prompt[i], lp, rel_bias, last, i, depth, kv_bufs)
        xs3, new_s = _sample_layer(xs, p_sample[i], st, lp, rel_bias, last, s_buf)
        xs = xs3[:, 0, :]
        kv_bufs, s_buf = new_p["kv_bufs"], new_s["rwkv_buf"]
        outs_p.append(new_p)
        outs_s.append(new_s)
    stack = lambda outs, name: jnp.stack([o[name] for o in outs])
    tokens_major = lambda z: z.reshape(depth, b, ATT_HEADS, ATT_HEAD, t).transpose(0, 1, 4, 2, 3)
    res = [xp, xs[:, None, :],
           tokens_major(kv_bufs[0]), stack(outs_s, "k"), tokens_major(kv_bufs[1]), stack(outs_s, "v")]
    for name in ("lru_h", "lru_conv", "rwkv", "rwkv_shift", "ffn_conv"):
        res.append(stack(outs_p, name))
        res.append(s_buf.transpose(0, 4, 1, 2, 3) if name == "rwkv" else stack(outs_s, name))
    return tuple(res)
```

```python
import functools
import math

import numpy as np
import jax
import jax.numpy as jnp
from jax import lax
from jax.experimental import pallas as pl
from jax.experimental.pallas import tpu as pltpu

F32 = jnp.float32
BF16 = jnp.bfloat16
I32 = jnp.int32

D_MODEL = 1024
DEPTH = 2
PAGE_SIZE = 128
N_BRANCH = 3
BRANCH_W = D_MODEL // 2
LRU_BLOCKS = 8
LRU_CONV = 4
LRU_C = 8.0
RWKV_HEAD = 64
RWKV_HEADS = BRANCH_W // RWKV_HEAD
DECAY_LORA = 64
AAA_LORA = 64
GATE_LORA = 128
RWKV_COLS = 3 * BRANCH_W + DECAY_LORA + AAA_LORA + GATE_LORA
RWKV_GN_EPS = 64e-5
ATT_HEAD = 64
ATT_HEADS = BRANCH_W // ATT_HEAD
MOBA_BLOCK = 256
MOBA_TOPK = 3
N_BUCKETS = 32
MAX_DISTANCE = 128
D_FF = 3 * D_MODEL
FFN_CONV = 3
PLE_DIM = 256
RMS_EPS = 1e-6
N_IN = 2 * BRANCH_W + RWKV_COLS + 3 * BRANCH_W + N_BRANCH * D_MODEL

_SEG_EDGES = (0, 2 * BRANCH_W, 2 * BRANCH_W + RWKV_COLS, 2 * BRANCH_W + RWKV_COLS + BRANCH_W,
              2 * BRANCH_W + RWKV_COLS + 2 * BRANCH_W, 2 * BRANCH_W + RWKV_COLS + 3 * BRANCH_W, N_IN)
_IN_SEGS = tuple(zip(_SEG_EDGES[:-1], _SEG_EDGES[1:]))

LANES = 128
SUBLANES = 8
VMEM_LIMIT = 56 * 1024 * 1024
NEG = -1e30
LORA_W = DECAY_LORA + AAA_LORA


def _params(n_axes):
    return pltpu.CompilerParams(dimension_semantics=("arbitrary",) * n_axes, vmem_limit_bytes=VMEM_LIMIT)


def _const(shape):
    return pl.BlockSpec(shape, lambda *_: (0,) * len(shape), pipeline_mode=pl.Buffered(1))


def _softplus(x):
    return jnp.maximum(x, 0.0) + jnp.log1p(jnp.exp(-jnp.abs(x)))


def _gelu_tanh(x):
    return 0.5 * x * (1.0 + jnp.tanh(math.sqrt(2.0 / math.pi) * (x + 0.044715 * (x * x * x))))


def _rms(x, g):
    return x * lax.rsqrt(jnp.mean(x * x, axis=-1, keepdims=True) + RMS_EPS) * g


def _dot(a, b):
    return jnp.dot(a, b, preferred_element_type=F32)


def _dot_nt(a, b, precision=None):
    return lax.dot_general(a, b, (((1,), (1,)), ((), ())), precision=precision, preferred_element_type=F32)


def _head_sum(x, ones_bd):
    hi = x.astype(BF16)
    lo = (x - hi.astype(F32)).astype(BF16)
    return _dot(hi, ones_bd) + _dot(lo, ones_bd)


def _in_proj_kernel(x_ref, g_ref, w_ref, *out_refs):
    xn = _rms(x_ref[...], g_ref[...]).astype(BF16)
    for ref, (lo, hi) in zip(out_refs, _IN_SEGS):
        ref[...] = _dot(xn, w_ref[:, lo:hi])


def _in_proj(x, g, w_bf16, tm):
    m = x.shape[0]
    widths = [hi - lo for lo, hi in _IN_SEGS]
    return pl.pallas_call(
        _in_proj_kernel,
        grid=(m // tm,),
        in_specs=[pl.BlockSpec((tm, D_MODEL), lambda i: (i, 0)), _const((1, D_MODEL)), _const((D_MODEL, N_IN))],
        out_specs=[pl.BlockSpec((tm, w), lambda i: (i, 0)) for w in widths],
        out_shape=[jax.ShapeDtypeStruct((m, w), F32) for w in widths],
        compiler_params=_params(1),
        name="in_proj",
    )(x, g, w_bf16)


_SEQ_SEGS = (0, 1, 5)


def _in_proj_seq_kernel(x_ref, g_ref, w_ref, wt_ref, wka_ref, *refs, nblk, n_carried):
    out_refs = refs[n_carried:]
    xn = _rms(x_ref[...], g_ref[...]).astype(BF16)
    for ref, s in zip(out_refs[:3], _SEQ_SEGS):
        lo, hi = _IN_SEGS[s]
        ref[...] = _dot(xn, w_ref[:, lo:hi])
    for j, ref in enumerate(out_refs[3:6]):
        ref[0] = _dot_nt(wt_ref[j], xn)
    ka_ref = out_refs[6]
    n = pl.program_id(0) % nblk
    onehot = (lax.broadcasted_iota(I32, (1, LANES), 1) == ATT_HEAD + n).astype(F32)
    for h in range(ATT_HEADS):
        ka_ref[0, h] = (_dot(xn, wka_ref[h]) + onehot).astype(BF16)


def _in_proj_seq(x, g, pw, b, t, layer, depth, kv_bufs):
    m = x.shape[0]
    tm = MOBA_BLOCK
    nblk = t // tm
    widths = [_IN_SEGS[s][1] - _IN_SEGS[s][0] for s in _SEQ_SEGS]
    blk_t = pl.BlockSpec((1, BRANCH_W, tm), lambda i: (i // nblk, 0, i % nblk))
    buf_t = pl.BlockSpec((None, 1, BRANCH_W, tm), lambda i: (layer, i // nblk, 0, i % nblk))
    in_specs = [pl.BlockSpec((tm, D_MODEL), lambda i: (i, 0)), _const((1, D_MODEL)), _const((D_MODEL, N_IN)),
                _const((3, BRANCH_W, D_MODEL)), _const((ATT_HEADS, D_MODEL, LANES))]
    args = [x, g, pw["w_in"], pw["w_qkv_t"], pw["wk_aug"]]
    aliases = {}
    if kv_bufs is not None:
        in_specs += [pl.BlockSpec(memory_space=pl.ANY)] * 2
        args += list(kv_bufs)
        aliases = {5: 4, 6: 5}
    buf = jax.ShapeDtypeStruct((depth, b, BRANCH_W, t), F32)
    return pl.pallas_call(
        functools.partial(_in_proj_seq_kernel, nblk=nblk, n_carried=len(aliases)),
        grid=(m // tm,),
        in_specs=in_specs,
        out_specs=[pl.BlockSpec((tm, w), lambda i: (i, 0)) for w in widths]
                  + [blk_t, buf_t, buf_t, pl.BlockSpec((1, ATT_HEADS, tm, LANES), lambda i: (i // nblk, 0, i % nblk, 0))],
        out_shape=[jax.ShapeDtypeStruct((m, w), F32) for w in widths]
                  + [jax.ShapeDtypeStruct((b, BRANCH_W, t), F32), buf, buf,
                     jax.ShapeDtypeStruct((b, ATT_HEADS, t, LANES), BF16)],
        input_output_aliases=aliases,
        compiler_params=_params(1),
        name="in_proj_seq",
    )(*args)


def _lru_gates(xc, ga_unused, wr, br, wi, bi, lam):
    xcb = xc.astype(BF16)
    r = jax.nn.sigmoid(_dot(xcb, wr) + br)
    ig = jax.nn.sigmoid(_dot(xcb, wi) + bi)
    log_a = (-LRU_C * r) * _softplus(-lam)
    a = jnp.exp(log_a)
    bx = jnp.sqrt(1.0 - jnp.exp(2.0 * log_a)) * (ig * xc)
    return a, bx


def _lru_seq_kernel(cols_ref, cw_ref, cb_ref, wr_ref, br_ref, wi_ref, bi_ref, lam_ref,
                    y_ref, conv_ref, h_ref, ext_ref, hc_ref, *, tc):
    t = pl.program_id(1)

    @pl.when(t == 0)
    def _():
        ext_ref[0:SUBLANES, :] = jnp.zeros((SUBLANES, BRANCH_W), F32)
        hc_ref[...] = jnp.zeros_like(hc_ref)

    @pl.when(t > 0)
    def _():
        ext_ref[0:SUBLANES, :] = ext_ref[tc:tc + SUBLANES, :]

    xa = cols_ref[0, :, 0:BRANCH_W]
    ga = cols_ref[0, :, BRANCH_W:]
    ext_ref[SUBLANES:, :] = xa
    xc = ext_ref[pl.ds(SUBLANES - 3, tc), :] * cw_ref[0:1, :]
    xc = xc + ext_ref[pl.ds(SUBLANES - 2, tc), :] * cw_ref[1:2, :]
    xc = xc + ext_ref[pl.ds(SUBLANES - 1, tc), :] * cw_ref[2:3, :]
    xc = xc + xa * cw_ref[3:4, :] + cb_ref[...]
    a, bx = _lru_gates(xc, ga, wr_ref[...], br_ref[...], wi_ref[...], bi_ref[...], lam_ref[...])
    sub = lax.broadcasted_iota(I32, (tc, BRANCH_W), 0) % SUBLANES
    d = 1
    while d < SUBLANES:
        keep = sub >= d
        a_s = jnp.where(keep, pltpu.roll(a, d, 0), 1.0)
        b_s = jnp.where(keep, pltpu.roll(bx, d, 0), 0.0)
        bx = a * b_s + bx
        a = a * a_s
        d *= 2
    carry = hc_ref[...]
    tiles = []
    for g in range(tc // SUBLANES):
        rows = slice(g * SUBLANES, (g + 1) * SUBLANES)
        hg = a[rows] * carry + bx[rows]
        carry = hg[SUBLANES - 1:SUBLANES, :]
        tiles.append(hg)
    h = jnp.concatenate(tiles, axis=0)
    hc_ref[...] = carry
    y_ref[0] = h * _gelu_tanh(ga)
    conv_ref[0] = ext_ref[tc + SUBLANES - 3:tc + SUBLANES, :]
    h_ref[0] = h[tc - 1:tc, :]


def _lru_seq(cols, lw, tc):
    b, t, _ = cols.shape
    vec = _const((1, BRANCH_W))
    mat = _const((BRANCH_W, BRANCH_W))
    return pl.pallas_call(
        functools.partial(_lru_seq_kernel, tc=tc),
        grid=(b, t // tc),
        in_specs=[pl.BlockSpec((1, tc, 2 * BRANCH_W), lambda i, j: (i, j, 0)),
                  _const((LRU_CONV, BRANCH_W)), vec, mat, vec, mat, vec, vec],
        out_specs=[pl.BlockSpec((1, tc, BRANCH_W), lambda i, j: (i, j, 0)),
                   pl.BlockSpec((1, LRU_CONV - 1, BRANCH_W), lambda i, j: (i, 0, 0)),
                   pl.BlockSpec((1, 1, BRANCH_W), lambda i, j: (i, 0, 0))],
        out_shape=[jax.ShapeDtypeStruct((b, t, BRANCH_W), F32),
                   jax.ShapeDtypeStruct((b, LRU_CONV - 1, BRANCH_W), F32),
                   jax.ShapeDtypeStruct((b, 1, BRANCH_W), F32)],
        scratch_shapes=[pltpu.VMEM((tc + SUBLANES, BRANCH_W), F32), pltpu.VMEM((1, BRANCH_W), F32)],
        compiler_params=_params(2),
        name="lru_seq",
    )(cols, lw["conv_w"], lw["conv_b"], lw["w_r"], lw["b_r"], lw["w_i"], lw["b_i"], lw["lam"])


def _lru_step_kernel(cols_ref, buf_ref, h0_ref, cw_ref, cb_ref, wr_ref, br_ref, wi_ref, bi_ref, lam_ref,
                     y_ref, conv_ref, h_ref):
    xa = cols_ref[:, 0:BRANCH_W]
    ga = cols_ref[:, BRANCH_W:]
    xc = buf_ref[:, 0:BRANCH_W] * cw_ref[0:1, :]
    xc = xc + buf_ref[:, BRANCH_W:2 * BRANCH_W] * cw_ref[1:2, :]
    xc = xc + buf_ref[:, 2 * BRANCH_W:] * cw_ref[2:3, :]
    xc = xc + xa * cw_ref[3:4, :] + cb_ref[...]
    a, bx = _lru_gates(xc, ga, wr_ref[...], br_ref[...], wi_ref[...], bi_ref[...], lam_ref[...])
    h = a * h0_ref[...] + bx
    y_ref[...] = h * _gelu_tanh(ga)
    conv_ref[:, 0:2 * BRANCH_W] = buf_ref[:, BRANCH_W:]
    conv_ref[:, 2 * BRANCH_W:] = xa
    h_ref[...] = h


def _lru_step(cols, buf, h0, lw):
    n = cols.shape[0]
    full = lambda w: pl.BlockSpec((n, w), lambda i: (0, 0))
    vec = _const((1, BRANCH_W))
    mat = _const((BRANCH_W, BRANCH_W))
    return pl.pallas_call(
        _lru_step_kernel,
        grid=(1,),
        in_specs=[full(2 * BRANCH_W), full(3 * BRANCH_W), full(BRANCH_W),
                  _const((LRU_CONV, BRANCH_W)), vec, mat, vec, mat, vec, vec],
        out_specs=[full(BRANCH_W), full(3 * BRANCH_W), full(BRANCH_W)],
        out_shape=[jax.ShapeDtypeStruct((n, BRANCH_W), F32), jax.ShapeDtypeStruct((n, 3 * BRANCH_W), F32),
                   jax.ShapeDtypeStruct((n, BRANCH_W), F32)],
        compiler_params=_params(1),
        name="lru_step",
    )(cols, buf, h0, lw["conv_w"], lw["conv_b"], lw["w_r"], lw["b_r"], lw["w_i"], lw["b_i"], lw["lam"])


def _rwkv_token_math(cols, prev, mu, w0, w2p, a0, a2p, g2, k_k, k_a, r_k, ones_bd, head_sel=None):
    mixed = cols + (prev - cols) * mu
    r = mixed[:, 0:BRANCH_W]
    k = mixed[:, BRANCH_W:2 * BRANCH_W]
    v = mixed[:, 2 * BRANCH_W:3 * BRANCH_W]
    la = mixed[:, 3 * BRANCH_W:3 * BRANCH_W + LORA_W]
    gl = mixed[:, 3 * BRANCH_W + LORA_W:]
    w = -_softplus(-(w0 + _dot(jnp.tanh(la).astype(BF16), w2p))) - 0.5
    decay = jnp.exp(-jnp.exp(w))
    a = jax.nn.sigmoid(a0 + _dot(la.astype(BF16), a2p))
    g = _dot(jax.nn.sigmoid(gl).astype(BF16), g2)
    kk = k * k_k
    k_mod = k * (1.0 + (a - 1.0) * k_a)
    bonus = _head_sum(r * k_mod * r_k, ones_bd) * v
    if head_sel is None:
        kk = kk / jnp.maximum(jnp.sqrt(_head_sum(kk * kk, ones_bd)), 1e-12)
        return r, decay, k_mod, v, -kk, kk * a, g, bonus
    inv_norm = 1.0 / jnp.maximum(jnp.sqrt(_head_sum(kk * kk, head_sel)), 1e-12)
    return r, decay, k, v, a, g, bonus, inv_norm


_RW_PARAM_ORDER = ("mu", "w0", "w2p", "a0", "a2p", "g2", "k_k", "k_a", "r_k", "ones_bd")


def _rw_param_specs():
    vec = _const((1, BRANCH_W))
    return [_const((1, RWKV_COLS)), vec, _const((LORA_W, BRANCH_W)), vec, _const((LORA_W, BRANCH_W)),
            _const((GATE_LORA, BRANCH_W)), vec, vec, vec, _const((BRANCH_W, BRANCH_W))]


def _rwkv_pre_seq_kernel(cols_ref, *refs, tc):
    prm = [r[...] for r in refs[:11]]
    outs = refs[11:19]
    ext_ref = refs[19]
    t = pl.program_id(1)

    @pl.when(t == 0)
    def _():
        ext_ref[0:SUBLANES, :] = jnp.zeros((SUBLANES, RWKV_COLS), F32)

    @pl.when(t > 0)
    def _():
        ext_ref[0:SUBLANES, :] = ext_ref[tc:tc + SUBLANES, :]

    cols = cols_ref[0]
    ext_ref[SUBLANES:, :] = cols
    prev = ext_ref[pl.ds(SUBLANES - 1, tc), :]
    for ref, val in zip(outs, _rwkv_token_math(cols, prev, *prm)):
        ref[0] = val


def _rwkv_pre_seq(cols, rw, tc):
    b, t, _ = cols.shape
    blk = pl.BlockSpec((1, tc, BRANCH_W), lambda i, j: (i, j, 0))
    return pl.pallas_call(
        functools.partial(_rwkv_pre_seq_kernel, tc=tc),
        grid=(b, t // tc),
        in_specs=[pl.BlockSpec((1, tc, RWKV_COLS), lambda i, j: (i, j, 0))] + _rw_param_specs()
                 + [_const((BRANCH_W, LANES))],
        out_specs=[blk] * 7 + [pl.BlockSpec((1, tc, LANES), lambda i, j: (i, j, 0))],
        out_shape=[jax.ShapeDtypeStruct((b, t, BRANCH_W), F32)] * 7 + [jax.ShapeDtypeStruct((b, t, LANES), F32)],
        scratch_shapes=[pltpu.VMEM((tc + SUBLANES, RWKV_COLS), F32)],
        compiler_params=_params(2),
        name="rwkv_pre_seq",
    )(cols, *[rw[k] for k in _RW_PARAM_ORDER], rw["head_sel"])


def _rwkv_pre_step_kernel(cols_ref, prev_ref, *refs):
    prm = [r[...] for r in refs[:10]]
    for ref, val in zip(refs[10:18], _rwkv_token_math(cols_ref[...], prev_ref[...], *prm)):
        ref[...] = val


def _rwkv_pre_step(cols, prev, rw):
    n = cols.shape[0]
    full = lambda w: pl.BlockSpec((n, w), lambda i: (0, 0))
    return pl.pallas_call(
        _rwkv_pre_step_kernel,
        grid=(1,),
        in_specs=[full(RWKV_COLS), full(RWKV_COLS)] + _rw_param_specs(),
        out_specs=[full(BRANCH_W)] * 8,
        out_shape=[jax.ShapeDtypeStruct((n, BRANCH_W), F32)] * 8,
        compiler_params=_params(1),
        name="rwkv_pre_step",
    )(cols, prev, *[rw[k] for k in _RW_PARAM_ORDER])


SCAN_GROUPS = 4
SCAN_GROUP_LANES = LANES // SCAN_GROUPS
SCAN_KEY_ROWS = RWKV_HEAD // SCAN_GROUPS


def _rwkv_scan_kernel(r_ref, w_ref, k_ref, g_ref, n_ref, kkc_ref, kac_ref, v_ref, y_ref, so_ref, s_ref, exp_ref,
                      pk_ref, *, tc, ni):
    t = pl.program_id(0)

    @pl.when(t == 0)
    def _():
        s_ref[...] = jnp.zeros_like(s_ref)

    k_raw = k_ref[...]
    rate = g_ref[...]
    kk = k_raw * kkc_ref[...] * n_ref[...]
    pk_ref[0] = -kk
    pk_ref[1] = kk * rate
    pk_ref[2] = k_raw * (1.0 + (rate - 1.0) * kac_ref[...])

    def unpack(slot, tt):
        tiles = (pk_ref[0, tt], pk_ref[1, tt], w_ref[tt], pk_ref[2, tt], r_ref[tt])
        for idx, x in enumerate(tiles):
            for g in range(SCAN_GROUPS):
                xg = x if g == 0 else pltpu.roll(x, LANES - SCAN_GROUP_LANES * g, 1)
                exp_ref[idx, slot, g * SCAN_KEY_ROWS:(g + 1) * SCAN_KEY_ROWS, :] = xg

    unpack(0, 0)

    n_acc = 4

    def fold(accs):
        return (accs[0] + accs[1]) + (accs[2] + accs[3])

    def step(tt, slot):
        unpack(1 - slot, jnp.minimum(tt + 1, tc - 1))
        row = lambda kk, j: exp_ref[kk, slot, pl.ds(j, 1), :]
        groups = range(ni // SUBLANES)
        acc = [[None] * n_acc for _ in groups]
        for j in range(RWKV_HEAD):
            a = row(0, j)
            for g in groups:
                term = s_ref[g, j] * a
                acc[g][j % n_acc] = term if j < n_acc else acc[g][j % n_acc] + term
        sa = [fold(acc[g]) for g in groups]
        vg = [v_ref[tt, g * SUBLANES:(g + 1) * SUBLANES, :] for g in groups]
        acc = [[None] * n_acc for _ in groups]
        for j in range(RWKV_HEAD):
            b, w, k, r = row(1, j), row(2, j), row(3, j), row(4, j)
            for g in groups:
                s = s_ref[g, j] * w + sa[g] * b + vg[g] * k
                s_ref[g, j] = s
                term = s * r
                acc[g][j % n_acc] = term if j < n_acc else acc[g][j % n_acc] + term
        for g in groups:
            y_ref[tt, g * SUBLANES:(g + 1) * SUBLANES, :] = fold(acc[g])

    def two_steps(pair, carry):
        step(2 * pair, 0)
        step(2 * pair + 1, 1)
        return carry

    lax.fori_loop(0, tc // 2, two_steps, 0)

    @pl.when(t == pl.num_programs(0) - 1)
    def _():
        so_ref[...] = s_ref[...]


def _rwkv_scan(r, w, k, rate, inv_norm, kk_c, ka_c, v, tc):
    t = r.shape[0]
    ni = v.shape[1]
    vec = pl.BlockSpec((tc, SCAN_KEY_ROWS, LANES), lambda j: (j, 0, 0))
    row = pl.BlockSpec((tc, ni, LANES), lambda j: (j, 0, 0))
    st = pl.BlockSpec((ni // SUBLANES, RWKV_HEAD, SUBLANES, LANES), lambda j: (0, 0, 0, 0))
    par = _const((SCAN_KEY_ROWS, LANES))
    return pl.pallas_call(
        functools.partial(_rwkv_scan_kernel, tc=tc, ni=ni),
        grid=(t // tc,),
        in_specs=[vec] * 4 + [pl.BlockSpec((tc, 1, LANES), lambda j: (j, 0, 0)), par, par, row],
        out_specs=[row, st],
        out_shape=[jax.ShapeDtypeStruct((t, ni, LANES), F32),
                   jax.ShapeDtypeStruct((ni // SUBLANES, RWKV_HEAD, SUBLANES, LANES), F32)],
        scratch_shapes=[pltpu.VMEM((ni // SUBLANES, RWKV_HEAD, SUBLANES, LANES), F32),
                        pltpu.VMEM((5, 2, RWKV_HEAD, LANES), F32),
                        pltpu.VMEM((3, tc, SCAN_KEY_ROWS, LANES), F32)],
        compiler_params=_params(1),
        name="rwkv_scan",
    )(r, w, k, rate, inv_norm, kk_c, ka_c, v)


def _rwkv_step_kernel(s_ref, r_ref, w_ref, k_ref, a_ref, b_ref, v_ref, *refs):
    so_ref, y_ref = refs[-2:]
    a = a_ref[0]
    b = b_ref[0]
    w = w_ref[0]
    k = k_ref[0]
    r = r_ref[0]
    for i in range(RWKV_HEAD):
        s = s_ref[0, i]
        sa = jnp.sum(s * a, axis=0, keepdims=True)
        s = s * w + sa * b + v_ref[0, pl.ds(i, 1), :] * k
        so_ref[0, i] = s
        y_ref[0, pl.ds(i, 1), :] = jnp.sum(s * r, axis=0, keepdims=True)


def _rwkv_step(s, layer, r, w, k, a, b, v, s_out):
    depth, nh, hd, _, n = s.shape
    st = pl.BlockSpec((None, 1, hd, hd, n), lambda i: (layer, i, 0, 0, 0))
    vec = pl.BlockSpec((1, hd, n), lambda i: (i, 0, 0))
    in_specs = [st] + [vec] * 6
    args = [s, r, w, k, a, b, v]
    aliases = {}
    if s_out is not None:
        in_specs.append(pl.BlockSpec(memory_space=pl.ANY))
        args.append(s_out)
        aliases = {7: 0}
    return pl.pallas_call(
        _rwkv_step_kernel,
        grid=(nh,),
        in_specs=in_specs,
        out_specs=[st, vec],
        out_shape=[jax.ShapeDtypeStruct((depth, nh, hd, hd, n), F32), jax.ShapeDtypeStruct((nh, hd, n), F32)],
        input_output_aliases=aliases,
        compiler_params=_params(1),
        name="rwkv_step",
    )(*args)


def _t5_bucket_np(dist):
    n = np.maximum(dist, 0)
    max_exact = N_BUCKETS // 2
    nf = np.maximum(n, 1).astype(np.float32)
    large = max_exact + (np.log(nf / np.float32(max_exact)) / np.float32(math.log(MAX_DISTANCE / max_exact))
                         * np.float32(N_BUCKETS - max_exact)).astype(np.int32)
    large = np.minimum(large, N_BUCKETS - 1)
    return np.where(n < max_exact, n, large).astype(np.int32)


def _moba_seq_kernel(tab_ref, bko_ref, bkp_ref, qt_ref, ka_ref, vt_ref, o_ref,
                     km_ref, bias_ref, qa_ref, m_ref, l_ref, acc_ref, s_ref, *, nblk):
    bi = pl.program_id(0)
    i = pl.program_id(1)
    blk = MOBA_BLOCK
    far_bucket = N_BUCKETS - 1
    sel_rows = 2 * SUBLANES
    log2e = math.log2(math.e)
    scale = ATT_HEAD ** -0.5 * log2e
    ones_rows = jnp.ones((SUBLANES, blk), BF16)

    @pl.when((bi == 0) & (i == 0))
    def _():
        bko = bko_ref[...]
        bkp = bkp_ref[...]
        for h in range(ATT_HEADS):
            c = tab_ref[far_bucket, h]
            own = jnp.zeros((blk, blk), F32)
            prev = jnp.zeros((blk, blk), F32)
            for j in range(N_BUCKETS):
                val = (tab_ref[j, h] - c) * log2e
                own = jnp.where(bko == j, val, own)
                prev = jnp.where(bkp == j, val, prev)
            bias_ref[h, 0] = jnp.where(bko < 0, NEG, own)
            bias_ref[h, 1] = prev

    @pl.when(i == 0)
    def _():
        for h in range(ATT_HEADS):
            km_ref[h] = jnp.zeros((sel_rows, ATT_HEAD), F32)
            for n in range(nblk):
                ks = jnp.sum(ka_ref[0, h, n * blk:(n + 1) * blk, :].astype(F32), axis=0, keepdims=True)
                km_ref[h, n:n + 1, :] = ks[:, 0:ATT_HEAD] * (1.0 / blk)

    row = lax.broadcasted_iota(I32, (sel_rows, blk), 0)
    row_f = row.astype(F32)
    for h in range(ATT_HEADS):
        qt = qt_ref[0, h * ATT_HEAD:(h + 1) * ATT_HEAD, :]
        bs = jnp.dot(km_ref[h], qt, precision=lax.Precision.HIGHEST, preferred_element_type=F32)
        work = jnp.where(row < i, bs, -jnp.inf)
        sel = row == i
        for _ in range(MOBA_TOPK):
            mx = jnp.max(work, axis=0, keepdims=True)
            is_m = (work == mx) & (work > -jnp.inf)
            idx = jnp.min(jnp.where(is_m, row_f, 4.0 * LANES), axis=0, keepdims=True)
            pick = row_f == idx
            sel = sel | pick
            work = jnp.where(pick, -jnp.inf, work)
        selb = jnp.where(sel, 0.0, NEG)
        pad = jnp.zeros((LANES - ATT_HEAD - sel_rows, blk), F32)
        qa_ref[h] = jnp.concatenate([qt * scale, selb, pad], axis=0).astype(BF16)

    key_rows = lambda n: pl.ds(pl.multiple_of(n * blk, blk), blk)

    def block_step(n, kind):
        for h in range(ATT_HEADS):
            s_ref[h] = _dot(ka_ref[0, h, key_rows(n), :], qa_ref[h])
        for h in range(ATT_HEADS):
            s = s_ref[h]
            vt = vt_ref[0, h * ATT_HEAD:(h + 1) * ATT_HEAD, key_rows(n)].astype(BF16)
            if kind == "own":
                s = s + bias_ref[h, 0]
                m_new = jnp.max(s, axis=0, keepdims=True)
                pb = jnp.exp2(s - m_new).astype(BF16)
                l_ref[h] = _dot(ones_rows, pb)[0:1, :]
                acc_ref[h] = _dot(vt, pb)
            else:
                if kind == "prev":
                    s = s + bias_ref[h, 1] + jnp.where(i >= 1, 0.0, NEG)
                m_old = m_ref[h]
                m_new = jnp.maximum(m_old, jnp.max(s, axis=0, keepdims=True))
                alpha = jnp.exp2(m_old - m_new)
                pb = jnp.exp2(s - m_new).astype(BF16)
                l_ref[h] = alpha * l_ref[h] + _dot(ones_rows, pb)[0:1, :]
                acc_ref[h] = alpha * acc_ref[h] + _dot(vt, pb)
            m_ref[h] = m_new

    block_step(i, "own")
    block_step(jnp.maximum(i - 1, 0), "prev")

    def far(n, carry):
        block_step(n, "far")
        return carry

    lax.fori_loop(0, jnp.maximum(i - 1, 0), far, 0)
    for h in range(ATT_HEADS):
        o_ref[0, h * ATT_HEAD:(h + 1) * ATT_HEAD, :] = acc_ref[h] / l_ref[h]


def _moba_seq(q_t, k_aug, v_buf, layer, rel_bias):
    b, nh, t, _ = k_aug.shape
    blk = MOBA_BLOCK
    nblk = t // blk
    assert nblk <= 2 * SUBLANES
    d = np.arange(blk)[None, :] - np.arange(blk)[:, None]
    bko = jnp.asarray(np.where(d >= 0, _t5_bucket_np(d), -1).astype(np.int32))
    bkp = jnp.asarray(_t5_bucket_np(d + blk))
    q_blk = pl.BlockSpec((1, BRANCH_W, blk), lambda i, j: (i, 0, j))
    return pl.pallas_call(
        functools.partial(_moba_seq_kernel, nblk=nblk),
        grid=(b, nblk),
        in_specs=[pl.BlockSpec(memory_space=pltpu.SMEM), _const((blk, blk)), _const((blk, blk)), q_blk,
                  pl.BlockSpec((1, nh, t, LANES), lambda i, j: (i, 0, 0, 0), pipeline_mode=pl.Buffered(1)),
                  pl.BlockSpec((None, 1, BRANCH_W, t), lambda i, j: (layer, i, 0, 0), pipeline_mode=pl.Buffered(1))],
        out_specs=q_blk,
        out_shape=jax.ShapeDtypeStruct((b, BRANCH_W, t), F32),
        scratch_shapes=[pltpu.VMEM((nh, 2 * SUBLANES, ATT_HEAD), F32), pltpu.VMEM((nh, 2, blk, blk), F32),
                        pltpu.VMEM((nh, LANES, blk), BF16), pltpu.VMEM((nh, 1, blk), F32),
                        pltpu.VMEM((nh, 1, blk), F32), pltpu.VMEM((nh, ATT_HEAD, blk), F32),
                        pltpu.VMEM((nh, blk, blk), F32)],
        compiler_params=_params(2),
        name="moba_seq",
    )(rel_bias, bko, bkp, q_t, k_aug, v_buf)


def _moba_step_kernel(pt_ref, q_ref, kn_ref, vn_ref, relt_ref, bkt_ref, *refs, n_pages):
    kp = refs[:n_pages]
    vp = refs[n_pages:2 * n_pages]
    o_ref, acc_ref = refs[2 * n_pages:2 * n_pages + 2]
    pages_per_block = MOBA_BLOCK // PAGE_SIZE
    n_blocks = n_pages // pages_per_block
    scale = ATT_HEAD ** -0.5
    heads = range(ATT_HEADS)

    def per_head(fn):
        return jnp.concatenate([fn(h) for h in heads], axis=0)

    relt = relt_ref[...]
    c_far = relt[:, N_BUCKETS - 1:N_BUCKETS]
    bkt = bkt_ref[...]
    bias_last = jnp.zeros((ATT_HEADS, PAGE_SIZE), F32)
    for j in range(N_BUCKETS):
        bias_last = jnp.where(bkt == j, relt[:, j:j + 1] - c_far, bias_last)

    q = q_ref[0]
    qb = [jnp.broadcast_to(q[h], (ATT_HEAD, PAGE_SIZE)) for h in heads]
    logit, rowsum = [], []
    for p in range(n_pages):
        raw = per_head(lambda h: jnp.sum(kp[p][0, 0, h] * qb[h], axis=0, keepdims=True))
        rowsum.append(jnp.sum(raw, axis=1, keepdims=True))
        lg = raw * scale
        logit.append(lg + bias_last if p == n_pages - 1 else lg)

    sc = []
    for n in range(n_blocks):
        tot = rowsum[n * pages_per_block]
        for j in range(1, pages_per_block):
            tot = tot + rowsum[n * pages_per_block + j]
        sc.append(tot * (1.0 / MOBA_BLOCK))
    sel = []
    for n in range(n_blocks):
        rank = jnp.zeros((ATT_HEADS, 1), I32)
        for j in range(n_blocks):
            if j != n:
                ahead = sc[j] > sc[n]
                if j < n:
                    ahead = ahead | (sc[j] == sc[n])
                rank = rank + ahead.astype(I32)
        sel.append(rank < MOBA_TOPK)

    s_self = per_head(lambda h: jnp.sum(q[h] * kn_ref[0, h], axis=0, keepdims=True)) * scale + (relt[:, 0:1] - c_far)
    m_all = s_self
    for p in range(n_pages):
        m_all = jnp.maximum(m_all, jnp.where(sel[p // pages_per_block], jnp.max(logit[p], axis=1, keepdims=True), -jnp.inf))
    w_self = jnp.exp(s_self - m_all)
    l_all = w_self
    acc_ref[...] = jnp.zeros_like(acc_ref)
    for p in range(n_pages):
        e = jnp.where(sel[p // pages_per_block], jnp.exp(logit[p] - m_all), 0.0)
        l_all = l_all + jnp.sum(e, axis=1, keepdims=True)
        for h in heads:
            acc_ref[h] += vp[p][0, 0, h] * e[h:h + 1, :]
    for h in heads:
        out = jnp.sum(acc_ref[h], axis=1, keepdims=True) + w_self[h:h + 1, :] * vn_ref[0, h]
        o_ref[0, h] = out / l_all[h:h + 1, :]


def _moba_step(q, k_new, v_new, cache_k, cache_v, layer, page_table, rel_bias):
    n, n_pages = page_table.shape
    col = lambda z: z.reshape(n, ATT_HEADS, ATT_HEAD, 1)
    col_spec = pl.BlockSpec((1, ATT_HEADS, ATT_HEAD, 1), lambda i, pt: (i, 0, 0, 0))
    relt = jnp.pad(rel_bias.T, ((0, 0), (0, LANES - N_BUCKETS)))
    bkt = jnp.asarray(_t5_bucket_np(PAGE_SIZE - np.arange(PAGE_SIZE))[None, :])

    def page_spec(p):
        return pl.BlockSpec((1, 1, ATT_HEADS, ATT_HEAD, PAGE_SIZE),
                            lambda i, pt: (layer, pt[i * n_pages + p], 0, 0, 0))

    grid_spec = pltpu.PrefetchScalarGridSpec(
        num_scalar_prefetch=1,
        grid=(n,),
        in_specs=[col_spec, col_spec, col_spec,
                  pl.BlockSpec((ATT_HEADS, LANES), lambda i, pt: (0, 0)),
                  pl.BlockSpec((1, PAGE_SIZE), lambda i, pt: (0, 0))]
                 + [page_spec(p) for p in range(n_pages)] * 2,
        out_specs=col_spec,
        scratch_shapes=[pltpu.VMEM((ATT_HEADS, ATT_HEAD, PAGE_SIZE), F32)],
    )
    rows_last = lambda c: c.transpose(0, 1, 3, 4, 2)
    out = pl.pallas_call(
        functools.partial(_moba_step_kernel, n_pages=n_pages),
        grid_spec=grid_spec,
        out_shape=jax.ShapeDtypeStruct((n, ATT_HEADS, ATT_HEAD, 1), F32),
        compiler_params=_params(1),
        name="moba_step",
    )(page_table.reshape(-1), col(q), col(k_new), col(v_new), relt, bkt,
      *([rows_last(cache_k)] * n_pages), *([rows_last(cache_v)] * n_pages))
    return out.reshape(n, BRANCH_W)


def _merge_kernel(x_ref, ya_ref, yr_ref, bon_ref, g_ref, yc_ref, gate_ref,
                  lnw_ref, lnb_ref, ones_ref, wb_ref, wo_ref, o_ref, *, yc_transposed):
    ones_bd = ones_ref[...]
    y = yr_ref[...]
    mu = _head_sum(y, ones_bd) * (1.0 / RWKV_HEAD)
    d = y - mu
    var = _head_sum(d * d, ones_bd) * (1.0 / RWKV_HEAD)
    yb = (d * lax.rsqrt(var + RWKV_GN_EPS) * lnw_ref[...] + lnb_ref[...] + bon_ref[...]) * g_ref[...]
    yc = yc_ref[0].T if yc_transposed else yc_ref[...]
    merged = None
    for j, yj in enumerate((ya_ref[...], yb, yc)):
        gate = jax.nn.sigmoid(gate_ref[:, j * D_MODEL:(j + 1) * D_MODEL])
        term = _dot(yj.astype(BF16), wb_ref[j]) * gate
        merged = term if merged is None else merged + term
    o_ref[...] = x_ref[...] + _dot(merged.astype(BF16), wo_ref[...])


def _merge(x, ya, yr, bonus, g, yc, gates, mw, tm):
    m = x.shape[0]
    tok = lambda w: pl.BlockSpec((tm, w), lambda i: (i, 0))
    vec = _const((1, BRANCH_W))
    yc_transposed = yc.ndim == 3
    if yc_transposed:
        nblk = yc.shape[2] // tm
        yc_spec = pl.BlockSpec((1, BRANCH_W, tm), lambda i: (i // nblk, 0, i % nblk))
    else:
        yc_spec = tok(BRANCH_W)
    return pl.pallas_call(
        functools.partial(_merge_kernel, yc_transposed=yc_transposed),
        grid=(m // tm,),
        in_specs=[tok(D_MODEL)] + [tok(BRANCH_W)] * 4 + [yc_spec, tok(N_BRANCH * D_MODEL), vec, vec,
                  _const((BRANCH_W, BRANCH_W)), _const((N_BRANCH, BRANCH_W, D_MODEL)), _const((D_MODEL, D_MODEL))],
        out_specs=tok(D_MODEL),
        out_shape=jax.ShapeDtypeStruct((m, D_MODEL), F32),
        compiler_params=_params(1),
        name="merge",
    )(x, ya, yr, bonus, g, yc, gates, mw["ln_w"], mw["ln_b"], mw["ones_bd"], mw["w_branch"], mw["w_out"])


FFN_CHUNK = 512


def _ffn_kernel(*refs, seq_mode, final_norm, tm, tiles_per_seq):
    (x_ref, p_ref, ln2_ref, wup_ref, fcw_ref, fcb_ref, wdn_ref, ln3_ref, pg_ref, pp_ref, lnf_ref) = refs[:11]
    if seq_mode:
        o_ref, fc_ref, ext_ref = refs[11:14]
        i = pl.program_id(0)

        @pl.when(i % tiles_per_seq == 0)
        def _():
            ext_ref[0:SUBLANES, :] = jnp.zeros((SUBLANES, 2 * D_FF), F32)

        @pl.when(i % tiles_per_seq != 0)
        def _():
            ext_ref[0:SUBLANES, :] = ext_ref[tm:tm + SUBLANES, :]
    else:
        prev_ref, o_ref, u_ref = refs[11:14]

    x = x_ref[...]
    hb = _rms(x, ln2_ref[...]).astype(BF16)

    def up_cols(lo, hi):
        u = _dot(hb, wup_ref[:, lo:hi])
        if seq_mode:
            ext_ref[SUBLANES:, lo:hi] = u
        else:
            u_ref[:, lo:hi] = u

    def conv_cols(lo, hi):
        if seq_mode:
            u = ext_ref[SUBLANES:, lo:hi]
            u2 = ext_ref[pl.ds(SUBLANES - 2, tm), lo:hi]
            u1 = ext_ref[pl.ds(SUBLANES - 1, tm), lo:hi]
        else:
            u = u_ref[:, lo:hi]
            u2 = prev_ref[:, lo:hi]
            u1 = prev_ref[:, 2 * D_FF + lo:2 * D_FF + hi]
        return u2 * fcw_ref[0:1, lo:hi] + u1 * fcw_ref[1:2, lo:hi] + u * fcw_ref[2:3, lo:hi] + fcb_ref[:, lo:hi]

    def up_chunk(c):
        up_cols(c * FFN_CHUNK, (c + 1) * FFN_CHUNK)
        up_cols(D_FF + c * FFN_CHUNK, D_FF + (c + 1) * FFN_CHUNK)

    n_chunks = D_FF // FFN_CHUNK
    up_chunk(0)
    acc = None
    for c in range(n_chunks):
        if c + 1 < n_chunks:
            up_chunk(c + 1)
        lo, hi = c * FFN_CHUNK, (c + 1) * FFN_CHUNK
        act = _gelu_tanh(conv_cols(lo, hi)) * conv_cols(D_FF + lo, D_FF + hi)
        term = _dot(act.astype(BF16), wdn_ref[lo:hi, :])
        acc = term if acc is None else acc + term
    x = x + acc
    x = x + jax.nn.sigmoid(_dot(_rms(x, ln3_ref[...]).astype(BF16), pg_ref[...])) * _dot(p_ref[...].astype(BF16), pp_ref[...])
    o_ref[...] = _rms(x, lnf_ref[...]) if final_norm else x
    if seq_mode:
        fc_ref[0] = ext_ref[tm + SUBLANES - 2:tm + SUBLANES, :]


def _ffn(x, p, fw, tm, seq_len, prev=None, final_norm=False):
    m = x.shape[0]
    seq_mode = seq_len is not None
    tok = lambda w: pl.BlockSpec((tm, w), lambda i: (i, 0))
    vecd = _const((1, D_MODEL))
    in_specs = [tok(D_MODEL), tok(PLE_DIM), vecd, _const((D_MODEL, 2 * D_FF)), _const((FFN_CONV, 2 * D_FF)),
                _const((1, 2 * D_FF)), _const((D_FF, D_MODEL)), vecd, _const((D_MODEL, D_MODEL)),
                _const((PLE_DIM, D_MODEL)), vecd]
    args = [x, p, fw["ln2"], fw["ffn_up"], fw["conv_w"], fw["conv_b"], fw["ffn_down"], fw["ln3"],
            fw["ple_gate"], fw["ple_proj"], fw["ln_f"]]
    if seq_mode:
        tiles_per_seq = seq_len // tm
        out_specs = [tok(D_MODEL), pl.BlockSpec((1, FFN_CONV - 1, 2 * D_FF), lambda i: (i // tiles_per_seq, 0, 0))]
        out_shape = [jax.ShapeDtypeStruct((m, D_MODEL), F32),
                     jax.ShapeDtypeStruct((m // seq_len, FFN_CONV - 1, 2 * D_FF), F32)]
        scratch = [pltpu.VMEM((tm + SUBLANES, 2 * D_FF), F32)]
    else:
        tiles_per_seq = 1
        in_specs.append(tok((FFN_CONV - 1) * 2 * D_FF))
        args.append(prev)
        out_specs = [tok(D_MODEL), tok(2 * D_FF)]
        out_shape = [jax.ShapeDtypeStruct((m, D_MODEL), F32), jax.ShapeDtypeStruct((m, 2 * D_FF), F32)]
        scratch = []
    return pl.pallas_call(
        functools.partial(_ffn_kernel, seq_mode=seq_mode, final_norm=final_norm, tm=tm, tiles_per_seq=tiles_per_seq),
        grid=(m // tm,),
        in_specs=in_specs,
        out_specs=out_specs,
        out_shape=out_shape,
        scratch_shapes=scratch,
        compiler_params=_params(1),
        name="ffn",
    )(*args)


def _block_diag(w):
    g, n, _ = w.shape
    eye = jnp.eye(g, dtype=w.dtype)
    return (eye[:, None, :, None] * w[:, :, None, :]).reshape(g * n, g * n)


def _layer_weights(i, wt):
    row = lambda v: v.reshape(1, -1)
    ones_bd = _block_diag(jnp.ones((RWKV_HEADS, RWKV_HEAD, RWKV_HEAD), BF16))
    zeros_lora = jnp.zeros((DECAY_LORA, BRANCH_W), BF16)
    lw = {"conv_w": wt["lru_conv_w"][i], "conv_b": row(wt["lru_conv_b"][i]),
          "w_r": _block_diag(wt["lru_w_r"][i]).astype(BF16), "b_r": row(wt["lru_b_r"][i]),
          "w_i": _block_diag(wt["lru_w_i"][i]).astype(BF16), "b_i": row(wt["lru_b_i"][i]),
          "lam": row(wt["lru_lambda"][i])}
    rw = {"mu": row(wt["rwkv_mu"][i]), "w0": row(wt["rwkv_w0"][i]),
          "w2p": jnp.concatenate([wt["rwkv_w2"][i].astype(BF16), zeros_lora], axis=0),
          "a0": row(wt["rwkv_a0"][i]),
          "a2p": jnp.concatenate([zeros_lora, wt["rwkv_a2"][i].astype(BF16)], axis=0),
          "g2": wt["rwkv_g2"][i].astype(BF16), "k_k": row(wt["rwkv_k_k"][i]), "k_a": row(wt["rwkv_k_a"][i]),
          "r_k": row(wt["rwkv_r_k"][i]), "ones_bd": ones_bd,
          "head_sel": (jnp.arange(BRANCH_W)[:, None] // RWKV_HEAD == jnp.arange(LANES)[None, :]).astype(BF16)}
    mw = {"ln_w": row(wt["rwkv_ln_w"][i]), "ln_b": row(wt["rwkv_ln_b"][i]), "ones_bd": ones_bd,
          "w_branch": wt["w_branch"][i].astype(BF16), "w_out": wt["w_out"][i].astype(BF16)}
    fw = {"ln2": row(wt["ln2"][i]), "ffn_up": wt["ffn_up"][i].astype(BF16), "conv_w": wt["ffn_conv_w"][i],
          "conv_b": row(wt["ffn_conv_b"][i]), "ffn_down": wt["ffn_down"][i].astype(BF16),
          "ln3": row(wt["ln3"][i]), "ple_gate": wt["ple_gate"][i].astype(BF16),
          "ple_proj": wt["ple_proj"][i].astype(BF16), "ln_f": row(wt["ln_f"])}
    w_in = wt["w_in"][i].astype(BF16)
    seg = lambda j: w_in[:, _IN_SEGS[j][0]:_IN_SEGS[j][1]]
    wk_heads = seg(3).reshape(D_MODEL, ATT_HEADS, ATT_HEAD).transpose(1, 0, 2)
    pw = {"w_in": w_in, "w_qkv_t": jnp.stack([seg(2).T, seg(3).T, seg(4).T]),
          "wk_aug": jnp.pad(wk_heads, ((0, 0), (0, 0), (0, LANES - ATT_HEAD)))}
    return {"ln1": row(wt["ln1"][i]), "w_in": w_in, "proj": pw, "lru": lw, "rwkv": rw, "merge": mw, "ffn": fw}


def _prompt_layer(x, p, lp, rel_bias, final_norm, layer=0, depth=1, kv_bufs=None):
    b, t, _ = x.shape
    m = b * t
    nh, hd = RWKV_HEADS, RWKV_HEAD
    lru_c, rw_c, gates, q_t, k_buf, v_buf, k_aug = _in_proj_seq(x.reshape(m, D_MODEL), lp["ln1"], lp["proj"], b, t,
                                                                 layer, depth, kv_bufs)

    ya, conv_new, h_new = _lru_seq(lru_c.reshape(b, t, 2 * BRANCH_W), lp["lru"], tc=512)

    r_, w_, k_, v_, rate, g_, bonus, inv_n = _rwkv_pre_seq(rw_c.reshape(b, t, RWKV_COLS), lp["rwkv"], tc=512)
    grp, kr = SCAN_GROUPS, SCAN_KEY_ROWS
    assert b * nh * grp == LANES

    def key_packed(z):
        return z.reshape(b, t, nh, grp, kr).transpose(1, 4, 3, 0, 2).reshape(t, kr, LANES)

    def row_major(z):
        return z.reshape(b, t, nh, hd // grp, grp).transpose(1, 3, 4, 0, 2).reshape(t, hd // grp, LANES)

    def param_packed(z):
        z = z.reshape(nh, grp, kr).transpose(2, 1, 0)
        return jnp.broadcast_to(z[:, :, None, :], (kr, grp, b, nh)).reshape(kr, LANES)

    inv_n = inv_n[:, :, :nh].transpose(1, 0, 2).reshape(t, 1, b * nh)
    inv_n = jnp.broadcast_to(inv_n, (t, grp, b * nh)).reshape(t, 1, LANES)
    y_l, s_l = _rwkv_scan(key_packed(r_), key_packed(w_), key_packed(k_), key_packed(rate), inv_n,
                          param_packed(lp["rwkv"]["k_k"]), param_packed(lp["rwkv"]["k_a"]), row_major(v_), tc=64)
    yr = y_l.reshape(t, hd // grp, grp, b, nh).transpose(3, 0, 4, 1, 2).reshape(m, BRANCH_W)
    s_l = s_l.reshape(-1, grp, kr, SUBLANES, grp, b, nh)
    s_l = jnp.stack([jnp.roll(s_l[:, :, :, :, c], c, axis=1) for c in range(grp)], axis=4)
    s_new = s_l.transpose(5, 6, 0, 3, 4, 1, 2).reshape(b, nh, hd, hd)

    yc_t = _moba_seq(q_t, k_aug, v_buf, layer, rel_bias)

    flat = lambda z: z.reshape(m, BRANCH_W)
    x1 = _merge(x.reshape(m, D_MODEL), flat(ya), yr, flat(bonus), flat(g_), yc_t, gates, lp["merge"], tm=512)
    x2, fc_new = _ffn(x1, p.reshape(m, PLE_DIM), lp["ffn"], tm=512, seq_len=t, final_norm=final_norm)
    new = {"kv_bufs": (k_buf, v_buf),
           "lru_h": h_new[:, 0, :], "lru_conv": conv_new, "rwkv": s_new,
           "rwkv_shift": rw_c.reshape(b, t, RWKV_COLS)[:, -1, :], "ffn_conv": fc_new}
    return x2.reshape(b, t, D_MODEL), new


def _sample_layer(x, p, st, lp, rel_bias, final_norm, s_out=None):
    n = x.shape[0]
    lru_c, rw_c, q, k, v, gates = _in_proj(x.reshape(n, D_MODEL), lp["ln1"], lp["w_in"], tm=n)

    ya, conv_new, h_new = _lru_step(lru_c, st["lru_conv"].reshape(n, (LRU_CONV - 1) * BRANCH_W), st["lru_h"], lp["lru"])

    r_, w_, k_, v_, a_, b_, g_, bonus = _rwkv_pre_step(rw_c, st["rwkv_shift"], lp["rwkv"])
    seq_last = lambda z: z.reshape(n, RWKV_HEADS, RWKV_HEAD).transpose(1, 2, 0)
    s_l, y_l = _rwkv_step(st["rwkv"].transpose(0, 2, 3, 4, 1), st["layer"], seq_last(r_), seq_last(w_), seq_last(k_),
                          seq_last(a_), seq_last(b_), seq_last(v_), s_out)
    yr = y_l.transpose(2, 0, 1).reshape(n, BRANCH_W)

    yc = _moba_step(q, k, v, st["cache_k"], st["cache_v"], st["layer"], st["page_table"], rel_bias)

    x1 = _merge(x.reshape(n, D_MODEL), ya, yr, bonus, g_, yc, gates, lp["merge"], tm=n)
    fc_prev = st["ffn_conv"].reshape(n, (FFN_CONV - 1) * 2 * D_FF)
    x2, u = _ffn(x1, p.reshape(n, PLE_DIM), lp["ffn"], tm=n, seq_len=None, prev=fc_prev, final_norm=final_norm)
    new = {"k": k.reshape(n, 1, ATT_HEADS, ATT_HEAD), "v": v.reshape(n, 1, ATT_HEADS, ATT_HEAD),
           "lru_h": h_new, "lru_conv": conv_new.reshape(n, LRU_CONV - 1, BRANCH_W), "rwkv_buf": s_l,
           "rwkv_shift": rw_c, "ffn_conv": jnp.stack([st["ffn_conv"][:, 1, :], u], axis=1)}
    return x2.reshape(n, 1, D_MODEL), new


def kernel(x_prompt, x_sample, cache_k, cache_v, state_lru_h, state_lru_conv, state_rwkv, state_rwkv_shift, state_ffn_conv, page_table, p_prompt, p_sample, ln1, w_in, lru_conv_w, lru_conv_b, lru_w_r, lru_b_r, lru_w_i, lru_b_i, lru_lambda, rwkv_mu, rwkv_w0, rwkv_w2, rwkv_a0, rwkv_a2, rwkv_g2, rwkv_k_k, rwkv_k_a, rwkv_r_k, rwkv_ln_w, rwkv_ln_b, rel_bias, w_branch, w_out, ln2, ffn_up, ffn_conv_w, ffn_conv_b, ffn_down, ln3, ple_gate, ple_proj, ln_f):
    wt = dict(ln1=ln1, w_in=w_in, lru_conv_w=lru_conv_w, lru_conv_b=lru_conv_b, lru_w_r=lru_w_r, lru_b_r=lru_b_r,
              lru_w_i=lru_w_i, lru_b_i=lru_b_i, lru_lambda=lru_lambda, rwkv_mu=rwkv_mu, rwkv_w0=rwkv_w0,
              rwkv_w2=rwkv_w2, rwkv_a0=rwkv_a0, rwkv_a2=rwkv_a2, rwkv_g2=rwkv_g2, rwkv_k_k=rwkv_k_k,
              rwkv_k_a=rwkv_k_a, rwkv_r_k=rwkv_r_k, rwkv_ln_w=rwkv_ln_w, rwkv_ln_b=rwkv_ln_b, w_branch=w_branch,
              w_out=w_out, ln2=ln2, ffn_up=ffn_up, ffn_conv_w=ffn_conv_w, ffn_conv_b=ffn_conv_b, ffn_down=ffn_down,
              ln3=ln3, ple_gate=ple_gate, ple_proj=ple_proj, ln_f=ln_f)
    depth = w_in.shape[0]
    b, t = x_prompt.shape[:2]
    xp, xs = x_prompt, x_sample[:, 0, :]
    outs_p, outs_s = [], []
    kv_bufs, s_buf = None, None
    for i in range(depth):
        lp = _layer_weights(i, wt)
        last = i == depth - 1
        st = {"cache_k": cache_k, "cache_v": cache_v, "layer": i, "page_table": page_table,
              "lru_h": state_lru_h[i], "lru_conv": state_lru_conv[i], "rwkv": state_rwkv,
              "rwkv_shift": state_rwkv_shift[i], "ffn_conv": state_ffn_conv[i]}
        xp, new_p = _prompt_layer(xp, p_prompt[i], lp, rel_bias, last, i, depth, kv_bufs)
        xs3, new_s = _sample_layer(xs, p_sample[i], st, lp, rel_bias, last, s_buf)
        xs = xs3[:, 0, :]
        kv_bufs, s_buf = new_p["kv_bufs"], new_s["rwkv_buf"]
        outs_p.append(new_p)
        outs_s.append(new_s)
    stack = lambda outs, name: jnp.stack([o[name] for o in outs])
    tokens_major = lambda z: z.reshape(depth, b, ATT_HEADS, ATT_HEAD, t).transpose(0, 1, 4, 2, 3)
    res = [xp, xs[:, None, :],
           tokens_major(kv_bufs[0]), stack(outs_s, "k"), tokens_major(kv_bufs[1]), stack(outs_s, "v")]
    for name in ("lru_h", "lru_conv", "rwkv", "rwkv_shift", "ffn_conv"):
        res.append(stack(outs_p, name))
        res.append(s_buf.transpose(0, 4, 1, 2, 3) if name == "rwkv" else stack(outs_s, name))
    return tuple(res)
```

```python
import functools
import math

import numpy as np
import jax
import jax.numpy as jnp
from jax import lax
from jax.experimental import pallas as pl
from jax.experimental.pallas import tpu as pltpu

F32 = jnp.float32
BF16 = jnp.bfloat16
I32 = jnp.int32

D_MODEL = 1024
DEPTH = 2
PAGE_SIZE = 128
N_BRANCH = 3
BRANCH_W = D_MODEL // 2
LRU_BLOCKS = 8
LRU_CONV = 4
LRU_C = 8.0
RWKV_HEAD = 64
RWKV_HEADS = BRANCH_W // RWKV_HEAD
DECAY_LORA = 64
AAA_LORA = 64
GATE_LORA = 128
RWKV_COLS = 3 * BRANCH_W + DECAY_LORA + AAA_LORA + GATE_LORA
RWKV_GN_EPS = 64e-5
ATT_HEAD = 64
ATT_HEADS = BRANCH_W // ATT_HEAD
MOBA_BLOCK = 256
MOBA_TOPK = 3
N_BUCKETS = 32
MAX_DISTANCE = 128
D_FF = 3 * D_MODEL
FFN_CONV = 3
PLE_DIM = 256
RMS_EPS = 1e-6
N_IN = 2 * BRANCH_W + RWKV_COLS + 3 * BRANCH_W + N_BRANCH * D_MODEL

_SEG_EDGES = (0, 2 * BRANCH_W, 2 * BRANCH_W + RWKV_COLS, 2 * BRANCH_W + RWKV_COLS + BRANCH_W,
              2 * BRANCH_W + RWKV_COLS + 2 * BRANCH_W, 2 * BRANCH_W + RWKV_COLS + 3 * BRANCH_W, N_IN)
_IN_SEGS = tuple(zip(_SEG_EDGES[:-1], _SEG_EDGES[1:]))

LANES = 128
SUBLANES = 8
VMEM_LIMIT = 56 * 1024 * 1024
NEG = -1e30
LORA_W = DECAY_LORA + AAA_LORA


def _params(n_axes):
    return pltpu.CompilerParams(dimension_semantics=("arbitrary",) * n_axes, vmem_limit_bytes=VMEM_LIMIT)


def _const(shape):
    return pl.BlockSpec(shape, lambda *_: (0,) * len(shape), pipeline_mode=pl.Buffered(1))


def _softplus(x):
    return jnp.maximum(x, 0.0) + jnp.log1p(jnp.exp(-jnp.abs(x)))


def _gelu_tanh(x):
    return 0.5 * x * (1.0 + jnp.tanh(math.sqrt(2.0 / math.pi) * (x + 0.044715 * (x * x * x))))


def _rms(x, g):
    return x * lax.rsqrt(jnp.mean(x * x, axis=-1, keepdims=True) + RMS_EPS) * g


def _dot(a, b):
    return jnp.dot(a, b, preferred_element_type=F32)


def _dot_nt(a, b, precision=None):
    return lax.dot_general(a, b, (((1,), (1,)), ((), ())), precision=precision, preferred_element_type=F32)


def _head_sum(x, ones_bd):
    hi = x.astype(BF16)
    lo = (x - hi.astype(F32)).astype(BF16)
    return _dot(hi, ones_bd) + _dot(lo, ones_bd)


def _in_proj_kernel(x_ref, g_ref, w_ref, *out_refs):
    xn = _rms(x_ref[...], g_ref[...]).astype(BF16)
    for ref, (lo, hi) in zip(out_refs, _IN_SEGS):
        ref[...] = _dot(xn, w_ref[:, lo:hi])


def _in_proj(x, g, w_bf16, tm):
    m = x.shape[0]
    widths = [hi - lo for lo, hi in _IN_SEGS]
    return pl.pallas_call(
        _in_proj_kernel,
        grid=(m // tm,),
        in_specs=[pl.BlockSpec((tm, D_MODEL), lambda i: (i, 0)), _const((1, D_MODEL)), _const((D_MODEL, N_IN))],
        out_specs=[pl.BlockSpec((tm, w), lambda i: (i, 0)) for w in widths],
        out_shape=[jax.ShapeDtypeStruct((m, w), F32) for w in widths],
        compiler_params=_params(1),
        name="in_proj",
    )(x, g, w_bf16)


_SEQ_SEGS = (0, 1, 5)


def _in_proj_seq_kernel(x_ref, g_ref, w_ref, wt_ref, wka_ref, *refs, nblk, n_carried):
    out_refs = refs[n_carried:]
    xn = _rms(x_ref[...], g_ref[...]).astype(BF16)
    for ref, s in zip(out_refs[:3], _SEQ_SEGS):
        lo, hi = _IN_SEGS[s]
        ref[...] = _dot(xn, w_ref[:, lo:hi])
    for j, ref in enumerate(out_refs[3:6]):
        ref[0] = _dot_nt(wt_ref[j], xn)
    ka_ref = out_refs[6]
    n = pl.program_id(0) % nblk
    onehot = (lax.broadcasted_iota(I32, (1, LANES), 1) == ATT_HEAD + n).astype(F32)
    for h in range(ATT_HEADS):
        ka_ref[0, h] = (_dot(xn, wka_ref[h]) + onehot).astype(BF16)


def _in_proj_seq(x, g, pw, b, t, layer, depth, kv_bufs):
    m = x.shape[0]
    tm = MOBA_BLOCK
    nblk = t // tm
    widths = [_IN_SEGS[s][1] - _IN_SEGS[s][0] for s in _SEQ_SEGS]
    blk_t = pl.BlockSpec((1, BRANCH_W, tm), lambda i: (i // nblk, 0, i % nblk))
    buf_t = pl.BlockSpec((None, 1, BRANCH_W, tm), lambda i: (layer, i // nblk, 0, i % nblk))
    in_specs = [pl.BlockSpec((tm, D_MODEL), lambda i: (i, 0)), _const((1, D_MODEL)), _const((D_MODEL, N_IN)),
                _const((3, BRANCH_W, D_MODEL)), _const((ATT_HEADS, D_MODEL, LANES))]
    args = [x, g, pw["w_in"], pw["w_qkv_t"], pw["wk_aug"]]
    aliases = {}
    if kv_bufs is not None:
        in_specs += [pl.BlockSpec(memory_space=pl.ANY)] * 2
        args += list(kv_bufs)
        aliases = {5: 4, 6: 5}
    buf = jax.ShapeDtypeStruct((depth, b, BRANCH_W, t), F32)
    return pl.pallas_call(
        functools.partial(_in_proj_seq_kernel, nblk=nblk, n_carried=len(aliases)),
        grid=(m // tm,),
        in_specs=in_specs,
        out_specs=[pl.BlockSpec((tm, w), lambda i: (i, 0)) for w in widths]
                  + [blk_t, buf_t, buf_t, pl.BlockSpec((1, ATT_HEADS, tm, LANES), lambda i: (i // nblk, 0, i % nblk, 0))],
        out_shape=[jax.ShapeDtypeStruct((m, w), F32) for w in widths]
                  + [jax.ShapeDtypeStruct((b, BRANCH_W, t), F32), buf, buf,
                     jax.ShapeDtypeStruct((b, ATT_HEADS, t, LANES), BF16)],
        input_output_aliases=aliases,
        compiler_params=_params(1),
        name="in_proj_seq",
    )(*args)


def _lru_gates(xc, wr, br, wi, bi, lam):
    xcb = xc.astype(BF16)
    r = jax.nn.sigmoid(_dot(xcb, wr) + br)
    ig = jax.nn.sigmoid(_dot(xcb, wi) + bi)
    log_a = (-LRU_C * r) * _softplus(-lam)
    a = jnp.exp(log_a)
    bx = jnp.sqrt(1.0 - jnp.exp(2.0 * log_a)) * (ig * xc)
    return a, bx


def _lru_seq_kernel(cols_ref, cw_ref, cb_ref, wr_ref, br_ref, wi_ref, bi_ref, lam_ref,
                    y_ref, conv_ref, h_ref, ext_ref, hc_ref, *, tc):
    t = pl.program_id(1)

    @pl.when(t == 0)
    def _():
        ext_ref[0:SUBLANES, :] = jnp.zeros((SUBLANES, BRANCH_W), F32)
        hc_ref[...] = jnp.zeros_like(hc_ref)

    @pl.when(t > 0)
    def _():
        ext_ref[0:SUBLANES, :] = ext_ref[tc:tc + SUBLANES, :]

    xa = cols_ref[0, :, 0:BRANCH_W]
    ga = cols_ref[0, :, BRANCH_W:]
    ext_ref[SUBLANES:, :] = xa
    xc = ext_ref[pl.ds(SUBLANES - 3, tc), :] * cw_ref[0:1, :]
    xc = xc + ext_ref[pl.ds(SUBLANES - 2, tc), :] * cw_ref[1:2, :]
    xc = xc + ext_ref[pl.ds(SUBLANES - 1, tc), :] * cw_ref[2:3, :]
    xc = xc + xa * cw_ref[3:4, :] + cb_ref[...]
    a, bx = _lru_gates(xc, wr_ref[...], br_ref[...], wi_ref[...], bi_ref[...], lam_ref[...])
    sub = lax.broadcasted_iota(I32, (tc, BRANCH_W), 0) % SUBLANES
    d = 1
    while d < SUBLANES:
        keep = sub >= d
        a_s = jnp.where(keep, pltpu.roll(a, d, 0), 1.0)
        b_s = jnp.where(keep, pltpu.roll(bx, d, 0), 0.0)
        bx = a * b_s + bx
        a = a * a_s
        d *= 2
    carry = hc_ref[...]
    tiles = []
    for g in range(tc // SUBLANES):
        rows = slice(g * SUBLANES, (g + 1) * SUBLANES)
        hg = a[rows] * carry + bx[rows]
        carry = hg[SUBLANES - 1:SUBLANES, :]
        tiles.append(hg)
    h = jnp.concatenate(tiles, axis=0)
    hc_ref[...] = carry
    y_ref[0] = h * _gelu_tanh(ga)
    conv_ref[0] = ext_ref[tc + SUBLANES - 3:tc + SUBLANES, :]
    h_ref[0] = h[tc - 1:tc, :]


def _lru_seq(cols, lw, tc):
    b, t, _ = cols.shape
    vec = _const((1, BRANCH_W))
    mat = _const((BRANCH_W, BRANCH_W))
    return pl.pallas_call(
        functools.partial(_lru_seq_kernel, tc=tc),
        grid=(b, t // tc),
        in_specs=[pl.BlockSpec((1, tc, 2 * BRANCH_W), lambda i, j: (i, j, 0)),
                  _const((LRU_CONV, BRANCH_W)), vec, mat, vec, mat, vec, vec],
        out_specs=[pl.BlockSpec((1, tc, BRANCH_W), lambda i, j: (i, j, 0)),
                   pl.BlockSpec((1, LRU_CONV - 1, BRANCH_W), lambda i, j: (i, 0, 0)),
                   pl.BlockSpec((1, 1, BRANCH_W), lambda i, j: (i, 0, 0))],
        out_shape=[jax.ShapeDtypeStruct((b, t, BRANCH_W), F32),
                   jax.ShapeDtypeStruct((b, LRU_CONV - 1, BRANCH_W), F32),
                   jax.ShapeDtypeStruct((b, 1, BRANCH_W), F32)],
        scratch_shapes=[pltpu.VMEM((tc + SUBLANES, BRANCH_W), F32), pltpu.VMEM((1, BRANCH_W), F32)],
        compiler_params=_params(2),
        name="lru_seq",
    )(cols, lw["conv_w"], lw["conv_b"], lw["w_r"], lw["b_r"], lw["w_i"], lw["b_i"], lw["lam"])


def _lru_step_kernel(cols_ref, buf_ref, h0_ref, cw_ref, cb_ref, wr_ref, br_ref, wi_ref, bi_ref, lam_ref,
                     y_ref, conv_ref, h_ref):
    xa = cols_ref[:, 0:BRANCH_W]
    ga = cols_ref[:, BRANCH_W:]
    xc = buf_ref[:, 0:BRANCH_W] * cw_ref[0:1, :]
    xc = xc + buf_ref[:, BRANCH_W:2 * BRANCH_W] * cw_ref[1:2, :]
    xc = xc + buf_ref[:, 2 * BRANCH_W:] * cw_ref[2:3, :]
    xc = xc + xa * cw_ref[3:4, :] + cb_ref[...]
    a, bx = _lru_gates(xc, wr_ref[...], br_ref[...], wi_ref[...], bi_ref[...], lam_ref[...])
    h = a * h0_ref[...] + bx
    y_ref[...] = h * _gelu_tanh(ga)
    conv_ref[:, 0:2 * BRANCH_W] = buf_ref[:, BRANCH_W:]
    conv_ref[:, 2 * BRANCH_W:] = xa
    h_ref[...] = h


def _lru_step(cols, buf, h0, lw):
    n = cols.shape[0]
    full = lambda w: pl.BlockSpec((n, w), lambda i: (0, 0))
    vec = _const((1, BRANCH_W))
    mat = _const((BRANCH_W, BRANCH_W))
    return pl.pallas_call(
        _lru_step_kernel,
        grid=(1,),
        in_specs=[full(2 * BRANCH_W), full(3 * BRANCH_W), full(BRANCH_W),
                  _const((LRU_CONV, BRANCH_W)), vec, mat, vec, mat, vec, vec],
        out_specs=[full(BRANCH_W), full(3 * BRANCH_W), full(BRANCH_W)],
        out_shape=[jax.ShapeDtypeStruct((n, BRANCH_W), F32), jax.ShapeDtypeStruct((n, 3 * BRANCH_W), F32),
                   jax.ShapeDtypeStruct((n, BRANCH_W), F32)],
        compiler_params=_params(1),
        name="lru_step",
    )(cols, buf, h0, lw["conv_w"], lw["conv_b"], lw["w_r"], lw["b_r"], lw["w_i"], lw["b_i"], lw["lam"])


def _rwkv_token_math(cols, prev, mu, w0, w2p, a0, a2p, g2, k_k, k_a, r_k, ones_bd, head_sel=None):
    mixed = cols + (prev - cols) * mu
    r = mixed[:, 0:BRANCH_W]
    k = mixed[:, BRANCH_W:2 * BRANCH_W]
    v = mixed[:, 2 * BRANCH_W:3 * BRANCH_W]
    la = mixed[:, 3 * BRANCH_W:3 * BRANCH_W + LORA_W]
    gl = mixed[:, 3 * BRANCH_W + LORA_W:]
    w = -_softplus(-(w0 + _dot(jnp.tanh(la).astype(BF16), w2p))) - 0.5
    decay = jnp.exp(-jnp.exp(w))
    a = jax.nn.sigmoid(a0 + _dot(la.astype(BF16), a2p))
    g = _dot(jax.nn.sigmoid(gl).astype(BF16), g2)
    kk = k * k_k
    k_mod = k * (1.0 + (a - 1.0) * k_a)
    bonus = _head_sum(r * k_mod * r_k, ones_bd) * v
    if head_sel is None:
        kk = kk / jnp.maximum(jnp.sqrt(_head_sum(kk * kk, ones_bd)), 1e-12)
        return r, decay, k_mod, v, -kk, kk * a, g, bonus
    inv_norm = 1.0 / jnp.maximum(jnp.sqrt(_head_sum(kk * kk, head_sel)), 1e-12)
    return r, decay, k, v, a, g, bonus, inv_norm


_RW_PARAM_ORDER = ("mu", "w0", "w2p", "a0", "a2p", "g2", "k_k", "k_a", "r_k", "ones_bd")


def _rw_param_specs():
    vec = _const((1, BRANCH_W))
    return [_const((1, RWKV_COLS)), vec, _const((LORA_W, BRANCH_W)), vec, _const((LORA_W, BRANCH_W)),
            _const((GATE_LORA, BRANCH_W)), vec, vec, vec, _const((BRANCH_W, BRANCH_W))]


def _rwkv_pre_seq_kernel(cols_ref, *refs, tc):
    prm = [r[...] for r in refs[:11]]
    outs = refs[11:19]
    ext_ref = refs[19]
    t = pl.program_id(1)

    @pl.when(t == 0)
    def _():
        ext_ref[0:SUBLANES, :] = jnp.zeros((SUBLANES, RWKV_COLS), F32)

    @pl.when(t > 0)
    def _():
        ext_ref[0:SUBLANES, :] = ext_ref[tc:tc + SUBLANES, :]

    cols = cols_ref[0]
    ext_ref[SUBLANES:, :] = cols
    prev = ext_ref[pl.ds(SUBLANES - 1, tc), :]
    for ref, val in zip(outs, _rwkv_token_math(cols, prev, *prm)):
        ref[0] = val


def _rwkv_pre_seq(cols, rw, tc):
    b, t, _ = cols.shape
    blk = pl.BlockSpec((1, tc, BRANCH_W), lambda i, j: (i, j, 0))
    return pl.pallas_call(
        functools.partial(_rwkv_pre_seq_kernel, tc=tc),
        grid=(b, t // tc),
        in_specs=[pl.BlockSpec((1, tc, RWKV_COLS), lambda i, j: (i, j, 0))] + _rw_param_specs()
                 + [_const((BRANCH_W, LANES))],
        out_specs=[blk] * 7 + [pl.BlockSpec((1, tc, LANES), lambda i, j: (i, j, 0))],
        out_shape=[jax.ShapeDtypeStruct((b, t, BRANCH_W), F32)] * 7 + [jax.ShapeDtypeStruct((b, t, LANES), F32)],
        scratch_shapes=[pltpu.VMEM((tc + SUBLANES, RWKV_COLS), F32)],
        compiler_params=_params(2),
        name="rwkv_pre_seq",
    )(cols, *[rw[k] for k in _RW_PARAM_ORDER], rw["head_sel"])


def _rwkv_pre_step_kernel(cols_ref, prev_ref, *refs):
    prm = [r[...] for r in refs[:10]]
    for ref, val in zip(refs[10:18], _rwkv_token_math(cols_ref[...], prev_ref[...], *prm)):
        ref[...] = val


def _rwkv_pre_step(cols, prev, rw):
    n = cols.shape[0]
    full = lambda w: pl.BlockSpec((n, w), lambda i: (0, 0))
    return pl.pallas_call(
        _rwkv_pre_step_kernel,
        grid=(1,),
        in_specs=[full(RWKV_COLS), full(RWKV_COLS)] + _rw_param_specs(),
        out_specs=[full(BRANCH_W)] * 8,
        out_shape=[jax.ShapeDtypeStruct((n, BRANCH_W), F32)] * 8,
        compiler_params=_params(1),
        name="rwkv_pre_step",
    )(cols, prev, *[rw[k] for k in _RW_PARAM_ORDER])


SCAN_GROUPS = 4
SCAN_GROUP_LANES = LANES // SCAN_GROUPS
SCAN_KEY_ROWS = RWKV_HEAD // SCAN_GROUPS


def _rwkv_scan_kernel(r_ref, w_ref, k_ref, g_ref, n_ref, kkc_ref, kac_ref, v_ref, y_ref, so_ref, s_ref, exp_ref,
                      pk_ref, *, tc, ni):
    t = pl.program_id(0)

    @pl.when(t == 0)
    def _():
        s_ref[...] = jnp.zeros_like(s_ref)

    k_raw = k_ref[...]
    rate = g_ref[...]
    kk = k_raw * kkc_ref[...] * n_ref[...]
    pk_ref[0] = -kk
    pk_ref[1] = kk * rate
    pk_ref[2] = k_raw * (1.0 + (rate - 1.0) * kac_ref[...])

    def unpack(slot, tt):
        tiles = (pk_ref[0, tt], pk_ref[1, tt], w_ref[tt], pk_ref[2, tt], r_ref[tt])
        for idx, x in enumerate(tiles):
            for g in range(SCAN_GROUPS):
                xg = x if g == 0 else pltpu.roll(x, LANES - SCAN_GROUP_LANES * g, 1)
                exp_ref[idx, slot, g * SCAN_KEY_ROWS:(g + 1) * SCAN_KEY_ROWS, :] = xg

    unpack(0, 0)

    n_acc = 4

    def fold(accs):
        return (accs[0] + accs[1]) + (accs[2] + accs[3])

    def step(tt, slot):
        unpack(1 - slot, jnp.minimum(tt + 1, tc - 1))
        row = lambda kk, j: exp_ref[kk, slot, pl.ds(j, 1), :]
        groups = range(ni // SUBLANES)
        acc = [[None] * n_acc for _ in groups]
        for j in range(RWKV_HEAD):
            a = row(0, j)
            for g in groups:
                term = s_ref[g, j] * a
                acc[g][j % n_acc] = term if j < n_acc else acc[g][j % n_acc] + term
        sa = [fold(acc[g]) for g in groups]
        vg = [v_ref[tt, g * SUBLANES:(g + 1) * SUBLANES, :] for g in groups]
        acc = [[None] * n_acc for _ in groups]
        for j in range(RWKV_HEAD):
            b, w, k, r = row(1, j), row(2, j), row(3, j), row(4, j)
            for g in groups:
                s = s_ref[g, j] * w + sa[g] * b + vg[g] * k
                s_ref[g, j] = s
                term = s * r
                acc[g][j % n_acc] = term if j < n_acc else acc[g][j % n_acc] + term
        for g in groups:
            y_ref[tt, g * SUBLANES:(g + 1) * SUBLANES, :] = fold(acc[g])

    def two_steps(pair, carry):
        step(2 * pair, 0)
        step(2 * pair + 1, 1)
        return carry

    lax.fori_loop(0, tc // 2, two_steps, 0)

    @pl.when(t == pl.num_programs(0) - 1)
    def _():
        so_ref[...] = s_ref[...]


def _rwkv_scan(r, w, k, rate, inv_norm, kk_c, ka_c, v, tc):
    t = r.shape[0]
    ni = v.shape[1]
    vec = pl.BlockSpec((tc, SCAN_KEY_ROWS, LANES), lambda j: (j, 0, 0))
    row = pl.BlockSpec((tc, ni, LANES), lambda j: (j, 0, 0))
    st = pl.BlockSpec((ni // SUBLANES, RWKV_HEAD, SUBLANES, LANES), lambda j: (0, 0, 0, 0))
    par = _const((SCAN_KEY_ROWS, LANES))
    return pl.pallas_call(
        functools.partial(_rwkv_scan_kernel, tc=tc, ni=ni),
        grid=(t // tc,),
        in_specs=[vec] * 4 + [pl.BlockSpec((tc, 1, LANES), lambda j: (j, 0, 0)), par, par, row],
        out_specs=[row, st],
        out_shape=[jax.ShapeDtypeStruct((t, ni, LANES), F32),
                   jax.ShapeDtypeStruct((ni // SUBLANES, RWKV_HEAD, SUBLANES, LANES), F32)],
        scratch_shapes=[pltpu.VMEM((ni // SUBLANES, RWKV_HEAD, SUBLANES, LANES), F32),
                        pltpu.VMEM((5, 2, RWKV_HEAD, LANES), F32),
                        pltpu.VMEM((3, tc, SCAN_KEY_ROWS, LANES), F32)],
        compiler_params=_params(1),
        name="rwkv_scan",
    )(r, w, k, rate, inv_norm, kk_c, ka_c, v)


def _rwkv_step_kernel(s_ref, r_ref, w_ref, k_ref, a_ref, b_ref, v_ref, *refs):
    so_ref, y_ref = refs[-2:]
    a = a_ref[0]
    b = b_ref[0]
    w = w_ref[0]
    k = k_ref[0]
    r = r_ref[0]
    for i in range(RWKV_HEAD):
        s = s_ref[0, i]
        sa = jnp.sum(s * a, axis=0, keepdims=True)
        s = s * w + sa * b + v_ref[0, pl.ds(i, 1), :] * k
        so_ref[0, i] = s
        y_ref[0, pl.ds(i, 1), :] = jnp.sum(s * r, axis=0, keepdims=True)


def _rwkv_step(s, layer, r, w, k, a, b, v, s_out):
    depth, nh, hd, _, n = s.shape
    st = pl.BlockSpec((None, 1, hd, hd, n), lambda i: (layer, i, 0, 0, 0))
    vec = pl.BlockSpec((1, hd, n), lambda i: (i, 0, 0))
    in_specs = [st] + [vec] * 6
    args = [s, r, w, k, a, b, v]
    aliases = {}
    if s_out is not None:
        in_specs.append(pl.BlockSpec(memory_space=pl.ANY))
        args.append(s_out)
        aliases = {7: 0}
    return pl.pallas_call(
        _rwkv_step_kernel,
        grid=(nh,),
        in_specs=in_specs,
        out_specs=[st, vec],
        out_shape=[jax.ShapeDtypeStruct((depth, nh, hd, hd, n), F32), jax.ShapeDtypeStruct((nh, hd, n), F32)],
        input_output_aliases=aliases,
        compiler_params=_params(1),
        name="rwkv_step",
    )(*args)


def _t5_bucket_np(dist):
    n = np.maximum(dist, 0)
    max_exact = N_BUCKETS // 2
    nf = np.maximum(n, 1).astype(np.float32)
    large = max_exact + (np.log(nf / np.float32(max_exact)) / np.float32(math.log(MAX_DISTANCE / max_exact))
                         * np.float32(N_BUCKETS - max_exact)).astype(np.int32)
    large = np.minimum(large, N_BUCKETS - 1)
    return np.where(n < max_exact, n, large).astype(np.int32)


def _moba_seq_kernel(tab_ref, bko_ref, bkp_ref, qt_ref, ka_ref, vt_ref, o_ref,
                     km_ref, bias_ref, qa_ref, m_ref, l_ref, acc_ref, s_ref, *, nblk):
    bi = pl.program_id(0)
    i = pl.program_id(1)
    blk = MOBA_BLOCK
    far_bucket = N_BUCKETS - 1
    sel_rows = 2 * SUBLANES
    log2e = math.log2(math.e)
    scale = ATT_HEAD ** -0.5 * log2e
    ones_rows = jnp.ones((SUBLANES, blk), BF16)

    @pl.when((bi == 0) & (i == 0))
    def _():
        bko = bko_ref[...]
        bkp = bkp_ref[...]
        for h in range(ATT_HEADS):
            c = tab_ref[far_bucket, h]
            own = jnp.zeros((blk, blk), F32)
            prev = jnp.zeros((blk, blk), F32)
            for j in range(N_BUCKETS):
                val = (tab_ref[j, h] - c) * log2e
                own = jnp.where(bko == j, val, own)
                prev = jnp.where(bkp == j, val, prev)
            bias_ref[h, 0] = jnp.where(bko < 0, NEG, own)
            bias_ref[h, 1] = prev

    @pl.when(i == 0)
    def _():
        for h in range(ATT_HEADS):
            km_ref[h] = jnp.zeros((sel_rows, ATT_HEAD), F32)
            for n in range(nblk):
                ks = jnp.sum(ka_ref[0, h, n * blk:(n + 1) * blk, :].astype(F32), axis=0, keepdims=True)
                km_ref[h, n:n + 1, :] = ks[:, 0:ATT_HEAD] * (1.0 / blk)

    row = lax.broadcasted_iota(I32, (sel_rows, blk), 0)
    row_f = row.astype(F32)
    for h in range(ATT_HEADS):
        qt = qt_ref[0, h * ATT_HEAD:(h + 1) * ATT_HEAD, :]
        bs = jnp.dot(km_ref[h], qt, precision=lax.Precision.HIGHEST, preferred_element_type=F32)
        work = jnp.where(row < i, bs, -jnp.inf)
        sel = row == i
        for _ in range(MOBA_TOPK):
            mx = jnp.max(work, axis=0, keepdims=True)
            is_m = (work == mx) & (work > -jnp.inf)
            idx = jnp.min(jnp.where(is_m, row_f, 4.0 * LANES), axis=0, keepdims=True)
            pick = row_f == idx
            sel = sel | pick
            work = jnp.where(pick, -jnp.inf, work)
        selb = jnp.where(sel, 0.0, NEG)
        pad = jnp.zeros((LANES - ATT_HEAD - sel_rows, blk), F32)
        qa_ref[h] = jnp.concatenate([qt * scale, selb, pad], axis=0).astype(BF16)

    def key_rows(n):
        return pl.ds(n * blk, blk) if isinstance(n, int) else pl.ds(pl.multiple_of(n * blk, blk), blk)

    def block_steps(blocks):
        for slot, (n, _) in enumerate(blocks):
            for h in range(ATT_HEADS):
                s_ref[slot, h] = _dot(ka_ref[0, h, key_rows(n), :], qa_ref[h])
        for slot, (n, kind) in enumerate(blocks):
            for h in range(ATT_HEADS):
                s = s_ref[slot, h]
                vt = vt_ref[0, h * ATT_HEAD:(h + 1) * ATT_HEAD, key_rows(n)].astype(BF16)
                if kind == "own":
                    s = s + bias_ref[h, 0]
                    m_new = jnp.max(s, axis=0, keepdims=True)
                    pb = jnp.exp2(s - m_new).astype(BF16)
                    l_ref[h] = _dot(ones_rows, pb)[0:1, :]
                    acc_ref[h] = _dot(vt, pb)
                else:
                    if kind == "prev":
                        s = s + bias_ref[h, 1] + jnp.where(i >= 1, 0.0, NEG)
                    m_old = m_ref[h]
                    m_new = jnp.maximum(m_old, jnp.max(s, axis=0, keepdims=True))
                    alpha = jnp.exp2(m_old - m_new)
                    pb = jnp.exp2(s - m_new).astype(BF16)
                    l_ref[h] = alpha * l_ref[h] + _dot(ones_rows, pb)[0:1, :]
                    acc_ref[h] = alpha * acc_ref[h] + _dot(vt, pb)
                m_ref[h] = m_new

    block_steps([(i, "own"), (jnp.maximum(i - 1, 0), "prev")])
    n_far = jnp.maximum(i - 1, 0)
    odd = jnp.bitwise_and(n_far, 1)

    @pl.when(odd == 1)
    def _():
        block_steps([(0, "far")])

    def far_pair(pr, carry):
        n0 = odd + 2 * pr
        block_steps([(n0, "far"), (n0 + 1, "far")])
        return carry

    lax.fori_loop(0, lax.shift_right_logical(n_far, 1), far_pair, 0)
    for h in range(ATT_HEADS):
        o_ref[0, h * ATT_HEAD:(h + 1) * ATT_HEAD, :] = acc_ref[h] / l_ref[h]


def _moba_seq(q_t, k_aug, v_buf, layer, rel_bias):
    b, nh, t, _ = k_aug.shape
    blk = MOBA_BLOCK
    nblk = t // blk
    assert nblk <= 2 * SUBLANES
    d = np.arange(blk)[None, :] - np.arange(blk)[:, None]
    bko = jnp.asarray(np.where(d >= 0, _t5_bucket_np(d), -1).astype(np.int32))
    bkp = jnp.asarray(_t5_bucket_np(d + blk))
    q_blk = pl.BlockSpec((1, BRANCH_W, blk), lambda i, j: (i, 0, j))
    return pl.pallas_call(
        functools.partial(_moba_seq_kernel, nblk=nblk),
        grid=(b, nblk),
        in_specs=[pl.BlockSpec(memory_space=pltpu.SMEM), _const((blk, blk)), _const((blk, blk)), q_blk,
                  pl.BlockSpec((1, nh, t, LANES), lambda i, j: (i, 0, 0, 0), pipeline_mode=pl.Buffered(1)),
                  pl.BlockSpec((None, 1, BRANCH_W, t), lambda i, j: (layer, i, 0, 0), pipeline_mode=pl.Buffered(1))],
        out_specs=q_blk,
        out_shape=jax.ShapeDtypeStruct((b, BRANCH_W, t), F32),
        scratch_shapes=[pltpu.VMEM((nh, 2 * SUBLANES, ATT_HEAD), F32), pltpu.VMEM((nh, 2, blk, blk), F32),
                        pltpu.VMEM((nh, LANES, blk), BF16), pltpu.VMEM((nh, 1, blk), F32),
                        pltpu.VMEM((nh, 1, blk), F32), pltpu.VMEM((nh, ATT_HEAD, blk), F32),
                        pltpu.VMEM((2, nh, blk, blk), F32)],
        compiler_params=_params(2),
        name="moba_seq",
    )(rel_bias, bko, bkp, q_t, k_aug, v_buf)


def _moba_step_kernel(pt_ref, q_ref, kn_ref, vn_ref, relt_ref, bkt_ref, *refs, n_pages):
    kp = refs[:n_pages]
    vp = refs[n_pages:2 * n_pages]
    o_ref, acc_ref = refs[2 * n_pages:2 * n_pages + 2]
    pages_per_block = MOBA_BLOCK // PAGE_SIZE
    n_blocks = n_pages // pages_per_block
    scale = ATT_HEAD ** -0.5
    heads = range(ATT_HEADS)

    def per_head(fn):
        return jnp.concatenate([fn(h) for h in heads], axis=0)

    relt = relt_ref[...]
    c_far = relt[:, N_BUCKETS - 1:N_BUCKETS]
    bkt = bkt_ref[...]
    bias_last = jnp.zeros((ATT_HEADS, PAGE_SIZE), F32)
    for j in range(N_BUCKETS):
        bias_last = jnp.where(bkt == j, relt[:, j:j + 1] - c_far, bias_last)

    q = q_ref[0]
    qb = [jnp.broadcast_to(q[h], (ATT_HEAD, PAGE_SIZE)) for h in heads]
    logit, rowsum = [], []
    for p in range(n_pages):
        raw = per_head(lambda h: jnp.sum(kp[p][0, 0, h] * qb[h], axis=0, keepdims=True))
        rowsum.append(jnp.sum(raw, axis=1, keepdims=True))
        lg = raw * scale
        logit.append(lg + bias_last if p == n_pages - 1 else lg)

    sc = []
    for n in range(n_blocks):
        tot = rowsum[n * pages_per_block]
        for j in range(1, pages_per_block):
            tot = tot + rowsum[n * pages_per_block + j]
        sc.append(tot * (1.0 / MOBA_BLOCK))
    sel = []
    for n in range(n_blocks):
        rank = jnp.zeros((ATT_HEADS, 1), I32)
        for j in range(n_blocks):
            if j != n:
                ahead = sc[j] > sc[n]
                if j < n:
                    ahead = ahead | (sc[j] == sc[n])
                rank = rank + ahead.astype(I32)
        sel.append(rank < MOBA_TOPK)

    s_self = per_head(lambda h: jnp.sum(q[h] * kn_ref[0, h], axis=0, keepdims=True)) * scale + (relt[:, 0:1] - c_far)
    m_all = s_self
    for p in range(n_pages):
        m_all = jnp.maximum(m_all, jnp.where(sel[p // pages_per_block], jnp.max(logit[p], axis=1, keepdims=True), -jnp.inf))
    w_self = jnp.exp(s_self - m_all)
    l_all = w_self
    acc_ref[...] = jnp.zeros_like(acc_ref)
    for p in range(n_pages):
        e = jnp.where(sel[p // pages_per_block], jnp.exp(logit[p] - m_all), 0.0)
        l_all = l_all + jnp.sum(e, axis=1, keepdims=True)
        for h in heads:
            acc_ref[h] += vp[p][0, 0, h] * e[h:h + 1, :]
    for h in heads:
        out = jnp.sum(acc_ref[h], axis=1, keepdims=True) + w_self[h:h + 1, :] * vn_ref[0, h]
        o_ref[0, h] = out / l_all[h:h + 1, :]


def _moba_step(q, k_new, v_new, cache_k, cache_v, layer, page_table, rel_bias):
    n, n_pages = page_table.shape
    col = lambda z: z.reshape(n, ATT_HEADS, ATT_HEAD, 1)
    col_spec = pl.BlockSpec((1, ATT_HEADS, ATT_HEAD, 1), lambda i, pt: (i, 0, 0, 0))
    relt = jnp.pad(rel_bias.T, ((0, 0), (0, LANES - N_BUCKETS)))
    bkt = jnp.asarray(_t5_bucket_np(PAGE_SIZE - np.arange(PAGE_SIZE))[None, :])

    def page_spec(p):
        return pl.BlockSpec((1, 1, ATT_HEADS, ATT_HEAD, PAGE_SIZE),
                            lambda i, pt: (layer, pt[i * n_pages + p], 0, 0, 0))

    grid_spec = pltpu.PrefetchScalarGridSpec(
        num_scalar_prefetch=1,
        grid=(n,),
        in_specs=[col_spec, col_spec, col_spec,
                  pl.BlockSpec((ATT_HEADS, LANES), lambda i, pt: (0, 0)),
                  pl.BlockSpec((1, PAGE_SIZE), lambda i, pt: (0, 0))]
                 + [page_spec(p) for p in range(n_pages)] * 2,
        out_specs=col_spec,
        scratch_shapes=[pltpu.VMEM((ATT_HEADS, ATT_HEAD, PAGE_SIZE), F32)],
    )
    rows_last = lambda c: c.transpose(0, 1, 3, 4, 2)
    out = pl.pallas_call(
        functools.partial(_moba_step_kernel, n_pages=n_pages),
        grid_spec=grid_spec,
        out_shape=jax.ShapeDtypeStruct((n, ATT_HEADS, ATT_HEAD, 1), F32),
        compiler_params=_params(1),
        name="moba_step",
    )(page_table.reshape(-1), col(q), col(k_new), col(v_new), relt, bkt,
      *([rows_last(cache_k)] * n_pages), *([rows_last(cache_v)] * n_pages))
    return out.reshape(n, BRANCH_W)


def _merge_kernel(x_ref, ya_ref, yr_ref, bon_ref, g_ref, yc_ref, gate_ref,
                  lnw_ref, lnb_ref, ones_ref, wb_ref, wo_ref, o_ref, *, yc_transposed):
    ones_bd = ones_ref[...]
    y = yr_ref[...]
    mu = _head_sum(y, ones_bd) * (1.0 / RWKV_HEAD)
    d = y - mu
    var = _head_sum(d * d, ones_bd) * (1.0 / RWKV_HEAD)
    yb = (d * lax.rsqrt(var + RWKV_GN_EPS) * lnw_ref[...] + lnb_ref[...] + bon_ref[...]) * g_ref[...]
    yc = yc_ref[0].T if yc_transposed else yc_ref[...]
    merged = None
    for j, yj in enumerate((ya_ref[...], yb, yc)):
        gate = jax.nn.sigmoid(gate_ref[:, j * D_MODEL:(j + 1) * D_MODEL])
        term = _dot(yj.astype(BF16), wb_ref[j]) * gate
        merged = term if merged is None else merged + term
    o_ref[...] = x_ref[...] + _dot(merged.astype(BF16), wo_ref[...])


def _merge(x, ya, yr, bonus, g, yc, gates, mw, tm):
    m = x.shape[0]
    tok = lambda w: pl.BlockSpec((tm, w), lambda i: (i, 0))
    vec = _const((1, BRANCH_W))
    yc_transposed = yc.ndim == 3
    if yc_transposed:
        nblk = yc.shape[2] // tm
        yc_spec = pl.BlockSpec((1, BRANCH_W, tm), lambda i: (i // nblk, 0, i % nblk))
    else:
        yc_spec = tok(BRANCH_W)
    return pl.pallas_call(
        functools.partial(_merge_kernel, yc_transposed=yc_transposed),
        grid=(m // tm,),
        in_specs=[tok(D_MODEL)] + [tok(BRANCH_W)] * 4 + [yc_spec, tok(N_BRANCH * D_MODEL), vec, vec,
                  _const((BRANCH_W, BRANCH_W)), _const((N_BRANCH, BRANCH_W, D_MODEL)), _const((D_MODEL, D_MODEL))],
        out_specs=tok(D_MODEL),
        out_shape=jax.ShapeDtypeStruct((m, D_MODEL), F32),
        compiler_params=_params(1),
        name="merge",
    )(x, ya, yr, bonus, g, yc, gates, mw["ln_w"], mw["ln_b"], mw["ones_bd"], mw["w_branch"], mw["w_out"])


FFN_CHUNK = 512


def _ffn_kernel(*refs, seq_mode, final_norm, tm, tiles_per_seq):
    (x_ref, p_ref, ln2_ref, wup_ref, fcw_ref, fcb_ref, wdn_ref, ln3_ref, pg_ref, pp_ref, lnf_ref) = refs[:11]
    if seq_mode:
        o_ref, fc_ref, ext_ref = refs[11:14]
        i = pl.program_id(0)

        @pl.when(i % tiles_per_seq == 0)
        def _():
            ext_ref[0:SUBLANES, :] = jnp.zeros((SUBLANES, 2 * D_FF), F32)

        @pl.when(i % tiles_per_seq != 0)
        def _():
            ext_ref[0:SUBLANES, :] = ext_ref[tm:tm + SUBLANES, :]
    else:
        prev_ref, o_ref, u_ref = refs[11:14]

    x = x_ref[...]
    hb = _rms(x, ln2_ref[...]).astype(BF16)

    def up_cols(lo, hi):
        u = _dot(hb, wup_ref[:, lo:hi])
        if seq_mode:
            ext_ref[SUBLANES:, lo:hi] = u
        else:
            u_ref[:, lo:hi] = u

    def conv_cols(lo, hi):
        if seq_mode:
            u = ext_ref[SUBLANES:, lo:hi]
            u2 = ext_ref[pl.ds(SUBLANES - 2, tm), lo:hi]
            u1 = ext_ref[pl.ds(SUBLANES - 1, tm), lo:hi]
        else:
            u = u_ref[:, lo:hi]
            u2 = prev_ref[:, lo:hi]
            u1 = prev_ref[:, 2 * D_FF + lo:2 * D_FF + hi]
        return u2 * fcw_ref[0:1, lo:hi] + u1 * fcw_ref[1:2, lo:hi] + u * fcw_ref[2:3, lo:hi] + fcb_ref[:, lo:hi]

    def up_chunk(c):
        up_cols(c * FFN_CHUNK, (c + 1) * FFN_CHUNK)
        up_cols(D_FF + c * FFN_CHUNK, D_FF + (c + 1) * FFN_CHUNK)

    n_chunks = D_FF // FFN_CHUNK
    up_chunk(0)
    acc = None
    for c in range(n_chunks):
        if c + 1 < n_chunks:
            up_chunk(c + 1)
        lo, hi = c * FFN_CHUNK, (c + 1) * FFN_CHUNK
        act = _gelu_tanh(conv_cols(lo, hi)) * conv_cols(D_FF + lo, D_FF + hi)
        term = _dot(act.astype(BF16), wdn_ref[lo:hi, :])
        acc = term if acc is None else acc + term
    x = x + acc
    x = x + jax.nn.sigmoid(_dot(_rms(x, ln3_ref[...]).astype(BF16), pg_ref[...])) * _dot(p_ref[...].astype(BF16), pp_ref[...])
    o_ref[...] = _rms(x, lnf_ref[...]) if final_norm else x
    if seq_mode:
        fc_ref[0] = ext_ref[tm + SUBLANES - 2:tm + SUBLANES, :]


def _ffn(x, p, fw, tm, seq_len, prev=None, final_norm=False):
    m = x.shape[0]
    seq_mode = seq_len is not None
    tok = lambda w: pl.BlockSpec((tm, w), lambda i: (i, 0))
    vecd = _const((1, D_MODEL))
    in_specs = [tok(D_MODEL), tok(PLE_DIM), vecd, _const((D_MODEL, 2 * D_FF)), _const((FFN_CONV, 2 * D_FF)),
                _const((1, 2 * D_FF)), _const((D_FF, D_MODEL)), vecd, _const((D_MODEL, D_MODEL)),
                _const((PLE_DIM, D_MODEL)), vecd]
    args = [x, p, fw["ln2"], fw["ffn_up"], fw["conv_w"], fw["conv_b"], fw["ffn_down"], fw["ln3"],
            fw["ple_gate"], fw["ple_proj"], fw["ln_f"]]
    if seq_mode:
        tiles_per_seq = seq_len // tm
        out_specs = [tok(D_MODEL), pl.BlockSpec((1, FFN_CONV - 1, 2 * D_FF), lambda i: (i // tiles_per_seq, 0, 0))]
        out_shape = [jax.ShapeDtypeStruct((m, D_MODEL), F32),
                     jax.ShapeDtypeStruct((m // seq_len, FFN_CONV - 1, 2 * D_FF), F32)]
        scratch = [pltpu.VMEM((tm + SUBLANES, 2 * D_FF), F32)]
    else:
        tiles_per_seq = 1
        in_specs.append(tok((FFN_CONV - 1) * 2 * D_FF))
        args.append(prev)
        out_specs = [tok(D_MODEL), tok(2 * D_FF)]
        out_shape = [jax.ShapeDtypeStruct((m, D_MODEL), F32), jax.ShapeDtypeStruct((m, 2 * D_FF), F32)]
        scratch = []
    return pl.pallas_call(
        functools.partial(_ffn_kernel, seq_mode=seq_mode, final_norm=final_norm, tm=tm, tiles_per_seq=tiles_per_seq),
        grid=(m // tm,),
        in_specs=in_specs,
        out_specs=out_specs,
        out_shape=out_shape,
        scratch_shapes=scratch,
        compiler_params=_params(1),
        name="ffn",
    )(*args)


def _block_diag(w):
    g, n, _ = w.shape
    eye = jnp.eye(g, dtype=w.dtype)
    return (eye[:, None, :, None] * w[:, :, None, :]).reshape(g * n, g * n)


def _layer_weights(i, wt):
    row = lambda v: v.reshape(1, -1)
    ones_bd = _block_diag(jnp.ones((RWKV_HEADS, RWKV_HEAD, RWKV_HEAD), BF16))
    zeros_lora = jnp.zeros((DECAY_LORA, BRANCH_W), BF16)
    lw = {"conv_w": wt["lru_conv_w"][i], "conv_b": row(wt["lru_conv_b"][i]),
          "w_r": _block_diag(wt["lru_w_r"][i]).astype(BF16), "b_r": row(wt["lru_b_r"][i]),
          "w_i": _block_diag(wt["lru_w_i"][i]).astype(BF16), "b_i": row(wt["lru_b_i"][i]),
          "lam": row(wt["lru_lambda"][i])}
    rw = {"mu": row(wt["rwkv_mu"][i]), "w0": row(wt["rwkv_w0"][i]),
          "w2p": jnp.concatenate([wt["rwkv_w2"][i].astype(BF16), zeros_lora], axis=0),
          "a0": row(wt["rwkv_a0"][i]),
          "a2p": jnp.concatenate([zeros_lora, wt["rwkv_a2"][i].astype(BF16)], axis=0),
          "g2": wt["rwkv_g2"][i].astype(BF16), "k_k": row(wt["rwkv_k_k"][i]), "k_a": row(wt["rwkv_k_a"][i]),
          "r_k": row(wt["rwkv_r_k"][i]), "ones_bd": ones_bd,
          "head_sel": (jnp.arange(BRANCH_W)[:, None] // RWKV_HEAD == jnp.arange(LANES)[None, :]).astype(BF16)}
    mw = {"ln_w": row(wt["rwkv_ln_w"][i]), "ln_b": row(wt["rwkv_ln_b"][i]), "ones_bd": ones_bd,
          "w_branch": wt["w_branch"][i].astype(BF16), "w_out": wt["w_out"][i].astype(BF16)}
    fw = {"ln2": row(wt["ln2"][i]), "ffn_up": wt["ffn_up"][i].astype(BF16), "conv_w": wt["ffn_conv_w"][i],
          "conv_b": row(wt["ffn_conv_b"][i]), "ffn_down": wt["ffn_down"][i].astype(BF16),
          "ln3": row(wt["ln3"][i]), "ple_gate": wt["ple_gate"][i].astype(BF16),
          "ple_proj": wt["ple_proj"][i].astype(BF16), "ln_f": row(wt["ln_f"])}
    w_in = wt["w_in"][i].astype(BF16)
    seg = lambda j: w_in[:, _IN_SEGS[j][0]:_IN_SEGS[j][1]]
    wk_heads = seg(3).reshape(D_MODEL, ATT_HEADS, ATT_HEAD).transpose(1, 0, 2)
    pw = {"w_in": w_in, "w_qkv_t": jnp.stack([seg(2).T, seg(3).T, seg(4).T]),
          "wk_aug": jnp.pad(wk_heads, ((0, 0), (0, 0), (0, LANES - ATT_HEAD)))}
    return {"ln1": row(wt["ln1"][i]), "w_in": w_in, "proj": pw, "lru": lw, "rwkv": rw, "merge": mw, "ffn": fw}


def _prompt_layer(x, p, lp, rel_bias, final_norm, layer=0, depth=1, kv_bufs=None):
    b, t, _ = x.shape
    m = b * t
    nh, hd = RWKV_HEADS, RWKV_HEAD
    lru_c, rw_c, gates, q_t, k_buf, v_buf, k_aug = _in_proj_seq(x.reshape(m, D_MODEL), lp["ln1"], lp["proj"], b, t,
                                                                 layer, depth, kv_bufs)

    ya, conv_new, h_new = _lru_seq(lru_c.reshape(b, t, 2 * BRANCH_W), lp["lru"], tc=512)

    r_, w_, k_, v_, rate, g_, bonus, inv_n = _rwkv_pre_seq(rw_c.reshape(b, t, RWKV_COLS), lp["rwkv"], tc=512)
    grp, kr = SCAN_GROUPS, SCAN_KEY_ROWS
    assert b * nh * grp == LANES

    def key_packed(z):
        return z.reshape(b, t, nh, grp, kr).transpose(1, 4, 3, 0, 2).reshape(t, kr, LANES)

    def row_major(z):
        return z.reshape(b, t, nh, hd // grp, grp).transpose(1, 3, 4, 0, 2).reshape(t, hd // grp, LANES)

    def param_packed(z):
        z = z.reshape(nh, grp, kr).transpose(2, 1, 0)
        return jnp.broadcast_to(z[:, :, None, :], (kr, grp, b, nh)).reshape(kr, LANES)

    inv_n = inv_n[:, :, :nh].transpose(1, 0, 2).reshape(t, 1, b * nh)
    inv_n = jnp.broadcast_to(inv_n, (t, grp, b * nh)).reshape(t, 1, LANES)
    y_l, s_l = _rwkv_scan(key_packed(r_), key_packed(w_), key_packed(k_), key_packed(rate), inv_n,
                          param_packed(lp["rwkv"]["k_k"]), param_packed(lp["rwkv"]["k_a"]), row_major(v_), tc=128)
    yr = y_l.reshape(t, hd // grp, grp, b, nh).transpose(3, 0, 4, 1, 2).reshape(m, BRANCH_W)
    s_l = s_l.reshape(-1, grp, kr, SUBLANES, grp, b, nh)
    s_l = jnp.stack([jnp.roll(s_l[:, :, :, :, c], c, axis=1) for c in range(grp)], axis=4)
    s_new = s_l.transpose(5, 6, 0, 3, 4, 1, 2).reshape(b, nh, hd, hd)

    yc_t = _moba_seq(q_t, k_aug, v_buf, layer, rel_bias)

    flat = lambda z: z.reshape(m, BRANCH_W)
    x1 = _merge(x.reshape(m, D_MODEL), flat(ya), yr, flat(bonus), flat(g_), yc_t, gates, lp["merge"], tm=512)
    x2, fc_new = _ffn(x1, p.reshape(m, PLE_DIM), lp["ffn"], tm=512, seq_len=t, final_norm=final_norm)
    new = {"kv_bufs": (k_buf, v_buf),
           "lru_h": h_new[:, 0, :], "lru_conv": conv_new, "rwkv": s_new,
           "rwkv_shift": rw_c.reshape(b, t, RWKV_COLS)[:, -1, :], "ffn_conv": fc_new}
    return x2.reshape(b, t, D_MODEL), new


def _sample_layer(x, p, st, lp, rel_bias, final_norm, s_out=None):
    n = x.shape[0]
    lru_c, rw_c, q, k, v, gates = _in_proj(x.reshape(n, D_MODEL), lp["ln1"], lp["w_in"], tm=n)

    ya, conv_new, h_new = _lru_step(lru_c, st["lru_conv"].reshape(n, (LRU_CONV - 1) * BRANCH_W), st["lru_h"], lp["lru"])

    r_, w_, k_, v_, a_, b_, g_, bonus = _rwkv_pre_step(rw_c, st["rwkv_shift"], lp["rwkv"])
    seq_last = lambda z: z.reshape(n, RWKV_HEADS, RWKV_HEAD).transpose(1, 2, 0)
    s_l, y_l = _rwkv_step(st["rwkv"].transpose(0, 2, 3, 4, 1), st["layer"], seq_last(r_), seq_last(w_), seq_last(k_),
                          seq_last(a_), seq_last(b_), seq_last(v_), s_out)
    yr = y_l.transpose(2, 0, 1).reshape(n, BRANCH_W)

    yc = _moba_step(q, k, v, st["cache_k"], st["cache_v"], st["layer"], st["page_table"], rel_bias)

    x1 = _merge(x.reshape(n, D_MODEL), ya, yr, bonus, g_, yc, gates, lp["merge"], tm=n)
    fc_prev = st["ffn_conv"].reshape(n, (FFN_CONV - 1) * 2 * D_FF)
    x2, u = _ffn(x1, p.reshape(n, PLE_DIM), lp["ffn"], tm=n, seq_len=None, prev=fc_prev, final_norm=final_norm)
    new = {"k": k.reshape(n, 1, ATT_HEADS, ATT_HEAD), "v": v.reshape(n, 1, ATT_HEADS, ATT_HEAD),
           "lru_h": h_new, "lru_conv": conv_new.reshape(n, LRU_CONV - 1, BRANCH_W), "rwkv_buf": s_l,
           "rwkv_shift": rw_c, "ffn_conv": jnp.stack([st["ffn_conv"][:, 1, :], u], axis=1)}
    return x2.reshape(n, 1, D_MODEL), new


def kernel(x_prompt, x_sample, cache_k, cache_v, state_lru_h, state_lru_conv, state_rwkv, state_rwkv_shift, state_ffn_conv, page_table, p_prompt, p_sample, ln1, w_in, lru_conv_w, lru_conv_b, lru_w_r, lru_b_r, lru_w_i, lru_b_i, lru_lambda, rwkv_mu, rwkv_w0, rwkv_w2, rwkv_a0, rwkv_a2, rwkv_g2, rwkv_k_k, rwkv_k_a, rwkv_r_k, rwkv_ln_w, rwkv_ln_b, rel_bias, w_branch, w_out, ln2, ffn_up, ffn_conv_w, ffn_conv_b, ffn_down, ln3, ple_gate, ple_proj, ln_f):
    wt = dict(ln1=ln1, w_in=w_in, lru_conv_w=lru_conv_w, lru_conv_b=lru_conv_b, lru_w_r=lru_w_r, lru_b_r=lru_b_r,
              lru_w_i=lru_w_i, lru_b_i=lru_b_i, lru_lambda=lru_lambda, rwkv_mu=rwkv_mu, rwkv_w0=rwkv_w0,
              rwkv_w2=rwkv_w2, rwkv_a0=rwkv_a0, rwkv_a2=rwkv_a2, rwkv_g2=rwkv_g2, rwkv_k_k=rwkv_k_k,
              rwkv_k_a=rwkv_k_a, rwkv_r_k=rwkv_r_k, rwkv_ln_w=rwkv_ln_w, rwkv_ln_b=rwkv_ln_b, w_branch=w_branch,
              w_out=w_out, ln2=ln2, ffn_up=ffn_up, ffn_conv_w=ffn_conv_w, ffn_conv_b=ffn_conv_b, ffn_down=ffn_down,
              ln3=ln3, ple_gate=ple_gate, ple_proj=ple_proj, ln_f=ln_f)
    depth = w_in.shape[0]
    b, t = x_prompt.shape[:2]
    xp, xs = x_prompt, x_sample[:, 0, :]
    outs_p, outs_s = [], []
    kv_bufs, s_buf = None, None
    for i in range(depth):
        lp = _layer_weights(i, wt)
        last = i == depth - 1
        st = {"cache_k": cache_k, "cache_v": cache_v, "layer": i, "page_table": page_table,
              "lru_h": state_lru_h[i], "lru_conv": state_lru_conv[i], "rwkv": state_rwkv,
              "rwkv_shift": state_rwkv_shift[i], "ffn_conv": state_ffn_conv[i]}
        xp, new_p = _prompt_layer(xp, p_prompt[i], lp, rel_bias, last, i, depth, kv_bufs)
        xs3, new_s = _sample_layer(xs, p_sample[i], st, lp, rel_bias, last, s_buf)
        xs = xs3[:, 0, :]
        kv_bufs, s_buf = new_p["kv_bufs"], new_s["rwkv_buf"]
        outs_p.append(new_p)
        outs_s.append(new_s)
    stack = lambda outs, name: jnp.stack([o[name] for o in outs])
    tokens_major = lambda z: z.reshape(depth, b, ATT_HEADS, ATT_HEAD, t).transpose(0, 1, 4, 2, 3)
    res = [xp, xs[:, None, :],
           tokens_major(kv_bufs[0]), stack(outs_s, "k"), tokens_major(kv_bufs[1]), stack(outs_s, "v")]
    for name in ("lru_h", "lru_conv", "rwkv", "rwkv_shift", "ffn_conv"):
        res.append(stack(outs_p, name))
        res.append(s_buf.transpose(0, 4, 1, 2, 3) if name == "rwkv" else stack(outs_s, name))
    return tuple(res)
```
